```python
import numpy as np
import jax, jax.numpy as jnp
from jax import lax

D_MODEL = 1024
BATCH = 1
SEQ = 16384
DEPTH = 2

HEAD_DIM = 64
N_A_LAYERS = DEPTH // 2
N_B_LAYERS = DEPTH - N_A_LAYERS
MEM_LEN = 256
MEM_HEADS = 4
MEM_W = MEM_HEADS * HEAD_DIM
CONV_W = D_MODEL - MEM_W
CONV_K = 31
NSA_HEADS = (D_MODEL - MEM_W) // HEAD_DIM
NSA_W = NSA_HEADS * HEAD_DIM
NSA_KV_GROUPS = 2
HEADS_PER_GROUP = NSA_HEADS // NSA_KV_GROUPS
CMP_L = 32
CMP_STRIDE = 16
CMP_HID = 256
SEL_L = 64
N_SEL = 16
WIN = 512
Q_BLOCK = 128
D_FF = 4 * D_MODEL
KV_W = 6 * NSA_KV_GROUPS * HEAD_DIM
A_IN_W = 2 * CONV_W + MEM_W
B_IN_W = NSA_W + MEM_W + 3 * NSA_HEADS

kernel_name = 'yoco_conformer_nsa_hybrid'


def rms_norm(x, g, eps=1e-6):
    xf = x.astype(jnp.float32)
    y = xf * lax.rsqrt(jnp.mean(xf * xf, axis=-1, keepdims=True) + eps)
    return (y * g.astype(jnp.float32)).astype(x.dtype)


def layer_norm(x, g, b, eps=1e-5):
    xf = x.astype(jnp.float32)
    mu = jnp.mean(xf, axis=-1, keepdims=True)
    var = jnp.mean(jnp.square(xf - mu), axis=-1, keepdims=True)
    y = (xf - mu) * lax.rsqrt(var + eps)
    return (y * g.astype(jnp.float32) + b.astype(jnp.float32)).astype(x.dtype)


def masked_softmax(s, mask):
    s = jnp.where(mask, s.astype(jnp.float32), -jnp.inf)
    m = jnp.max(s, axis=-1, keepdims=True)
    m = jnp.where(jnp.isfinite(m), m, 0.0)
    e = jnp.exp(s - m)
    return e / jnp.maximum(jnp.sum(e, axis=-1, keepdims=True), 1e-30)


def selection_map(n_cmp, n_sb):
    cs = np.arange(n_cmp)[:, None] * CMP_STRIDE
    ss = np.arange(n_sb)[None, :] * SEL_L
    ov = np.minimum(cs + CMP_L, ss + SEL_L) - np.maximum(cs, ss)
    return (np.clip(ov, 0, None) / CMP_STRIDE).astype(np.float32)


def squared_relu_mlp(h, w_in, w_out):
    return jnp.square(jax.nn.relu(h @ w_in)) @ w_out


def memory_attention(q, mk, mv):
    B, S = q.shape[:2]
    s = jnp.einsum('bshd,bmhd->bhsm', q, mk) * HEAD_DIM ** -0.5
    p = jax.nn.softmax(s.astype(jnp.float32), axis=-1)
    return jnp.einsum('bhsm,bmhd->bshd', p.astype(mv.dtype), mv).reshape(B, S, MEM_W)


def conformer_conv(u, dw, dw_b, ln_g, ln_b):
    a, gate = jnp.split(u, 2, axis=-1)
    v = a * jax.nn.sigmoid(gate)
    v = lax.conv_general_dilated(v, dw[:, None, :], window_strides=(1,),
                                 padding=[(CONV_K - 1, 0)],
                                 dimension_numbers=('NWC', 'WIO', 'NWC'),
                                 feature_group_count=CONV_W) + dw_b
    return jax.nn.silu(layer_norm(v, ln_g, ln_b))


def shared_nsa_kv(h, kv_norm_g, w_kv, k_norm_g, pe_k, pe_v, w1_k, w2_k, w1_v, w2_v):
    B, S = h.shape[:2]
    G, HD = NSA_KV_GROUPS, HEAD_DIM
    kv = (rms_norm(h, kv_norm_g) @ w_kv).reshape(B, S, 6, G, HD)
    k_c, v_c, k_s, v_s, k_w, v_w = (kv[:, :, i] for i in range(6))
    n_cmp = (S - CMP_L) // CMP_STRIDE + 1
    idx = np.arange(n_cmp)[:, None] * CMP_STRIDE + np.arange(CMP_L)[None, :]

    def compress(t, pe, w1, w2):
        blk = t[:, idx] + pe[:, None, :]
        blk = blk.transpose(0, 1, 3, 2, 4).reshape(B, n_cmp, G, CMP_L * HD)
        return jax.nn.gelu(blk @ w1) @ w2

    k_cmp = rms_norm(compress(k_c, pe_k, w1_k, w2_k), k_norm_g[0])
    v_cmp = compress(v_c, pe_v, w1_v, w2_v)
    n_sb = S // SEL_L
    k_blk = rms_norm(k_s, k_norm_g[1]).reshape(B, n_sb, SEL_L, G, HD).transpose(0, 3, 1, 2, 4)
    v_blk = v_s.reshape(B, n_sb, SEL_L, G, HD).transpose(0, 3, 1, 2, 4)
    pad = ((0, 0), (WIN, 0), (0, 0), (0, 0))
    k_win = jnp.pad(rms_norm(k_w, k_norm_g[2]), pad)
    v_win = jnp.pad(v_w, pad)
    return k_cmp, v_cmp, k_blk, v_blk, k_win, v_win


def nsa_attention(q, gates, k_cmp, v_cmp, k_blk, v_blk, k_win, v_win):
    B, S = q.shape[:2]
    G, HG, HD = NSA_KV_GROUPS, HEADS_PER_GROUP, HEAD_DIM
    n_qb = S // Q_BLOCK
    n_cmp = k_cmp.shape[1]
    n_sb = k_blk.shape[2]
    n_sel = min(N_SEL, n_sb)
    scale = HD ** -0.5
    cmp_end = jnp.asarray(np.arange(n_cmp) * CMP_STRIDE + CMP_L - 1)
    sel_map = jnp.asarray(selection_map(n_cmp, n_sb))
    blk_ids = jnp.arange(n_sb)
    b_ix = jnp.arange(B)[:, None, None, None]
    g_ix = jnp.arange(G)[None, None, :, None]
    qs = q.reshape(B, n_qb, Q_BLOCK, G, HG, HD).transpose(1, 0, 2, 3, 4, 5)
    gs = gates.reshape(B, n_qb, Q_BLOCK, G, HG, 3).transpose(1, 0, 2, 3, 4, 5)

    def block(args):
        c, qb, gb = args
        t = c * Q_BLOCK + jnp.arange(Q_BLOCK)
        s = jnp.einsum('bqghd,bngd->bqghn', qb, k_cmp) * scale
        p_cmp = masked_softmax(s, (cmp_end[None, :] <= t[:, None])[None, :, None, None, :])
        o_cmp = jnp.einsum('bqghn,bngd->bqghd', p_cmp.astype(v_cmp.dtype), v_cmp)
        imp = jnp.einsum('bqgn,nj->bqgj', jnp.sum(p_cmp, axis=3), sel_map)
        tb = (t // SEL_L)[:, None]
        valid = (blk_ids[None, :] <= tb)[None, :, None, :]
        forced = ((blk_ids[None, :] == 0) | (blk_ids[None, :] == tb)
                  | (blk_ids[None, :] == tb - 1))[None, :, None, :]
        score = jnp.where(forced, jnp.inf, jnp.where(valid, imp, -jnp.inf))
        top_val, top_idx = lax.top_k(score, n_sel)
        kg = k_blk[b_ix, g_ix, top_idx]
        vg = v_blk[b_ix, g_ix, top_idx]
        pos = top_idx[..., None] * SEL_L + jnp.arange(SEL_L)
        smask = (top_val > -jnp.inf)[..., None] & (pos <= t[None, :, None, None, None])
        s = jnp.einsum('bqghd,bqgnkd->bqghnk', qb, kg) * scale
        n_tok = n_sel * SEL_L
        p = masked_softmax(s.reshape(B, Q_BLOCK, G, HG, n_tok),
                           smask.reshape(B, Q_BLOCK, G, 1, n_tok))
        o_slc = jnp.einsum('bqghm,bqgmd->bqghd', p.astype(vg.dtype),
                           vg.reshape(B, Q_BLOCK, G, n_tok, HD))
        kw = lax.dynamic_slice_in_dim(k_win, c * Q_BLOCK, Q_BLOCK + WIN, axis=1)
        vw = lax.dynamic_slice_in_dim(v_win, c * Q_BLOCK, Q_BLOCK + WIN, axis=1)
        kp = c * Q_BLOCK - WIN + jnp.arange(Q_BLOCK + WIN)
        wmask = (kp[None, :] <= t[:, None]) & (kp[None, :] > t[:, None] - WIN) & (kp[None, :] >= 0)
        s = jnp.einsum('bqghd,bkgd->bqghk', qb, kw) * scale
        p = masked_softmax(s, wmask[None, :, None, None, :])
        o_win = jnp.einsum('bqghk,bkgd->bqghd', p.astype(vw.dtype), vw)
        o = o_cmp * gb[..., 0:1] + o_slc * gb[..., 1:2] + o_win * gb[..., 2:3]
        return o.reshape(B, Q_BLOCK, NSA_W)

    out = lax.map(block, (jnp.arange(n_qb), qs, gs))
    return out.transpose(1, 0, 2, 3).reshape(B, S, NSA_W)


def setup_inputs(seed: int = 0) -> dict:
    key = jax.random.key(seed)
    ks = iter(jax.random.split(key, 32))

    def w(shape, fan_in):
        return jax.random.normal(next(ks), shape, jnp.float32) * fan_in ** -0.5

    def gain(shape):
        return 1.0 + 0.1 * jax.random.normal(next(ks), shape, jnp.float32)

    def bias(shape):
        return 0.01 * jax.random.normal(next(ks), shape, jnp.float32)

    return {
        'x': jax.random.normal(next(ks), (BATCH, SEQ, D_MODEL), jnp.float32),
        'mem': jax.random.normal(next(ks), (BATCH, MEM_LEN, D_MODEL), jnp.float32),
        'norm_mix_g': gain((DEPTH, D_MODEL)),
        'norm_mlp_g': gain((DEPTH, D_MODEL)),
        'mem_norm_g': gain((D_MODEL,)),
        'w_mem_kv': w((DEPTH, D_MODEL, 2 * MEM_W), D_MODEL),
        'mem_q_norm_g': gain((DEPTH, HEAD_DIM)),
        'mem_k_norm_g': gain((DEPTH, HEAD_DIM)),
        'w_out': w((DEPTH, D_MODEL, D_MODEL), D_MODEL),
        'w_mlp_in': w((DEPTH, D_MODEL, D_FF), D_MODEL),
        'w_mlp_out': w((DEPTH, D_FF, D_MODEL), D_FF),
        'a_w_in': w((N_A_LAYERS, D_MODEL, A_IN_W), D_MODEL),
        'a_b_glu': bias((N_A_LAYERS, 2 * CONV_W)),
        'a_dw': w((N_A_LAYERS, CONV_K, CONV_W), CONV_K),
        'a_dw_b': bias((N_A_LAYERS, CONV_W)),
        'a_ln_g': gain((N_A_LAYERS, CONV_W)),
        'a_ln_b': bias((N_A_LAYERS, CONV_W)),
        'b_w_in': w((N_B_LAYERS, D_MODEL, B_IN_W), D_MODEL),
        'b_gate_b': bias((N_B_LAYERS, 3 * NSA_HEADS)),
        'b_q_norm_g': gain((N_B_LAYERS, HEAD_DIM)),
        'kv_norm_g': gain((D_MODEL,)),
        'w_kv': w((D_MODEL, KV_W), D_MODEL),
        'k_norm_g': gain((3, HEAD_DIM)),
        'cmp_pe_k': 0.1 * jax.random.normal(next(ks), (CMP_L, HEAD_DIM), jnp.float32),
        'cmp_pe_v': 0.1 * jax.random.normal(next(ks), (CMP_L, HEAD_DIM), jnp.float32),
        'cmp_w1_k': w((CMP_L * HEAD_DIM, CMP_HID), CMP_L * HEAD_DIM),
        'cmp_w2_k': w((CMP_HID, HEAD_DIM), CMP_HID),
        'cmp_w1_v': w((CMP_L * HEAD_DIM, CMP_HID), CMP_L * HEAD_DIM),
        'cmp_w2_v': w((CMP_HID, HEAD_DIM), CMP_HID),
    }


def reference(x, mem, norm_mix_g, norm_mlp_g, mem_norm_g, w_mem_kv, mem_q_norm_g, mem_k_norm_g,
              w_out, w_mlp_in, w_mlp_out, a_w_in, a_b_glu, a_dw, a_dw_b, a_ln_g, a_ln_b,
              b_w_in, b_gate_b, b_q_norm_g, kv_norm_g, w_kv, k_norm_g, cmp_pe_k, cmp_pe_v,
              cmp_w1_k, cmp_w2_k, cmp_w1_v, cmp_w2_v):
    B, S = x.shape[:2]
    mem_n = rms_norm(mem, mem_norm_g)
    shared = None
    for l in range(DEPTH):
        mkv = (mem_n @ w_mem_kv[l]).reshape(B, MEM_LEN, 2, MEM_HEADS, HEAD_DIM)
        mk = rms_norm(mkv[:, :, 0], mem_k_norm_g[l])
        mv = mkv[:, :, 1]
        h = rms_norm(x, norm_mix_g[l])
        if l == N_A_LAYERS:
            shared = shared_nsa_kv(x, kv_norm_g, w_kv, k_norm_g, cmp_pe_k, cmp_pe_v,
                                   cmp_w1_k, cmp_w2_k, cmp_w1_v, cmp_w2_v)
        if l < N_A_LAYERS:
            i = l
            u = h @ a_w_in[i]
            conv_out = conformer_conv(u[..., :2 * CONV_W] + a_b_glu[i], a_dw[i], a_dw_b[i],
                                      a_ln_g[i], a_ln_b[i])
            qm = rms_norm(u[..., 2 * CONV_W:].reshape(B, S, MEM_HEADS, HEAD_DIM), mem_q_norm_g[l])
            mix = jnp.concatenate([conv_out, memory_attention(qm, mk, mv)], axis=-1)
        else:
            i = l - N_A_LAYERS
            u = h @ b_w_in[i]
            q = rms_norm(u[..., :NSA_W].reshape(B, S, NSA_HEADS, HEAD_DIM), b_q_norm_g[i])
            qm = rms_norm(u[..., NSA_W:NSA_W + MEM_W].reshape(B, S, MEM_HEADS, HEAD_DIM),
                          mem_q_norm_g[l])
            gates = jax.nn.sigmoid(u[..., NSA_W + MEM_W:] + b_gate_b[i]).reshape(B, S, NSA_HEADS, 3)
            nsa_out = nsa_attention(q, gates, *shared)
            mix = jnp.concatenate([nsa_out, memory_attention(qm, mk, mv)], axis=-1)
        x = x + mix @ w_out[l]
        x = x + squared_relu_mlp(rms_norm(x, norm_mlp_g[l]), w_mlp_in[l], w_mlp_out[l])
    return x
```

```python
import functools

import numpy as np
import jax
import jax.numpy as jnp
from jax import lax
from jax.experimental import pallas as pl
from jax.experimental.pallas import tpu as pltpu

F32 = jnp.float32
BF16 = jnp.bfloat16

D_MODEL = 1024
HEAD_DIM = 64
MEM_LEN = 256
MEM_HEADS = 4
MEM_W = MEM_HEADS * HEAD_DIM
CONV_W = D_MODEL - MEM_W
CONV_K = 31
NSA_HEADS = CONV_W // HEAD_DIM
NSA_W = NSA_HEADS * HEAD_DIM
GROUPS = 2
HEADS_PER_GROUP = NSA_HEADS // GROUPS
CMP_L = 32
CMP_STRIDE = 16
CMP_HID = 256
SEL_L = 64
N_SEL = 16
WIN = 512
Q_BLOCK = 128
D_FF = 4 * D_MODEL
GATE_PAD = 128
GATE_ROWS = GATE_PAD // GROUPS

ROW_TILE = 512
FF_CHUNK = 1024
KEY_TILE = 1024
SEL_PER_TILE = KEY_TILE // SEL_L
WIN_KEYS = WIN + Q_BLOCK
CONV_HALO = 32
CONV_ROWS = 64
VAUG_ROWS = 80
VMEM_LIMIT = 56 * 1024 * 1024

EPS = 1e-6
LN_EPS = 1e-5
LOG2E = 1.4426950408889634
QK_SCALE = HEAD_DIM ** -0.5
MASK_BIAS = -2.0 ** 126
M_INIT = -2.0 ** 100


def _dot(a, b):
    return jnp.dot(a, b, preferred_element_type=F32)


def _rms_scale(x):
    return lax.rsqrt(jnp.mean(x * x, axis=-1, keepdims=True) + EPS)


def _head_meansq(x, bd):
    x2 = x * x
    hi = x2.astype(BF16)
    lo = (x2 - hi.astype(F32)).astype(BF16)
    return (_dot(hi, bd) + _dot(lo, bd)) * (1.0 / HEAD_DIM)


def _mem_attention(qn, mkT, mv, hm_ref):
    out = jnp.zeros(qn.shape, F32)
    for h in range(MEM_HEADS):
        hm = hm_ref[h:h + 1, :]
        s = _dot((qn * hm).astype(BF16), mkT)
        e = jnp.exp2(s - jnp.max(s, axis=-1, keepdims=True))
        l = jnp.sum(e, axis=-1, keepdims=True)
        out = out + _dot(e.astype(BF16), mv) * (hm / l)
    return out


def _memkv_kernel(mem_ref, g_ref, w_ref, kg_ref, bd_ref, mkT_ref, mv_ref):
    m = mem_ref[...]
    mn = (m * _rms_scale(m) * g_ref[...]).astype(BF16)
    kv = _dot(mn, w_ref[0])
    k = kv[:, :MEM_W]
    kn = k * lax.rsqrt(_head_meansq(k, bd_ref[...]) + EPS) * kg_ref[0]
    mkT_ref[0] = (kn * (QK_SCALE * LOG2E)).T.astype(BF16)
    mv_ref[0] = kv[:, MEM_W:].astype(BF16)


def _memkv(mem, g, w, kg, bd):
    depth = w.shape[0]
    return pl.pallas_call(
        _memkv_kernel,
        grid=(depth,),
        in_specs=[
            pl.BlockSpec((MEM_LEN, D_MODEL), lambda l: (0, 0)),
            pl.BlockSpec((1, D_MODEL), lambda l: (0, 0)),
            pl.BlockSpec((1, D_MODEL, 2 * MEM_W), lambda l: (l, 0, 0)),
            pl.BlockSpec((1, 1, MEM_W), lambda l: (l, 0, 0)),
            pl.BlockSpec((MEM_W, MEM_W), lambda l: (0, 0)),
        ],
        out_specs=[
            pl.BlockSpec((1, MEM_W, MEM_LEN), lambda l: (l, 0, 0)),
            pl.BlockSpec((1, MEM_LEN, MEM_W), lambda l: (l, 0, 0)),
        ],
        out_shape=[
            jax.ShapeDtypeStruct((depth, MEM_W, MEM_LEN), BF16),
            jax.ShapeDtypeStruct((depth, MEM_LEN, MEM_W), BF16),
        ],
        name="mem_kv",
    )(mem, g, w, kg, bd)


def _layer0_kernel(x_ref, g_ref, win_ref, bglu_ref, dw_ref, dwb_ref, lng_ref, lnb_ref, qg_ref,
                   mkT_ref, mv_ref, bd_ref, hm_ref, conv_ref, memo_ref, buf_ref, cv_ref, sh_ref):
    i = pl.program_id(0)
    tm = x_ref.shape[0]
    x = x_ref[...]
    h = (x * _rms_scale(x) * g_ref[...]).astype(BF16)
    u = _dot(h, win_ref[...])
    a = u[:, :CONV_W] + bglu_ref[:, :CONV_W]
    gate = u[:, CONV_W:2 * CONV_W] + bglu_ref[:, CONV_W:]
    v = a * jax.nn.sigmoid(gate)

    @pl.when(i == 0)
    def _():
        buf_ref[0:CONV_HALO, :] = jnp.zeros((CONV_HALO, CONV_W), F32)

    buf_ref[CONV_HALO:CONV_HALO + tm, :] = v

    base = CONV_HALO - (CONV_K - 1)
    rows = tm + CONV_HALO
    for cb in range(CONV_W // 128):
        cs = slice(cb * 128, (cb + 1) * 128)
        xb = buf_ref[:, cs]
        sh_ref[0] = xb
        for b in range(1, 8):
            sh_ref[b] = pltpu.roll(xb, rows - b, axis=0)

        def conv_rows(r, carry, cs=cs):
            r0 = pl.multiple_of(r * CONV_ROWS, CONV_ROWS)
            acc = jnp.zeros((CONV_ROWS, 128), F32) + dwb_ref[:, cs]
            for k in range(CONV_K):
                a, b = divmod(base + k, 8)
                acc = acc + dw_ref[k:k + 1, cs] * sh_ref[b, pl.ds(r0 + 8 * a, CONV_ROWS), :]
            cv_ref[pl.ds(r0, CONV_ROWS), cs] = acc
            return carry

        lax.fori_loop(0, tm // CONV_ROWS, conv_rows, 0)
    buf_ref[0:CONV_HALO, :] = buf_ref[tm:tm + CONV_HALO, :]

    cv = cv_ref[...]
    mu = jnp.mean(cv, axis=-1, keepdims=True)
    d = cv - mu
    var = jnp.mean(d * d, axis=-1, keepdims=True)
    y = d * lax.rsqrt(var + LN_EPS) * lng_ref[...] + lnb_ref[...]
    conv_ref[...] = (y * jax.nn.sigmoid(y)).astype(BF16)

    qm = u[:, 2 * CONV_W:]
    qn = qm * lax.rsqrt(_head_meansq(qm, bd_ref[...]) + EPS) * qg_ref[...]
    memo_ref[...] = _mem_attention(qn, mkT_ref[0], mv_ref[0], hm_ref).astype(BF16)


def _layer0(x, g, w_in, bglu, dw, dwb, lng, lnb, qg, mkT, mv, bd, hm):
    s = x.shape[0]
    tm = ROW_TILE
    const = lambda i: (0, 0)
    return pl.pallas_call(
        _layer0_kernel,
        grid=(s // tm,),
        in_specs=[
            pl.BlockSpec((tm, D_MODEL), lambda i: (i, 0)),
            pl.BlockSpec((1, D_MODEL), const),
            pl.BlockSpec(w_in.shape, const),
            pl.BlockSpec((1, 2 * CONV_W), const),
            pl.BlockSpec((CONV_K, CONV_W), const),
            pl.BlockSpec((1, CONV_W), const),
            pl.BlockSpec((1, CONV_W), const),
            pl.BlockSpec((1, CONV_W), const),
            pl.BlockSpec((1, MEM_W), const),
            pl.BlockSpec((1, MEM_W, MEM_LEN), lambda i: (0, 0, 0)),
            pl.BlockSpec((1, MEM_LEN, MEM_W), lambda i: (0, 0, 0)),
            pl.BlockSpec((MEM_W, MEM_W), const),
            pl.BlockSpec((MEM_HEADS, MEM_W), const),
        ],
        out_specs=[
            pl.BlockSpec((tm, CONV_W), lambda i: (i, 0)),
            pl.BlockSpec((tm, MEM_W), lambda i: (i, 0)),
        ],
        out_shape=[
            jax.ShapeDtypeStruct((s, CONV_W), BF16),
            jax.ShapeDtypeStruct((s, MEM_W), BF16),
        ],
        scratch_shapes=[
            pltpu.VMEM((tm + CONV_HALO, CONV_W), F32),
            pltpu.VMEM((tm, CONV_W), F32),
            pltpu.VMEM((8, tm + CONV_HALO, 128), F32),
        ],
        compiler_params=pltpu.CompilerParams(
            dimension_semantics=("arbitrary",), vmem_limit_bytes=VMEM_LIMIT),
        name="layer0_mixer",
    )(x, g, w_in, bglu, dw, dwb, lng, lnb, qg, mkT, mv, bd, hm)


def _post_kernel(x_ref, ma_ref, mb_ref, wa_ref, wb_ref, g_ref, win_ref, wout_ref, o_ref):
    o_ref[...] = x_ref[...] + _dot(ma_ref[...], wa_ref[...]) + _dot(mb_ref[...], wb_ref[...])
    x1 = o_ref[...]
    h = (x1 * _rms_scale(x1) * g_ref[...]).astype(BF16)
    for c in range(D_FF // FF_CHUNK):
        cs = slice(c * FF_CHUNK, (c + 1) * FF_CHUNK)
        t = jnp.maximum(_dot(h, win_ref[:, cs]), 0.0)
        o_ref[...] += _dot((t * t).astype(BF16), wout_ref[cs, :])


def _post(x, ma, mb, wa, wb, g, w_in, w_out):
    s = x.shape[0]
    tm = ROW_TILE
    const = lambda i: (0, 0)
    single = pl.Buffered(1)
    return pl.pallas_call(
        _post_kernel,
        grid=(s // tm,),
        in_specs=[
            pl.BlockSpec((tm, D_MODEL), lambda i: (i, 0)),
            pl.BlockSpec((tm, CONV_W), lambda i: (i, 0)),
            pl.BlockSpec((tm, MEM_W), lambda i: (i, 0)),
            pl.BlockSpec((CONV_W, D_MODEL), const, pipeline_mode=single),
            pl.BlockSpec((MEM_W, D_MODEL), const, pipeline_mode=single),
            pl.BlockSpec((1, D_MODEL), const),
            pl.BlockSpec((D_MODEL, D_FF), const, pipeline_mode=single),
            pl.BlockSpec((D_FF, D_MODEL), const, pipeline_mode=single),
        ],
        out_specs=pl.BlockSpec((tm, D_MODEL), lambda i: (i, 0)),
        out_shape=jax.ShapeDtypeStruct((s, D_MODEL), F32),
        compiler_params=pltpu.CompilerParams(
            dimension_semantics=("arbitrary",), vmem_limit_bytes=VMEM_LIMIT),
        name="outproj_mlp",
    )(x, ma, mb, wa, wb, g, w_in, w_out)


def _layer1_proj_kernel(x_ref, gkv_ref, gmix_ref, wkv_ref, win_ref, gb_ref, qg_ref, mqg_ref, kng_ref,
                        mkT_ref, mv_ref, bd_ref, hm_ref, pat_ref,
                        qT_ref, gT_ref, ksel_ref, kwin_ref, vselT_ref, vwinT_ref, kvc_ref, memo_ref):
    tm = x_ref.shape[0]
    x = x_ref[...]
    xn = x * _rms_scale(x)
    bd = bd_ref[...]

    kv = _dot((xn * gkv_ref[...]).astype(BF16), wkv_ref[...])
    kvc_ref[...] = kv[:, 0:256].astype(BF16)
    k2 = kv[:, 256:384]
    v2 = kv[:, 384:512]
    kw = kv[:, 512:640]
    vw = kv[:, 640:768]
    bd2 = bd[0:128, 0:128]
    k2n = k2 * lax.rsqrt(_head_meansq(k2, bd2) + EPS) * kng_ref[0:1, :]
    kwn = kw * lax.rsqrt(_head_meansq(kw, bd2) + EPS) * kng_ref[1:2, :]

    lane = lax.broadcasted_iota(jnp.int32, (tm, 128), 1)
    pat = pat_ref[...]
    ksel_ref[0] = jnp.where(lane < HEAD_DIM, k2n, pat).astype(BF16)
    ksel_ref[1] = jnp.where(lane < HEAD_DIM, pltpu.roll(k2n, HEAD_DIM, axis=1), pat).astype(BF16)
    kwin_ref[0] = kwn[:, :HEAD_DIM].astype(BF16)
    kwin_ref[1] = kwn[:, HEAD_DIM:].astype(BF16)

    ones_rows = (lax.broadcasted_iota(jnp.int32, (VAUG_ROWS - HEAD_DIM, tm), 0) == 0).astype(BF16)
    v2T = v2.T.astype(BF16)
    vwT = vw.T.astype(BF16)
    for g in range(GROUPS):
        vselT_ref[g, 0:HEAD_DIM, :] = v2T[g * HEAD_DIM:(g + 1) * HEAD_DIM, :]
        vselT_ref[g, HEAD_DIM:VAUG_ROWS, :] = ones_rows
        vwinT_ref[g] = vwT[g * HEAD_DIM:(g + 1) * HEAD_DIM, :]

    u = _dot((xn * gmix_ref[...]).astype(BF16), win_ref[...])
    for cb in range(NSA_W // 256):
        cs = slice(cb * 256, (cb + 1) * 256)
        qc = u[:, cs]
        qn = qc * lax.rsqrt(_head_meansq(qc, bd) + EPS) * qg_ref[:, cs]
        qT_ref[4 * cb:4 * cb + 4] = qn.T.astype(BF16).reshape(4, HEAD_DIM, tm)

    gates = jax.nn.sigmoid(u[:, NSA_W + MEM_W:] + gb_ref[...])
    gT_ref[...] = gates.T

    qm = u[:, NSA_W:NSA_W + MEM_W]
    qmn = qm * lax.rsqrt(_head_meansq(qm, bd) + EPS) * mqg_ref[...]
    memo_ref[...] = _mem_attention(qmn, mkT_ref[0], mv_ref[0], hm_ref).astype(BF16)


def _layer1_proj(x, gkv, gmix, wkv, w_in, gb, qg, mqg, kng, mkT, mv, bd, hm, pat):
    s = x.shape[0]
    tm = ROW_TILE
    const = lambda i: (0, 0)
    pat_blocks = pat.shape[0] // tm
    return pl.pallas_call(
        _layer1_proj_kernel,
        grid=(s // tm,),
        in_specs=[
            pl.BlockSpec((tm, D_MODEL), lambda i: (i, 0)),
            pl.BlockSpec((1, D_MODEL), const),
            pl.BlockSpec((1, D_MODEL), const),
            pl.BlockSpec(wkv.shape, const),
            pl.BlockSpec(w_in.shape, const),
            pl.BlockSpec((1, GATE_PAD), const),
            pl.BlockSpec((1, NSA_W), const),
            pl.BlockSpec((1, MEM_W), const),
            pl.BlockSpec((2, 128), const),
            pl.BlockSpec((1, MEM_W, MEM_LEN), lambda i: (1, 0, 0)),
            pl.BlockSpec((1, MEM_LEN, MEM_W), lambda i: (1, 0, 0)),
            pl.BlockSpec((MEM_W, MEM_W), const),
            pl.BlockSpec((MEM_HEADS, MEM_W), const),
            pl.BlockSpec((tm, 128), lambda i: (i % pat_blocks, 0)),
        ],
        out_specs=[
            pl.BlockSpec((NSA_HEADS, HEAD_DIM, tm), lambda i: (0, 0, i)),
            pl.BlockSpec((GATE_PAD, tm), lambda i: (0, i)),
            pl.BlockSpec((GROUPS, tm, 128), lambda i: (0, i, 0)),
            pl.BlockSpec((GROUPS, tm, HEAD_DIM), lambda i: (0, i, 0)),
            pl.BlockSpec((GROUPS, VAUG_ROWS, tm), lambda i: (0, 0, i)),
            pl.BlockSpec((GROUPS, HEAD_DIM, tm), lambda i: (0, 0, i)),
            pl.BlockSpec((tm, 256), lambda i: (i, 0)),
            pl.BlockSpec((tm, MEM_W), lambda i: (i, 0)),
        ],
        out_shape=[
            jax.ShapeDtypeStruct((NSA_HEADS, HEAD_DIM, s), BF16),
            jax.ShapeDtypeStruct((GATE_PAD, s), F32),
            jax.ShapeDtypeStruct((GROUPS, s, 128), BF16),
            jax.ShapeDtypeStruct((GROUPS, s, HEAD_DIM), BF16),
            jax.ShapeDtypeStruct((GROUPS, VAUG_ROWS, s), BF16),
            jax.ShapeDtypeStruct((GROUPS, HEAD_DIM, s), BF16),
            jax.ShapeDtypeStruct((s, 256), BF16),
            jax.ShapeDtypeStruct((s, MEM_W), BF16),
        ],
        compiler_params=pltpu.CompilerParams(
            dimension_semantics=("arbitrary",), vmem_limit_bytes=VMEM_LIMIT),
        name="layer1_proj",
    )(x, gkv, gmix, wkv, w_in, gb, qg, mqg, kng, mkT, mv, bd, hm, pat)


def _compress_kernel(t_ref, w1_ref, w2_ref, w2T_ref, pe_ref, kg_ref, kgT_ref, o_ref, oT_ref, hb_ref):
    nc = t_ref.shape[1]
    half = CMP_STRIDE * HEAD_DIM
    t = t_ref[0]
    ha = _dot(t, w1_ref[0, 0:half, :])
    hb_ref[0:nc, :] = _dot(t, w1_ref[0, half:2 * half, :])
    hb_ref[nc:nc + 8, :] = jnp.zeros((8, CMP_HID), F32)
    pe = jnp.broadcast_to(pe_ref[0], (8, 2 * half)).astype(BF16)
    pe_term = _dot(pe, w1_ref[0])[0:1, :]
    h = ha + hb_ref[1:nc + 1, :] + pe_term
    hg = 0.5 * h * (1.0 + jnp.tanh(0.7978845608028654 * (h + 0.044715 * (h * h * h))))
    hg = hg.astype(BF16)
    o = _dot(hg, w2_ref[0])
    oT = lax.dot_general(w2T_ref[0], hg, (((1,), (1,)), ((), ())), preferred_element_type=F32)
    is_k = pl.program_id(0) < GROUPS
    on = o * lax.rsqrt(jnp.mean(o * o, axis=1, keepdims=True) + EPS) * kg_ref[...]
    oTn = oT * lax.rsqrt(jnp.mean(oT * oT, axis=0, keepdims=True) + EPS) * kgT_ref[...]
    o_ref[0] = jnp.where(is_k, on, o).astype(BF16)
    oT_ref[0] = jnp.where(is_k, oTn, oT).astype(BF16)


def _compress(t, w1, w2, w2T, pe, kg, kgT):
    n, nc, width = t.shape
    return pl.pallas_call(
        _compress_kernel,
        grid=(n,),
        in_specs=[
            pl.BlockSpec((1, nc, width), lambda i: (i, 0, 0)),
            pl.BlockSpec((1, 2 * width, CMP_HID), lambda i: (i // GROUPS, 0, 0)),
            pl.BlockSpec((1, CMP_HID, HEAD_DIM), lambda i: (i // GROUPS, 0, 0)),
            pl.BlockSpec((1, HEAD_DIM, CMP_HID), lambda i: (i // GROUPS, 0, 0)),
            pl.BlockSpec((1, 1, 2 * width), lambda i: (i // GROUPS, 0, 0)),
            pl.BlockSpec((1, HEAD_DIM), lambda i: (0, 0)),
            pl.BlockSpec((HEAD_DIM, 1), lambda i: (0, 0)),
        ],
        out_specs=[
            pl.BlockSpec((1, nc, HEAD_DIM), lambda i: (i, 0, 0)),
            pl.BlockSpec((1, HEAD_DIM, nc), lambda i: (i, 0, 0)),
        ],
        out_shape=[
            jax.ShapeDtypeStruct((n, nc, HEAD_DIM), BF16),
            jax.ShapeDtypeStruct((n, HEAD_DIM, nc), BF16),
        ],
        scratch_shapes=[pltpu.VMEM((nc + 8, CMP_HID), F32)],
        compiler_params=pltpu.CompilerParams(
            dimension_semantics=("arbitrary",), vmem_limit_bytes=VMEM_LIMIT),
        name="compress_kv",
    )(t, w1, w2, w2T, pe, kg, kgT)


def _nsa_kernel(qT_ref, gT_ref, kc_ref, vcT_ref, ksel_ref, vselT_ref, kwin_ref, vwinT_ref, out_ref,
                qaug_ref, psum_ref, bias_ref, acc_ref, m_ref):
    c = pl.program_id(1)
    nc = kc_ref.shape[1]
    nsb = bias_ref.shape[0]
    hg = HEADS_PER_GROUP
    width = hg * Q_BLOCK

    for hh in range(hg):
        qaug_ref[0:HEAD_DIM, hh * Q_BLOCK:(hh + 1) * Q_BLOCK] = qT_ref[hh]
    qaug_ref[HEAD_DIM:128, :] = jnp.zeros((128 - HEAD_DIM, width), BF16)
    qT = qaug_ref[0:HEAD_DIM, :]
    t_q = c * Q_BLOCK + (lax.broadcasted_iota(jnp.int32, (1, width), 1) & (Q_BLOCK - 1))

    s = _dot(kc_ref[0], qT)
    ci = lax.broadcasted_iota(jnp.int32, (nc, 1), 0)
    s = jnp.where(ci * CMP_STRIDE + (CMP_L - 1) <= t_q, s, -jnp.inf)
    m = jnp.max(s, axis=0, keepdims=True)
    m = jnp.where(m == -jnp.inf, 0.0, m)
    e = jnp.exp2(s - m)
    p = e / jnp.maximum(jnp.sum(e, axis=0, keepdims=True), 1e-30)
    o_cmp = _dot(vcT_ref[0], p.astype(BF16))

    psum = p[:, 0:Q_BLOCK]
    for hh in range(1, hg):
        psum = psum + p[:, hh * Q_BLOCK:(hh + 1) * Q_BLOCK]
    psum_ref[0:8, :] = jnp.zeros((8, Q_BLOCK), F32)
    psum_ref[8:8 + nc, :] = psum
    imp = (psum_ref[pl.ds(7, nsb, stride=4), :] + psum_ref[pl.ds(11, nsb, stride=4), :]
           + 2.0 * (psum_ref[pl.ds(8, nsb, stride=4), :] + psum_ref[pl.ds(9, nsb, stride=4), :]
                    + psum_ref[pl.ds(10, nsb, stride=4), :]))

    blk = lax.broadcasted_iota(jnp.int32, (nsb, Q_BLOCK), 0)
    tb = (c * Q_BLOCK + lax.broadcasted_iota(jnp.int32, (1, Q_BLOCK), 1)) >> 6
    valid = blk <= tb
    forced = (blk == 0) | (blk == tb) | (blk == tb - 1)
    score = jnp.where(forced, jnp.inf, jnp.where(valid, imp, -jnp.inf))

    blk_f = blk.astype(F32)

    def pick(_, sc):
        mx = jnp.max(sc, axis=0, keepdims=True)
        first = jnp.min(jnp.where(sc == mx, blk_f, float(nsb)), axis=0, keepdims=True)
        return jnp.where(blk_f == first, -jnp.inf, sc)

    left = lax.fori_loop(0, min(N_SEL, nsb), pick, score)
    sel = (left == -jnp.inf) & valid
    bias_ref[...] = jnp.where(sel, 0.0, MASK_BIAS).astype(BF16)

    acc_ref[...] = jnp.zeros(acc_ref.shape, F32)
    m_ref[...] = jnp.full(m_ref.shape, M_INIT, F32)

    def sweep(j, causal):
        k0 = pl.multiple_of(j * KEY_TILE, KEY_TILE)
        b16 = bias_ref[pl.ds(pl.multiple_of(j * SEL_PER_TILE, SEL_PER_TILE), SEL_PER_TILE), :]
        for hh in range(hg):
            qaug_ref[HEAD_DIM:HEAD_DIM + SEL_PER_TILE, hh * Q_BLOCK:(hh + 1) * Q_BLOCK] = b16
        sj = _dot(ksel_ref[0, pl.ds(k0, KEY_TILE), :], qaug_ref[...])
        if causal:
            kp = k0 + lax.broadcasted_iota(jnp.int32, (KEY_TILE, 1), 0)
            sj = jnp.where(kp <= t_q, sj, MASK_BIAS)
        m_old = m_ref[...]
        m_new = jnp.maximum(m_old, jnp.max(sj, axis=0, keepdims=True))
        pj = jnp.exp2(sj - m_new).astype(BF16)
        m_ref[...] = m_new
        acc_ref[...] = acc_ref[...] * jnp.exp2(m_old - m_new) + _dot(
            vselT_ref[0, :, pl.ds(k0, KEY_TILE)], pj)

    last = c // (KEY_TILE // Q_BLOCK)

    def sweep_past(j, carry):
        sweep(j, False)
        return carry

    lax.fori_loop(0, last, sweep_past, 0)
    sweep(last, True)
    o_slc = acc_ref[0:HEAD_DIM, :] / jnp.maximum(acc_ref[HEAD_DIM:HEAD_DIM + 1, :], 1e-30)

    row0 = pl.multiple_of(jnp.maximum(c * Q_BLOCK - WIN, 0), Q_BLOCK)
    sw = _dot(kwin_ref[0, pl.ds(row0, WIN_KEYS), :], qT)
    kp = row0 + lax.broadcasted_iota(jnp.int32, (WIN_KEYS, 1), 0)
    sw = jnp.where((kp <= t_q) & (kp > t_q - WIN), sw, -jnp.inf)
    mw = jnp.max(sw, axis=0, keepdims=True)
    mw = jnp.where(mw == -jnp.inf, 0.0, mw)
    ew = jnp.exp2(sw - mw)
    lw = jnp.maximum(jnp.sum(ew, axis=0, keepdims=True), 1e-30)
    o_win = _dot(vwinT_ref[0, :, pl.ds(row0, WIN_KEYS)], ew.astype(BF16)) / lw

    heads = []
    for hh in range(hg):
        hs = slice(hh * Q_BLOCK, (hh + 1) * Q_BLOCK)
        heads.append(o_cmp[:, hs] * gT_ref[hh:hh + 1, :] + o_slc[:, hs] * gT_ref[8 + hh:9 + hh, :]
                     + o_win[:, hs] * gT_ref[16 + hh:17 + hh, :])
    for pr in range(hg // 2):
        pair = jnp.concatenate([heads[2 * pr], heads[2 * pr + 1]], axis=0)
        out_ref[:, pr * 128:(pr + 1) * 128] = pair.T.astype(BF16)


def _nsa(qT, gT, kc, vcT, ksel, vselT, kwin, vwinT):
    s = ksel.shape[1]
    nc = kc.shape[1]
    nsb = s // SEL_L
    hg = HEADS_PER_GROUP
    grp = lambda g, c: (g, 0, 0)
    return pl.pallas_call(
        _nsa_kernel,
        grid=(GROUPS, s // Q_BLOCK),
        in_specs=[
            pl.BlockSpec((hg, HEAD_DIM, Q_BLOCK), lambda g, c: (g, 0, c)),
            pl.BlockSpec((GATE_ROWS, Q_BLOCK), lambda g, c: (g, c)),
            pl.BlockSpec((1, nc, HEAD_DIM), grp),
            pl.BlockSpec((1, HEAD_DIM, nc), lambda g, c: (GROUPS + g, 0, 0)),
            pl.BlockSpec((1, s, 128), grp),
            pl.BlockSpec((1, VAUG_ROWS, s), grp),
            pl.BlockSpec((1, s, HEAD_DIM), grp),
            pl.BlockSpec((1, HEAD_DIM, s), grp),
        ],
        out_specs=pl.BlockSpec((Q_BLOCK, hg * HEAD_DIM), lambda g, c: (c, g)),
        out_shape=jax.ShapeDtypeStruct((s, NSA_W), BF16),
        scratch_shapes=[
            pltpu.VMEM((128, hg * Q_BLOCK), BF16),
            pltpu.VMEM((nc + 16, Q_BLOCK), F32),
            pltpu.VMEM((nsb, Q_BLOCK), BF16),
            pltpu.VMEM((VAUG_ROWS, hg * Q_BLOCK), F32),
            pltpu.VMEM((1, hg * Q_BLOCK), F32),
        ],
        compiler_params=pltpu.CompilerParams(
            dimension_semantics=("arbitrary", "arbitrary"), vmem_limit_bytes=VMEM_LIMIT),
        name="nsa_attention",
    )(qT, gT, kc, vcT, ksel, vselT, kwin, vwinT)


def _block_diag_ones():
    idx = np.arange(MEM_W) // HEAD_DIM
    return jnp.asarray((idx[:, None] == idx[None, :]).astype(np.float32), BF16)


def _head_masks():
    idx = np.arange(MEM_W) // HEAD_DIM
    return jnp.asarray((idx[None, :] == np.arange(MEM_HEADS)[:, None]).astype(np.float32))


def _onehot_pattern():
    rows = np.arange(KEY_TILE)[:, None] // SEL_L
    lanes = np.arange(128)[None, :] - HEAD_DIM
    return jnp.asarray((rows == lanes).astype(np.float32))


def _gate_layout():
    src = np.full((GATE_PAD,), -1, np.int64)
    for g in range(GROUPS):
        for b in range(3):
            for hh in range(HEADS_PER_GROUP):
                src[g * GATE_ROWS + b * 8 + hh] = (g * HEADS_PER_GROUP + hh) * 3 + b
    return src


def kernel(x, mem, norm_mix_g, norm_mlp_g, mem_norm_g, w_mem_kv, mem_q_norm_g, mem_k_norm_g, w_out, w_mlp_in, w_mlp_out, a_w_in, a_b_glu, a_dw, a_dw_b, a_ln_g, a_ln_b, b_w_in, b_gate_b, b_q_norm_g, kv_norm_g, w_kv, k_norm_g, cmp_pe_k, cmp_pe_v, cmp_w1_k, cmp_w2_k, cmp_w1_v, cmp_w2_v):
    batch, s, _ = x.shape
    assert batch == 1 and s % KEY_TILE == 0 and s >= WIN_KEYS
    assert w_out.shape[0] == 2 and a_w_in.shape[0] == 1 and b_w_in.shape[0] == 1
    nc = s // CMP_STRIDE
    row = lambda v: v.reshape(1, -1)
    bd = _block_diag_ones()
    hm = _head_masks()

    mkT, mv = _memkv(mem[0], row(mem_norm_g), w_mem_kv.astype(BF16),
                     jnp.tile(mem_k_norm_g, (1, MEM_HEADS))[:, None, :], bd)

    conv0, memo0 = _layer0(
        x[0], row(norm_mix_g[0]), a_w_in[0].astype(BF16), row(a_b_glu[0]), a_dw[0], row(a_dw_b[0]),
        row(a_ln_g[0]), row(a_ln_b[0]), row(jnp.tile(mem_q_norm_g[0], MEM_HEADS)), mkT, mv, bd, hm)
    w_out_bf = w_out.astype(BF16)
    w_in_bf = w_mlp_in.astype(BF16)
    w_o_bf = w_mlp_out.astype(BF16)
    x1 = _post(x[0], conv0, memo0, w_out_bf[0, :CONV_W], w_out_bf[0, CONV_W:], row(norm_mlp_g[0]),
               w_in_bf[0], w_o_bf[0])

    src = _gate_layout()
    used = src >= 0
    w_gate = jnp.where(used[None, :], b_w_in[0][:, NSA_W + MEM_W + np.maximum(src, 0)], 0.0)
    b_gate = jnp.where(used, b_gate_b[0][np.maximum(src, 0)], 0.0)
    w_in1 = jnp.concatenate([b_w_in[0][:, :NSA_W + MEM_W], w_gate], axis=1).astype(BF16)
    q_gain = jnp.tile(b_q_norm_g[0], NSA_HEADS) * (QK_SCALE * LOG2E)
    kng = jnp.stack([jnp.tile(k_norm_g[1], GROUPS), jnp.tile(k_norm_g[2], GROUPS)])
    qT, gT, ksel, kwin, vselT, vwinT, kvc, memo1 = _layer1_proj(
        x1, row(kv_norm_g), row(norm_mix_g[1]), w_kv.astype(BF16), w_in1, row(b_gate), row(q_gain),
        row(jnp.tile(mem_q_norm_g[1], MEM_HEADS)), kng, mkT, mv, bd, hm, _onehot_pattern())

    t = kvc.reshape(s, 2 * GROUPS, HEAD_DIM).transpose(1, 0, 2).reshape(2 * GROUPS, nc, CMP_STRIDE * HEAD_DIM)
    w1 = jnp.stack([cmp_w1_k, cmp_w1_v]).astype(BF16)
    w2 = jnp.stack([cmp_w2_k, cmp_w2_v]).astype(BF16)
    pe = jnp.stack([cmp_pe_k.reshape(1, -1), cmp_pe_v.reshape(1, -1)])
    cmp_rows, cmp_cols = _compress(t, w1, w2, w2.transpose(0, 2, 1), pe, row(k_norm_g[0]),
                                   k_norm_g[0].reshape(-1, 1))

    nsa = _nsa(qT, gT, cmp_rows, cmp_cols, ksel, vselT, kwin, vwinT)
    x2 = _post(x1, nsa, memo1, w_out_bf[1, :NSA_W], w_out_bf[1, NSA_W:], row(norm_mlp_g[1]),
               w_in_bf[1], w_o_bf[1])
    return x2[None]
```

```python
import functools

import numpy as np
import jax
import jax.numpy as jnp
from jax import lax
from jax.experimental import pallas as pl
from jax.experimental.pallas import tpu as pltpu

F32 = jnp.float32
BF16 = jnp.bfloat16

D_MODEL = 1024
HEAD_DIM = 64
MEM_LEN = 256
MEM_HEADS = 4
MEM_W = MEM_HEADS * HEAD_DIM
CONV_W = D_MODEL - MEM_W
CONV_K = 31
NSA_HEADS = CONV_W // HEAD_DIM
NSA_W = NSA_HEADS * HEAD_DIM
GROUPS = 2
HEADS_PER_GROUP = NSA_HEADS // GROUPS
CMP_L = 32
CMP_STRIDE = 16
CMP_HID = 256
SEL_L = 64
N_SEL = 16
WIN = 512
Q_BLOCK = 128
D_FF = 4 * D_MODEL
GATE_PAD = 128
GATE_ROWS = GATE_PAD // GROUPS

ROW_TILE = 512
FF_CHUNK = 1024
KEY_TILE = 1024
ONEHOT_SPAN = 1024
KEY_CHUNK = 256
SEL_PER_SPAN = ONEHOT_SPAN // SEL_L
WIN_KEYS = WIN + Q_BLOCK
CONV_HALO = 32
CONV_ROWS = 64
VAUG_ROWS = 80
VMEM_LIMIT = 56 * 1024 * 1024

EPS = 1e-6
LN_EPS = 1e-5
LOG2E = 1.4426950408889634
QK_SCALE = HEAD_DIM ** -0.5
MASK_BIAS = -2.0 ** 126
M_INIT = -2.0 ** 100


def _dot(a, b):
    return jnp.dot(a, b, preferred_element_type=F32)


def _rms_scale(x):
    return lax.rsqrt(jnp.mean(x * x, axis=-1, keepdims=True) + EPS)


def _head_meansq(x, bd):
    x2 = x * x
    hi = x2.astype(BF16)
    lo = (x2 - hi.astype(F32)).astype(BF16)
    return (_dot(hi, bd) + _dot(lo, bd)) * (1.0 / HEAD_DIM)


def _mem_attention(qn, mkT, mv, hm_ref):
    out = jnp.zeros(qn.shape, F32)
    for h in range(MEM_HEADS):
        hm = hm_ref[h:h + 1, :]
        s = _dot((qn * hm).astype(BF16), mkT)
        e = jnp.exp2(s - jnp.max(s, axis=-1, keepdims=True))
        l = jnp.sum(e, axis=-1, keepdims=True)
        out = out + _dot(e.astype(BF16), mv) * (hm / l)
    return out


def _memkv_kernel(mem_ref, g_ref, w_ref, kg_ref, bd_ref, mkT_ref, mv_ref):
    m = mem_ref[...]
    mn = (m * _rms_scale(m) * g_ref[...]).astype(BF16)
    kv = _dot(mn, w_ref[0])
    k = kv[:, :MEM_W]
    kn = k * lax.rsqrt(_head_meansq(k, bd_ref[...]) + EPS) * kg_ref[0]
    mkT_ref[0] = (kn * (QK_SCALE * LOG2E)).T.astype(BF16)
    mv_ref[0] = kv[:, MEM_W:].astype(BF16)


def _memkv(mem, g, w, kg, bd):
    depth = w.shape[0]
    return pl.pallas_call(
        _memkv_kernel,
        grid=(depth,),
        in_specs=[
            pl.BlockSpec((MEM_LEN, D_MODEL), lambda l: (0, 0)),
            pl.BlockSpec((1, D_MODEL), lambda l: (0, 0)),
            pl.BlockSpec((1, D_MODEL, 2 * MEM_W), lambda l: (l, 0, 0)),
            pl.BlockSpec((1, 1, MEM_W), lambda l: (l, 0, 0)),
            pl.BlockSpec((MEM_W, MEM_W), lambda l: (0, 0)),
        ],
        out_specs=[
            pl.BlockSpec((1, MEM_W, MEM_LEN), lambda l: (l, 0, 0)),
            pl.BlockSpec((1, MEM_LEN, MEM_W), lambda l: (l, 0, 0)),
        ],
        out_shape=[
            jax.ShapeDtypeStruct((depth, MEM_W, MEM_LEN), BF16),
            jax.ShapeDtypeStruct((depth, MEM_LEN, MEM_W), BF16),
        ],
        name="mem_kv",
    )(mem, g, w, kg, bd)


def _layer0_kernel(x_ref, g_ref, win_ref, bglu_ref, dw_ref, dwb_ref, lng_ref, lnb_ref, qg_ref,
                   mkT_ref, mv_ref, bd_ref, hm_ref, conv_ref, memo_ref, buf_ref, cv_ref, sh_ref):
    i = pl.program_id(0)
    tm = x_ref.shape[0]
    x = x_ref[...]
    h = (x * _rms_scale(x) * g_ref[...]).astype(BF16)
    u = _dot(h, win_ref[...])
    a = u[:, :CONV_W] + bglu_ref[:, :CONV_W]
    gate = u[:, CONV_W:2 * CONV_W] + bglu_ref[:, CONV_W:]
    v = a * jax.nn.sigmoid(gate)

    @pl.when(i == 0)
    def _():
        buf_ref[0:CONV_HALO, :] = jnp.zeros((CONV_HALO, CONV_W), F32)

    buf_ref[CONV_HALO:CONV_HALO + tm, :] = v

    base = CONV_HALO - (CONV_K - 1)
    rows = tm + CONV_HALO
    for cb in range(CONV_W // 128):
        cs = slice(cb * 128, (cb + 1) * 128)
        xb = buf_ref[:, cs]
        sh_ref[0] = xb
        for b in range(1, 8):
            sh_ref[b] = pltpu.roll(xb, rows - b, axis=0)

        def conv_rows(r, carry, cs=cs):
            r0 = pl.multiple_of(r * CONV_ROWS, CONV_ROWS)
            acc = jnp.zeros((CONV_ROWS, 128), F32) + dwb_ref[:, cs]
            for k in range(CONV_K):
                a, b = divmod(base + k, 8)
                acc = acc + dw_ref[k:k + 1, cs] * sh_ref[b, pl.ds(r0 + 8 * a, CONV_ROWS), :]
            cv_ref[pl.ds(r0, CONV_ROWS), cs] = acc
            return carry

        lax.fori_loop(0, tm // CONV_ROWS, conv_rows, 0)
    buf_ref[0:CONV_HALO, :] = buf_ref[tm:tm + CONV_HALO, :]

    cv = cv_ref[...]
    mu = jnp.mean(cv, axis=-1, keepdims=True)
    d = cv - mu
    var = jnp.mean(d * d, axis=-1, keepdims=True)
    y = d * lax.rsqrt(var + LN_EPS) * lng_ref[...] + lnb_ref[...]
    conv_ref[...] = (y * jax.nn.sigmoid(y)).astype(BF16)

    qm = u[:, 2 * CONV_W:]
    qn = qm * lax.rsqrt(_head_meansq(qm, bd_ref[...]) + EPS) * qg_ref[...]
    memo_ref[...] = _mem_attention(qn, mkT_ref[0], mv_ref[0], hm_ref).astype(BF16)


def _layer0(x, g, w_in, bglu, dw, dwb, lng, lnb, qg, mkT, mv, bd, hm):
    s = x.shape[0]
    tm = ROW_TILE
    const = lambda i: (0, 0)
    return pl.pallas_call(
        _layer0_kernel,
        grid=(s // tm,),
        in_specs=[
            pl.BlockSpec((tm, D_MODEL), lambda i: (i, 0)),
            pl.BlockSpec((1, D_MODEL), const),
            pl.BlockSpec(w_in.shape, const),
            pl.BlockSpec((1, 2 * CONV_W), const),
            pl.BlockSpec((CONV_K, CONV_W), const),
            pl.BlockSpec((1, CONV_W), const),
            pl.BlockSpec((1, CONV_W), const),
            pl.BlockSpec((1, CONV_W), const),
            pl.BlockSpec((1, MEM_W), const),
            pl.BlockSpec((1, MEM_W, MEM_LEN), lambda i: (0, 0, 0)),
            pl.BlockSpec((1, MEM_LEN, MEM_W), lambda i: (0, 0, 0)),
            pl.BlockSpec((MEM_W, MEM_W), const),
            pl.BlockSpec((MEM_HEADS, MEM_W), const),
        ],
        out_specs=[
            pl.BlockSpec((tm, CONV_W), lambda i: (i, 0)),
            pl.BlockSpec((tm, MEM_W), lambda i: (i, 0)),
        ],
        out_shape=[
            jax.ShapeDtypeStruct((s, CONV_W), BF16),
            jax.ShapeDtypeStruct((s, MEM_W), BF16),
        ],
        scratch_shapes=[
            pltpu.VMEM((tm + CONV_HALO, CONV_W), F32),
            pltpu.VMEM((tm, CONV_W), F32),
            pltpu.VMEM((8, tm + CONV_HALO, 128), F32),
        ],
        compiler_params=pltpu.CompilerParams(
            dimension_semantics=("arbitrary",), vmem_limit_bytes=VMEM_LIMIT),
        name="layer0_mixer",
    )(x, g, w_in, bglu, dw, dwb, lng, lnb, qg, mkT, mv, bd, hm)


def _post_kernel(x_ref, ma_ref, mb_ref, wa_ref, wb_ref, g_ref, win_ref, wout_ref, o_ref):
    o_ref[...] = x_ref[...] + _dot(ma_ref[...], wa_ref[...]) + _dot(mb_ref[...], wb_ref[...])
    x1 = o_ref[...]
    h = (x1 * _rms_scale(x1) * g_ref[...]).astype(BF16)
    for c in range(D_FF // FF_CHUNK):
        cs = slice(c * FF_CHUNK, (c + 1) * FF_CHUNK)
        t = jnp.maximum(_dot(h, win_ref[:, cs]), 0.0)
        o_ref[...] += _dot((t * t).astype(BF16), wout_ref[cs, :])


def _post(x, ma, mb, wa, wb, g, w_in, w_out):
    s = x.shape[0]
    tm = ROW_TILE
    const = lambda i: (0, 0)
    single = pl.Buffered(1)
    return pl.pallas_call(
        _post_kernel,
        grid=(s // tm,),
        in_specs=[
            pl.BlockSpec((tm, D_MODEL), lambda i: (i, 0)),
            pl.BlockSpec((tm, CONV_W), lambda i: (i, 0)),
            pl.BlockSpec((tm, MEM_W), lambda i: (i, 0)),
            pl.BlockSpec((CONV_W, D_MODEL), const, pipeline_mode=single),
            pl.BlockSpec((MEM_W, D_MODEL), const, pipeline_mode=single),
            pl.BlockSpec((1, D_MODEL), const),
            pl.BlockSpec((D_MODEL, D_FF), const, pipeline_mode=single),
            pl.BlockSpec((D_FF, D_MODEL), const, pipeline_mode=single),
        ],
        out_specs=pl.BlockSpec((tm, D_MODEL), lambda i: (i, 0)),
        out_shape=jax.ShapeDtypeStruct((s, D_MODEL), F32),
        compiler_params=pltpu.CompilerParams(
            dimension_semantics=("arbitrary",), vmem_limit_bytes=VMEM_LIMIT),
        name="outproj_mlp",
    )(x, ma, mb, wa, wb, g, w_in, w_out)


def _layer1_proj_kernel(x_ref, gkv_ref, gmix_ref, wkv_ref, win_ref, gb_ref, qg_ref, mqg_ref, kng_ref,
                        mkT_ref, mv_ref, bd_ref, hm_ref, pat_ref,
                        qT_ref, gT_ref, ksel_ref, kwin_ref, vselT_ref, vwinT_ref, kvc_ref, memo_ref):
    tm = x_ref.shape[0]
    x = x_ref[...]
    xn = x * _rms_scale(x)
    bd = bd_ref[...]

    kv = _dot((xn * gkv_ref[...]).astype(BF16), wkv_ref[...])
    kvc_ref[...] = kv[:, 0:256].astype(BF16)
    k2 = kv[:, 256:384]
    v2 = kv[:, 384:512]
    kw = kv[:, 512:640]
    vw = kv[:, 640:768]
    bd2 = bd[0:128, 0:128]
    k2n = k2 * lax.rsqrt(_head_meansq(k2, bd2) + EPS) * kng_ref[0:1, :]
    kwn = kw * lax.rsqrt(_head_meansq(kw, bd2) + EPS) * kng_ref[1:2, :]

    lane = lax.broadcasted_iota(jnp.int32, (tm, 128), 1)
    pat = pat_ref[...]
    ksel_ref[0] = jnp.where(lane < HEAD_DIM, k2n, pat).astype(BF16)
    ksel_ref[1] = jnp.where(lane < HEAD_DIM, pltpu.roll(k2n, HEAD_DIM, axis=1), pat).astype(BF16)
    kwin_ref[0] = kwn[:, :HEAD_DIM].astype(BF16)
    kwin_ref[1] = kwn[:, HEAD_DIM:].astype(BF16)

    ones_rows = (lax.broadcasted_iota(jnp.int32, (VAUG_ROWS - HEAD_DIM, tm), 0) == 0).astype(BF16)
    v2T = v2.T.astype(BF16)
    vwT = vw.T.astype(BF16)
    for g in range(GROUPS):
        vselT_ref[g, 0:HEAD_DIM, :] = v2T[g * HEAD_DIM:(g + 1) * HEAD_DIM, :]
        vselT_ref[g, HEAD_DIM:VAUG_ROWS, :] = ones_rows
        vwinT_ref[g] = vwT[g * HEAD_DIM:(g + 1) * HEAD_DIM, :]

    u = _dot((xn * gmix_ref[...]).astype(BF16), win_ref[...])
    for cb in range(NSA_W // 256):
        cs = slice(cb * 256, (cb + 1) * 256)
        qc = u[:, cs]
        qn = qc * lax.rsqrt(_head_meansq(qc, bd) + EPS) * qg_ref[:, cs]
        qT_ref[4 * cb:4 * cb + 4] = qn.T.astype(BF16).reshape(4, HEAD_DIM, tm)

    gates = jax.nn.sigmoid(u[:, NSA_W + MEM_W:] + gb_ref[...])
    gT_ref[...] = gates.T

    qm = u[:, NSA_W:NSA_W + MEM_W]
    qmn = qm * lax.rsqrt(_head_meansq(qm, bd) + EPS) * mqg_ref[...]
    memo_ref[...] = _mem_attention(qmn, mkT_ref[0], mv_ref[0], hm_ref).astype(BF16)


def _layer1_proj(x, gkv, gmix, wkv, w_in, gb, qg, mqg, kng, mkT, mv, bd, hm, pat):
    s = x.shape[0]
    tm = ROW_TILE
    const = lambda i: (0, 0)
    pat_blocks = pat.shape[0] // tm
    return pl.pallas_call(
        _layer1_proj_kernel,
        grid=(s // tm,),
        in_specs=[
            pl.BlockSpec((tm, D_MODEL), lambda i: (i, 0)),
            pl.BlockSpec((1, D_MODEL), const),
            pl.BlockSpec((1, D_MODEL), const),
            pl.BlockSpec(wkv.shape, const),
            pl.BlockSpec(w_in.shape, const),
            pl.BlockSpec((1, GATE_PAD), const),
            pl.BlockSpec((1, NSA_W), const),
            pl.BlockSpec((1, MEM_W), const),
            pl.BlockSpec((2, 128), const),
            pl.BlockSpec((1, MEM_W, MEM_LEN), lambda i: (1, 0, 0)),
            pl.BlockSpec((1, MEM_LEN, MEM_W), lambda i: (1, 0, 0)),
            pl.BlockSpec((MEM_W, MEM_W), const),
            pl.BlockSpec((MEM_HEADS, MEM_W), const),
            pl.BlockSpec((tm, 128), lambda i: (i % pat_blocks, 0)),
        ],
        out_specs=[
            pl.BlockSpec((NSA_HEADS, HEAD_DIM, tm), lambda i: (0, 0, i)),
            pl.BlockSpec((GATE_PAD, tm), lambda i: (0, i)),
            pl.BlockSpec((GROUPS, tm, 128), lambda i: (0, i, 0)),
            pl.BlockSpec((GROUPS, tm, HEAD_DIM), lambda i: (0, i, 0)),
            pl.BlockSpec((GROUPS, VAUG_ROWS, tm), lambda i: (0, 0, i)),
            pl.BlockSpec((GROUPS, HEAD_DIM, tm), lambda i: (0, 0, i)),
            pl.BlockSpec((tm, 256), lambda i: (i, 0)),
            pl.BlockSpec((tm, MEM_W), lambda i: (i, 0)),
        ],
        out_shape=[
            jax.ShapeDtypeStruct((NSA_HEADS, HEAD_DIM, s), BF16),
            jax.ShapeDtypeStruct((GATE_PAD, s), F32),
            jax.ShapeDtypeStruct((GROUPS, s, 128), BF16),
            jax.ShapeDtypeStruct((GROUPS, s, HEAD_DIM), BF16),
            jax.ShapeDtypeStruct((GROUPS, VAUG_ROWS, s), BF16),
            jax.ShapeDtypeStruct((GROUPS, HEAD_DIM, s), BF16),
            jax.ShapeDtypeStruct((s, 256), BF16),
            jax.ShapeDtypeStruct((s, MEM_W), BF16),
        ],
        compiler_params=pltpu.CompilerParams(
            dimension_semantics=("arbitrary",), vmem_limit_bytes=VMEM_LIMIT),
        name="layer1_proj",
    )(x, gkv, gmix, wkv, w_in, gb, qg, mqg, kng, mkT, mv, bd, hm, pat)


def _compress_kernel(t_ref, w1_ref, w2_ref, w2T_ref, pe_ref, kg_ref, kgT_ref, o_ref, oT_ref, hb_ref):
    nc = t_ref.shape[1]
    half = CMP_STRIDE * HEAD_DIM
    t = t_ref[0]
    ha = _dot(t, w1_ref[0, 0:half, :])
    hb_ref[0:nc, :] = _dot(t, w1_ref[0, half:2 * half, :])
    hb_ref[nc:nc + 8, :] = jnp.zeros((8, CMP_HID), F32)
    pe = jnp.broadcast_to(pe_ref[0], (8, 2 * half)).astype(BF16)
    pe_term = _dot(pe, w1_ref[0])[0:1, :]
    h = ha + hb_ref[1:nc + 1, :] + pe_term
    hg = 0.5 * h * (1.0 + jnp.tanh(0.7978845608028654 * (h + 0.044715 * (h * h * h))))
    hg = hg.astype(BF16)
    o = _dot(hg, w2_ref[0])
    oT = lax.dot_general(w2T_ref[0], hg, (((1,), (1,)), ((), ())), preferred_element_type=F32)
    is_k = pl.program_id(0) < GROUPS
    on = o * lax.rsqrt(jnp.mean(o * o, axis=1, keepdims=True) + EPS) * kg_ref[...]
    oTn = oT * lax.rsqrt(jnp.mean(oT * oT, axis=0, keepdims=True) + EPS) * kgT_ref[...]
    o_ref[0] = jnp.where(is_k, on, o).astype(BF16)
    oT_ref[0] = jnp.where(is_k, oTn, oT).astype(BF16)


def _compress(t, w1, w2, w2T, pe, kg, kgT):
    n, nc, width = t.shape
    return pl.pallas_call(
        _compress_kernel,
        grid=(n,),
        in_specs=[
            pl.BlockSpec((1, nc, width), lambda i: (i, 0, 0)),
            pl.BlockSpec((1, 2 * width, CMP_HID), lambda i: (i // GROUPS, 0, 0)),
            pl.BlockSpec((1, CMP_HID, HEAD_DIM), lambda i: (i // GROUPS, 0, 0)),
            pl.BlockSpec((1, HEAD_DIM, CMP_HID), lambda i: (i // GROUPS, 0, 0)),
            pl.BlockSpec((1, 1, 2 * width), lambda i: (i // GROUPS, 0, 0)),
            pl.BlockSpec((1, HEAD_DIM), lambda i: (0, 0)),
            pl.BlockSpec((HEAD_DIM, 1), lambda i: (0, 0)),
        ],
        out_specs=[
            pl.BlockSpec((1, nc, HEAD_DIM), lambda i: (i, 0, 0)),
            pl.BlockSpec((1, HEAD_DIM, nc), lambda i: (i, 0, 0)),
        ],
        out_shape=[
            jax.ShapeDtypeStruct((n, nc, HEAD_DIM), BF16),
            jax.ShapeDtypeStruct((n, HEAD_DIM, nc), BF16),
        ],
        scratch_shapes=[pltpu.VMEM((nc + 8, CMP_HID), F32)],
        compiler_params=pltpu.CompilerParams(
            dimension_semantics=("arbitrary",), vmem_limit_bytes=VMEM_LIMIT),
        name="compress_kv",
    )(t, w1, w2, w2T, pe, kg, kgT)


def _nsa_kernel(qT_ref, gT_ref, kc_ref, vcT_ref, ksel_ref, vselT_ref, kwin_ref, vwinT_ref, out_ref,
                qaug_ref, psum_ref, bias_ref, acc_ref, m_ref, s_ref):
    c = pl.program_id(1)
    nc = kc_ref.shape[1]
    nsb = bias_ref.shape[0]
    hg = HEADS_PER_GROUP
    width = hg * Q_BLOCK

    for hh in range(hg):
        qaug_ref[0:HEAD_DIM, hh * Q_BLOCK:(hh + 1) * Q_BLOCK] = qT_ref[hh]
    qaug_ref[HEAD_DIM:128, :] = jnp.zeros((128 - HEAD_DIM, width), BF16)
    qT = qaug_ref[0:HEAD_DIM, :]
    t_q = c * Q_BLOCK + (lax.broadcasted_iota(jnp.int32, (1, width), 1) & (Q_BLOCK - 1))

    s = _dot(kc_ref[0], qT)
    ci = lax.broadcasted_iota(jnp.int32, (nc, 1), 0)
    s = jnp.where(ci * CMP_STRIDE + (CMP_L - 1) <= t_q, s, -jnp.inf)
    m = jnp.max(s, axis=0, keepdims=True)
    m = jnp.where(m == -jnp.inf, 0.0, m)
    e = jnp.exp2(s - m)
    p = e * (1.0 / jnp.maximum(jnp.sum(e, axis=0, keepdims=True), 1e-30))
    o_cmp = _dot(vcT_ref[0], p.astype(BF16))

    psum = p[:, 0:Q_BLOCK]
    for hh in range(1, hg):
        psum = psum + p[:, hh * Q_BLOCK:(hh + 1) * Q_BLOCK]
    psum_ref[0:8, :] = jnp.zeros((8, Q_BLOCK), F32)
    psum_ref[8:8 + nc, :] = psum
    imp = (psum_ref[pl.ds(7, nsb, stride=4), :] + psum_ref[pl.ds(11, nsb, stride=4), :]
           + 2.0 * (psum_ref[pl.ds(8, nsb, stride=4), :] + psum_ref[pl.ds(9, nsb, stride=4), :]
                    + psum_ref[pl.ds(10, nsb, stride=4), :]))

    blk = lax.broadcasted_iota(jnp.int32, (nsb, Q_BLOCK), 0)
    tb = (c * Q_BLOCK + lax.broadcasted_iota(jnp.int32, (1, Q_BLOCK), 1)) >> 6
    valid = blk <= tb
    forced = (blk == 0) | (blk == tb) | (blk == tb - 1)
    score = jnp.where(forced | ~valid, -jnp.inf, imp)

    blk_f = blk.astype(F32)

    def pick(_, sc):
        mx = jnp.max(sc, axis=0, keepdims=True)
        first = jnp.min(jnp.where(sc == mx, blk_f, float(nsb)), axis=0, keepdims=True)
        return jnp.where(blk_f == first, -jnp.inf, sc)

    left = lax.fori_loop(0, N_SEL - 3, pick, score)
    sel = (left == -jnp.inf) & valid
    bias_ref[...] = jnp.where(sel, 0.0, MASK_BIAS).astype(BF16)

    acc_ref[...] = jnp.zeros(acc_ref.shape, F32)
    m_ref[...] = jnp.full(m_ref.shape, M_INIT, F32)

    n_chunks = KEY_TILE // KEY_CHUNK

    def set_bias(j):
        span = j // (ONEHOT_SPAN // KEY_TILE)
        b16 = bias_ref[pl.ds(pl.multiple_of(span * SEL_PER_SPAN, SEL_PER_SPAN), SEL_PER_SPAN), :]
        for hh in range(hg):
            qaug_ref[HEAD_DIM:HEAD_DIM + SEL_PER_SPAN, hh * Q_BLOCK:(hh + 1) * Q_BLOCK] = b16

    def scores(j, r):
        k0 = pl.multiple_of(j * KEY_TILE + r * KEY_CHUNK, KEY_CHUNK)
        return _dot(ksel_ref[0, pl.ds(k0, KEY_CHUNK), :], qaug_ref[...])

    def chunk_max(cmax, sc):
        return jnp.maximum(cmax, jnp.max(sc.reshape(KEY_CHUNK // 8, 8, width), axis=0))

    def new_max(cmax):
        m_old = m_ref[...]
        m_new = jnp.maximum(m_old, jnp.max(cmax, axis=0, keepdims=True))
        m_ref[...] = m_new
        return m_new, jnp.exp2(m_old - m_new)

    def values(j, r):
        k0 = pl.multiple_of(j * KEY_TILE + r * KEY_CHUNK, KEY_CHUNK)
        return vselT_ref[0, :, pl.ds(k0, KEY_CHUNK)]

    cmax0 = jnp.full((8, width), MASK_BIAS, F32)
    set_bias(0)
    cmax = cmax0
    for r in range(n_chunks):
        sc = scores(0, r)
        s_ref[r * KEY_CHUNK:(r + 1) * KEY_CHUNK, :] = sc
        cmax = chunk_max(cmax, sc)

    last = c // (KEY_TILE // Q_BLOCK)

    def pipe_step(j, cmax):
        m_new, alpha = new_max(cmax)
        set_bias(j + 1)
        acc = acc_ref[...] * alpha
        cnext = cmax0
        for r in range(n_chunks):
            rs = slice(r * KEY_CHUNK, (r + 1) * KEY_CHUNK)
            pj = jnp.exp2(s_ref[rs, :] - m_new).astype(BF16)
            sc = scores(j + 1, r)
            s_ref[rs, :] = sc
            cnext = chunk_max(cnext, sc)
            acc = acc + _dot(values(j, r), pj)
        acc_ref[...] = acc
        return cnext

    lax.fori_loop(0, last, pipe_step, cmax)

    kp = last * KEY_TILE + lax.broadcasted_iota(jnp.int32, (KEY_TILE, 1), 0)
    s_last = jnp.where(kp <= t_q, s_ref[...], MASK_BIAS)
    m_new, alpha = new_max(jnp.max(s_last.reshape(KEY_TILE // 8, 8, width), axis=0))
    p_last = jnp.exp2(s_last - m_new).astype(BF16)
    acc_ref[...] = acc_ref[...] * alpha + _dot(
        vselT_ref[0, :, pl.ds(pl.multiple_of(last * KEY_TILE, KEY_TILE), KEY_TILE)], p_last)
    o_slc = acc_ref[0:HEAD_DIM, :] / jnp.maximum(acc_ref[HEAD_DIM:HEAD_DIM + 1, :], 1e-30)

    row0 = pl.multiple_of(jnp.maximum(c * Q_BLOCK - WIN, 0), Q_BLOCK)
    sw = _dot(kwin_ref[0, pl.ds(row0, WIN_KEYS), :], qT)
    kp = row0 + lax.broadcasted_iota(jnp.int32, (WIN_KEYS, 1), 0)
    sw = jnp.where((kp <= t_q) & (kp > t_q - WIN), sw, -jnp.inf)
    mw = jnp.max(sw, axis=0, keepdims=True)
    mw = jnp.where(mw == -jnp.inf, 0.0, mw)
    ew = jnp.exp2(sw - mw)
    lw = jnp.maximum(jnp.sum(ew, axis=0, keepdims=True), 1e-30)
    o_win = _dot(vwinT_ref[0, :, pl.ds(row0, WIN_KEYS)], ew.astype(BF16)) / lw

    heads = []
    for hh in range(hg):
        hs = slice(hh * Q_BLOCK, (hh + 1) * Q_BLOCK)
        heads.append(o_cmp[:, hs] * gT_ref[hh:hh + 1, :] + o_slc[:, hs] * gT_ref[8 + hh:9 + hh, :]
                     + o_win[:, hs] * gT_ref[16 + hh:17 + hh, :])
    for pr in range(hg // 2):
        pair = jnp.concatenate([heads[2 * pr], heads[2 * pr + 1]], axis=0)
        out_ref[:, pr * 128:(pr + 1) * 128] = pair.T.astype(BF16)


def _nsa(qT, gT, kc, vcT, ksel, vselT, kwin, vwinT):
    s = ksel.shape[1]
    nc = kc.shape[1]
    nsb = s // SEL_L
    hg = HEADS_PER_GROUP
    grp = lambda g, c: (g, 0, 0)
    return pl.pallas_call(
        _nsa_kernel,
        grid=(GROUPS, s // Q_BLOCK),
        in_specs=[
            pl.BlockSpec((hg, HEAD_DIM, Q_BLOCK), lambda g, c: (g, 0, c)),
            pl.BlockSpec((GATE_ROWS, Q_BLOCK), lambda g, c: (g, c)),
            pl.BlockSpec((1, nc, HEAD_DIM), grp),
            pl.BlockSpec((1, HEAD_DIM, nc), lambda g, c: (GROUPS + g, 0, 0)),
            pl.BlockSpec((1, s, 128), grp),
            pl.BlockSpec((1, VAUG_ROWS, s), grp),
            pl.BlockSpec((1, s, HEAD_DIM), grp),
            pl.BlockSpec((1, HEAD_DIM, s), grp),
        ],
        out_specs=pl.BlockSpec((Q_BLOCK, hg * HEAD_DIM), lambda g, c: (c, g)),
        out_shape=jax.ShapeDtypeStruct((s, NSA_W), BF16),
        scratch_shapes=[
            pltpu.VMEM((128, hg * Q_BLOCK), BF16),
            pltpu.VMEM((nc + 16, Q_BLOCK), F32),
            pltpu.VMEM((nsb, Q_BLOCK), BF16),
            pltpu.VMEM((VAUG_ROWS, hg * Q_BLOCK), F32),
            pltpu.VMEM((1, hg * Q_BLOCK), F32),
            pltpu.VMEM((KEY_TILE, hg * Q_BLOCK), F32),
        ],
        compiler_params=pltpu.CompilerParams(
            dimension_semantics=("arbitrary", "arbitrary"), vmem_limit_bytes=VMEM_LIMIT),
        name="nsa_attention",
    )(qT, gT, kc, vcT, ksel, vselT, kwin, vwinT)


def _block_diag_ones():
    idx = np.arange(MEM_W) // HEAD_DIM
    return jnp.asarray((idx[:, None] == idx[None, :]).astype(np.float32), BF16)


def _head_masks():
    idx = np.arange(MEM_W) // HEAD_DIM
    return jnp.asarray((idx[None, :] == np.arange(MEM_HEADS)[:, None]).astype(np.float32))


def _onehot_pattern():
    rows = np.arange(ONEHOT_SPAN)[:, None] // SEL_L
    lanes = np.arange(128)[None, :] - HEAD_DIM
    return jnp.asarray((rows == lanes).astype(np.float32))


def _gate_layout():
    src = np.full((GATE_PAD,), -1, np.int64)
    for g in range(GROUPS):
        for b in range(3):
            for hh in range(HEADS_PER_GROUP):
                src[g * GATE_ROWS + b * 8 + hh] = (g * HEADS_PER_GROUP + hh) * 3 + b
    return src


def kernel(x, mem, norm_mix_g, norm_mlp_g, mem_norm_g, w_mem_kv, mem_q_norm_g, mem_k_norm_g, w_out, w_mlp_in, w_mlp_out, a_w_in, a_b_glu, a_dw, a_dw_b, a_ln_g, a_ln_b, b_w_in, b_gate_b, b_q_norm_g, kv_norm_g, w_kv, k_norm_g, cmp_pe_k, cmp_pe_v, cmp_w1_k, cmp_w2_k, cmp_w1_v, cmp_w2_v):
    batch, s, _ = x.shape
    assert batch == 1 and s % ONEHOT_SPAN == 0 and s >= WIN_KEYS
    assert w_out.shape[0] == 2 and a_w_in.shape[0] == 1 and b_w_in.shape[0] == 1
    nc = s // CMP_STRIDE
    row = lambda v: v.reshape(1, -1)
    bd = _block_diag_ones()
    hm = _head_masks()

    mkT, mv = _memkv(mem[0], row(mem_norm_g), w_mem_kv.astype(BF16),
                     jnp.tile(mem_k_norm_g, (1, MEM_HEADS))[:, None, :], bd)

    conv0, memo0 = _layer0(
        x[0], row(norm_mix_g[0]), a_w_in[0].astype(BF16), row(a_b_glu[0]), a_dw[0], row(a_dw_b[0]),
        row(a_ln_g[0]), row(a_ln_b[0]), row(jnp.tile(mem_q_norm_g[0], MEM_HEADS)), mkT, mv, bd, hm)
    w_out_bf = w_out.astype(BF16)
    w_in_bf = w_mlp_in.astype(BF16)
    w_o_bf = w_mlp_out.astype(BF16)
    x1 = _post(x[0], conv0, memo0, w_out_bf[0, :CONV_W], w_out_bf[0, CONV_W:], row(norm_mlp_g[0]),
               w_in_bf[0], w_o_bf[0])

    src = _gate_layout()
    used = src >= 0
    w_gate = jnp.where(used[None, :], b_w_in[0][:, NSA_W + MEM_W + np.maximum(src, 0)], 0.0)
    b_gate = jnp.where(used, b_gate_b[0][np.maximum(src, 0)], 0.0)
    w_in1 = jnp.concatenate([b_w_in[0][:, :NSA_W + MEM_W], w_gate], axis=1).astype(BF16)
    q_gain = jnp.tile(b_q_norm_g[0], NSA_HEADS) * (QK_SCALE * LOG2E)
    kng = jnp.stack([jnp.tile(k_norm_g[1], GROUPS), jnp.tile(k_norm_g[2], GROUPS)])
    qT, gT, ksel, kwin, vselT, vwinT, kvc, memo1 = _layer1_proj(
        x1, row(kv_norm_g), row(norm_mix_g[1]), w_kv.astype(BF16), w_in1, row(b_gate), row(q_gain),
        row(jnp.tile(mem_q_norm_g[1], MEM_HEADS)), kng, mkT, mv, bd, hm, _onehot_pattern())

    t = kvc.reshape(s, 2 * GROUPS, HEAD_DIM).transpose(1, 0, 2).reshape(2 * GROUPS, nc, CMP_STRIDE * HEAD_DIM)
    w1 = jnp.stack([cmp_w1_k, cmp_w1_v]).astype(BF16)
    w2 = jnp.stack([cmp_w2_k, cmp_w2_v]).astype(BF16)
    pe = jnp.stack([cmp_pe_k.reshape(1, -1), cmp_pe_v.reshape(1, -1)])
    cmp_rows, cmp_cols = _compress(t, w1, w2, w2.transpose(0, 2, 1), pe, row(k_norm_g[0]),
                                   k_norm_g[0].reshape(-1, 1))

    nsa = _nsa(qT, gT, cmp_rows, cmp_cols, ksel, vselT, kwin, vwinT)
    x2 = _post(x1, nsa, memo1, w_out_bf[1, :NSA_W], w_out_bf[1, NSA_W:], row(norm_mlp_g[1]),
               w_in_bf[1], w_o_bf[1])
    return x2[None]
```

```python
import functools

import numpy as np
import jax
import jax.numpy as jnp
from jax import lax
from jax.experimental import pallas as pl
from jax.experimental.pallas import tpu as pltpu

F32 = jnp.float32
BF16 = jnp.bfloat16

D_MODEL = 1024
HEAD_DIM = 64
MEM_LEN = 256
MEM_HEADS = 4
MEM_W = MEM_HEADS * HEAD_DIM
CONV_W = D_MODEL - MEM_W
CONV_K = 31
NSA_HEADS = CONV_W // HEAD_DIM
NSA_W = NSA_HEADS * HEAD_DIM
GROUPS = 2
HEADS_PER_GROUP = NSA_HEADS // GROUPS
CMP_L = 32
CMP_STRIDE = 16
CMP_HID = 256
SEL_L = 64
N_SEL = 16
WIN = 512
Q_BLOCK = 128
D_FF = 4 * D_MODEL
GATE_PAD = 128
GATE_ROWS = GATE_PAD // GROUPS

ROW_TILE = 512
FF_CHUNK = 1024
KEY_TILE = 1024
ONEHOT_SPAN = 1024
CMP_PATHS = 4
KEY_CHUNK = 256
SEL_PER_SPAN = ONEHOT_SPAN // SEL_L
WIN_KEYS = WIN + Q_BLOCK
CONV_HALO = 32
CONV_ROWS = 64
VAUG_ROWS = 80
VMEM_LIMIT = 56 * 1024 * 1024

EPS = 1e-6
LN_EPS = 1e-5
LOG2E = 1.4426950408889634
QK_SCALE = HEAD_DIM ** -0.5
MASK_BIAS = -2.0 ** 126
M_INIT = -2.0 ** 100


def _dot(a, b):
    return jnp.dot(a, b, preferred_element_type=F32)


def _rms_scale(x):
    return lax.rsqrt(jnp.mean(x * x, axis=-1, keepdims=True) + EPS)


def _head_meansq(x, bd):
    x2 = x * x
    hi = x2.astype(BF16)
    lo = (x2 - hi.astype(F32)).astype(BF16)
    return (_dot(hi, bd) + _dot(lo, bd)) * (1.0 / HEAD_DIM)


def _mem_attention(qn, mkT, mv, hm_ref):
    out = jnp.zeros(qn.shape, F32)
    for h in range(MEM_HEADS):
        hm = hm_ref[h:h + 1, :]
        s = _dot((qn * hm).astype(BF16), mkT)
        e = jnp.exp2(s - jnp.max(s, axis=-1, keepdims=True))
        l = jnp.sum(e, axis=-1, keepdims=True)
        out = out + _dot(e.astype(BF16), mv) * (hm / l)
    return out


def _memkv_kernel(mem_ref, g_ref, w_ref, kg_ref, bd_ref, mkT_ref, mv_ref):
    m = mem_ref[...]
    mn = (m * _rms_scale(m) * g_ref[...]).astype(BF16)
    kv = _dot(mn, w_ref[0])
    k = kv[:, :MEM_W]
    kn = k * lax.rsqrt(_head_meansq(k, bd_ref[...]) + EPS) * kg_ref[0]
    mkT_ref[0] = (kn * (QK_SCALE * LOG2E)).T.astype(BF16)
    mv_ref[0] = kv[:, MEM_W:].astype(BF16)


def _memkv(mem, g, w, kg, bd):
    depth = w.shape[0]
    return pl.pallas_call(
        _memkv_kernel,
        grid=(depth,),
        in_specs=[
            pl.BlockSpec((MEM_LEN, D_MODEL), lambda l: (0, 0)),
            pl.BlockSpec((1, D_MODEL), lambda l: (0, 0)),
            pl.BlockSpec((1, D_MODEL, 2 * MEM_W), lambda l: (l, 0, 0)),
            pl.BlockSpec((1, 1, MEM_W), lambda l: (l, 0, 0)),
            pl.BlockSpec((MEM_W, MEM_W), lambda l: (0, 0)),
        ],
        out_specs=[
            pl.BlockSpec((1, MEM_W, MEM_LEN), lambda l: (l, 0, 0)),
            pl.BlockSpec((1, MEM_LEN, MEM_W), lambda l: (l, 0, 0)),
        ],
        out_shape=[
            jax.ShapeDtypeStruct((depth, MEM_W, MEM_LEN), BF16),
            jax.ShapeDtypeStruct((depth, MEM_LEN, MEM_W), BF16),
        ],
        name="mem_kv",
    )(mem, g, w, kg, bd)


def _layer0_kernel(x_ref, g_ref, win_ref, bglu_ref, dw_ref, dwb_ref, lng_ref, lnb_ref, qg_ref,
                   mkT_ref, mv_ref, bd_ref, hm_ref, conv_ref, memo_ref, buf_ref, cv_ref, sh_ref):
    i = pl.program_id(0)
    tm = x_ref.shape[0]
    x = x_ref[...]
    h = (x * _rms_scale(x) * g_ref[...]).astype(BF16)
    u = _dot(h, win_ref[...])
    a = u[:, :CONV_W] + bglu_ref[:, :CONV_W]
    gate = u[:, CONV_W:2 * CONV_W] + bglu_ref[:, CONV_W:]
    v = a * jax.nn.sigmoid(gate)

    @pl.when(i == 0)
    def _():
        buf_ref[0:CONV_HALO, :] = jnp.zeros((CONV_HALO, CONV_W), F32)

    buf_ref[CONV_HALO:CONV_HALO + tm, :] = v

    base = CONV_HALO - (CONV_K - 1)
    rows = tm + CONV_HALO
    for cb in range(CONV_W // 128):
        cs = slice(cb * 128, (cb + 1) * 128)
        xb = buf_ref[:, cs]
        sh_ref[0] = xb
        for b in range(1, 8):
            sh_ref[b] = pltpu.roll(xb, rows - b, axis=0)

        def conv_rows(r, carry, cs=cs):
            r0 = pl.multiple_of(r * CONV_ROWS, CONV_ROWS)
            acc = jnp.zeros((CONV_ROWS, 128), F32) + dwb_ref[:, cs]
            for k in range(CONV_K):
                a, b = divmod(base + k, 8)
                acc = acc + dw_ref[k:k + 1, cs] * sh_ref[b, pl.ds(r0 + 8 * a, CONV_ROWS), :]
            cv_ref[pl.ds(r0, CONV_ROWS), cs] = acc
            return carry

        lax.fori_loop(0, tm // CONV_ROWS, conv_rows, 0)
    buf_ref[0:CONV_HALO, :] = buf_ref[tm:tm + CONV_HALO, :]

    cv = cv_ref[...]
    mu = jnp.mean(cv, axis=-1, keepdims=True)
    d = cv - mu
    var = jnp.mean(d * d, axis=-1, keepdims=True)
    y = d * lax.rsqrt(var + LN_EPS) * lng_ref[...] + lnb_ref[...]
    conv_ref[...] = (y * jax.nn.sigmoid(y)).astype(BF16)

    qm = u[:, 2 * CONV_W:]
    qn = qm * lax.rsqrt(_head_meansq(qm, bd_ref[...]) + EPS) * qg_ref[...]
    memo_ref[...] = _mem_attention(qn, mkT_ref[0], mv_ref[0], hm_ref).astype(BF16)


def _layer0(x, g, w_in, bglu, dw, dwb, lng, lnb, qg, mkT, mv, bd, hm):
    s = x.shape[0]
    tm = ROW_TILE
    const = lambda i: (0, 0)
    return pl.pallas_call(
        _layer0_kernel,
        grid=(s // tm,),
        in_specs=[
            pl.BlockSpec((tm, D_MODEL), lambda i: (i, 0)),
            pl.BlockSpec((1, D_MODEL), const),
            pl.BlockSpec(w_in.shape, const),
            pl.BlockSpec((1, 2 * CONV_W), const),
            pl.BlockSpec((CONV_K, CONV_W), const),
            pl.BlockSpec((1, CONV_W), const),
            pl.BlockSpec((1, CONV_W), const),
            pl.BlockSpec((1, CONV_W), const),
            pl.BlockSpec((1, MEM_W), const),
            pl.BlockSpec((1, MEM_W, MEM_LEN), lambda i: (0, 0, 0)),
            pl.BlockSpec((1, MEM_LEN, MEM_W), lambda i: (0, 0, 0)),
            pl.BlockSpec((MEM_W, MEM_W), const),
            pl.BlockSpec((MEM_HEADS, MEM_W), const),
        ],
        out_specs=[
            pl.BlockSpec((tm, CONV_W), lambda i: (i, 0)),
            pl.BlockSpec((tm, MEM_W), lambda i: (i, 0)),
        ],
        out_shape=[
            jax.ShapeDtypeStruct((s, CONV_W), BF16),
            jax.ShapeDtypeStruct((s, MEM_W), BF16),
        ],
        scratch_shapes=[
            pltpu.VMEM((tm + CONV_HALO, CONV_W), F32),
            pltpu.VMEM((tm, CONV_W), F32),
            pltpu.VMEM((8, tm + CONV_HALO, 128), F32),
        ],
        compiler_params=pltpu.CompilerParams(
            dimension_semantics=("arbitrary",), vmem_limit_bytes=VMEM_LIMIT),
        name="layer0_mixer",
    )(x, g, w_in, bglu, dw, dwb, lng, lnb, qg, mkT, mv, bd, hm)


def _post_kernel(x_ref, ma_ref, mb_ref, wa_ref, wb_ref, g_ref, win_ref, wout_ref, o_ref):
    o_ref[...] = x_ref[...] + _dot(ma_ref[...], wa_ref[...]) + _dot(mb_ref[...], wb_ref[...])
    x1 = o_ref[...]
    h = (x1 * _rms_scale(x1) * g_ref[...]).astype(BF16)
    for c in range(D_FF // FF_CHUNK):
        cs = slice(c * FF_CHUNK, (c + 1) * FF_CHUNK)
        t = jnp.maximum(_dot(h, win_ref[:, cs]), 0.0)
        o_ref[...] += _dot((t * t).astype(BF16), wout_ref[cs, :])


def _post(x, ma, mb, wa, wb, g, w_in, w_out):
    s = x.shape[0]
    tm = ROW_TILE
    const = lambda i: (0, 0)
    single = pl.Buffered(1)
    return pl.pallas_call(
        _post_kernel,
        grid=(s // tm,),
        in_specs=[
            pl.BlockSpec((tm, D_MODEL), lambda i: (i, 0)),
            pl.BlockSpec((tm, CONV_W), lambda i: (i, 0)),
            pl.BlockSpec((tm, MEM_W), lambda i: (i, 0)),
            pl.BlockSpec((CONV_W, D_MODEL), const, pipeline_mode=single),
            pl.BlockSpec((MEM_W, D_MODEL), const, pipeline_mode=single),
            pl.BlockSpec((1, D_MODEL), const),
            pl.BlockSpec((D_MODEL, D_FF), const, pipeline_mode=single),
            pl.BlockSpec((D_FF, D_MODEL), const, pipeline_mode=single),
        ],
        out_specs=pl.BlockSpec((tm, D_MODEL), lambda i: (i, 0)),
        out_shape=jax.ShapeDtypeStruct((s, D_MODEL), F32),
        compiler_params=pltpu.CompilerParams(
            dimension_semantics=("arbitrary",), vmem_limit_bytes=VMEM_LIMIT),
        name="outproj_mlp",
    )(x, ma, mb, wa, wb, g, w_in, w_out)


def _layer1_proj_kernel(x_ref, gkv_ref, gmix_ref, wkv_ref, win_ref, gb_ref, qg_ref, mqg_ref, kng_ref,
                        mkT_ref, mv_ref, bd_ref, hm_ref, pat_ref,
                        qT_ref, gT_ref, ksel_ref, kwin_ref, vselT_ref, vwinT_ref, kvc_ref, memo_ref):
    tm = x_ref.shape[0]
    x = x_ref[...]
    xn = x * _rms_scale(x)
    bd = bd_ref[...]

    kv = _dot((xn * gkv_ref[...]).astype(BF16), wkv_ref[...])
    kvc_ref[...] = kv[:, 0:256].astype(BF16)
    k2 = kv[:, 256:384]
    v2 = kv[:, 384:512]
    kw = kv[:, 512:640]
    vw = kv[:, 640:768]
    bd2 = bd[0:128, 0:128]
    k2n = k2 * lax.rsqrt(_head_meansq(k2, bd2) + EPS) * kng_ref[0:1, :]
    kwn = kw * lax.rsqrt(_head_meansq(kw, bd2) + EPS) * kng_ref[1:2, :]

    lane = lax.broadcasted_iota(jnp.int32, (tm, 128), 1)
    pat = pat_ref[...]
    ksel_ref[0] = jnp.where(lane < HEAD_DIM, k2n, pat).astype(BF16)
    ksel_ref[1] = jnp.where(lane < HEAD_DIM, pltpu.roll(k2n, HEAD_DIM, axis=1), pat).astype(BF16)
    kwin_ref[0] = kwn[:, :HEAD_DIM].astype(BF16)
    kwin_ref[1] = kwn[:, HEAD_DIM:].astype(BF16)

    ones_rows = (lax.broadcasted_iota(jnp.int32, (VAUG_ROWS - HEAD_DIM, tm), 0) == 0).astype(BF16)
    v2T = v2.T.astype(BF16)
    vwT = vw.T.astype(BF16)
    for g in range(GROUPS):
        vselT_ref[g, 0:HEAD_DIM, :] = v2T[g * HEAD_DIM:(g + 1) * HEAD_DIM, :]
        vselT_ref[g, HEAD_DIM:VAUG_ROWS, :] = ones_rows
        vwinT_ref[g, 0:HEAD_DIM, :] = vwT[g * HEAD_DIM:(g + 1) * HEAD_DIM, :]
        vwinT_ref[g, HEAD_DIM:VAUG_ROWS, :] = ones_rows

    u = _dot((xn * gmix_ref[...]).astype(BF16), win_ref[...])
    for cb in range(NSA_W // 256):
        cs = slice(cb * 256, (cb + 1) * 256)
        qc = u[:, cs]
        qn = qc * lax.rsqrt(_head_meansq(qc, bd) + EPS) * qg_ref[:, cs]
        qT_ref[4 * cb:4 * cb + 4] = qn.T.astype(BF16).reshape(4, HEAD_DIM, tm)

    gates = jax.nn.sigmoid(u[:, NSA_W + MEM_W:] + gb_ref[...])
    gT_ref[...] = gates.T

    qm = u[:, NSA_W:NSA_W + MEM_W]
    qmn = qm * lax.rsqrt(_head_meansq(qm, bd) + EPS) * mqg_ref[...]
    memo_ref[...] = _mem_attention(qmn, mkT_ref[0], mv_ref[0], hm_ref).astype(BF16)


def _layer1_proj(x, gkv, gmix, wkv, w_in, gb, qg, mqg, kng, mkT, mv, bd, hm, pat):
    s = x.shape[0]
    tm = ROW_TILE
    const = lambda i: (0, 0)
    pat_blocks = pat.shape[0] // tm
    return pl.pallas_call(
        _layer1_proj_kernel,
        grid=(s // tm,),
        in_specs=[
            pl.BlockSpec((tm, D_MODEL), lambda i: (i, 0)),
            pl.BlockSpec((1, D_MODEL), const),
            pl.BlockSpec((1, D_MODEL), const),
            pl.BlockSpec(wkv.shape, const),
            pl.BlockSpec(w_in.shape, const),
            pl.BlockSpec((1, GATE_PAD), const),
            pl.BlockSpec((1, NSA_W), const),
            pl.BlockSpec((1, MEM_W), const),
            pl.BlockSpec((2, 128), const),
            pl.BlockSpec((1, MEM_W, MEM_LEN), lambda i: (1, 0, 0)),
            pl.BlockSpec((1, MEM_LEN, MEM_W), lambda i: (1, 0, 0)),
            pl.BlockSpec((MEM_W, MEM_W), const),
            pl.BlockSpec((MEM_HEADS, MEM_W), const),
            pl.BlockSpec((tm, 128), lambda i: (i % pat_blocks, 0)),
        ],
        out_specs=[
            pl.BlockSpec((NSA_HEADS, HEAD_DIM, tm), lambda i: (0, 0, i)),
            pl.BlockSpec((GATE_PAD, tm), lambda i: (0, i)),
            pl.BlockSpec((GROUPS, tm, 128), lambda i: (0, i, 0)),
            pl.BlockSpec((GROUPS, tm, HEAD_DIM), lambda i: (0, i, 0)),
            pl.BlockSpec((GROUPS, VAUG_ROWS, tm), lambda i: (0, 0, i)),
            pl.BlockSpec((GROUPS, VAUG_ROWS, tm), lambda i: (0, 0, i)),
            pl.BlockSpec((tm, 256), lambda i: (i, 0)),
            pl.BlockSpec((tm, MEM_W), lambda i: (i, 0)),
        ],
        out_shape=[
            jax.ShapeDtypeStruct((NSA_HEADS, HEAD_DIM, s), BF16),
            jax.ShapeDtypeStruct((GATE_PAD, s), F32),
            jax.ShapeDtypeStruct((GROUPS, s, 128), BF16),
            jax.ShapeDtypeStruct((GROUPS, s, HEAD_DIM), BF16),
            jax.ShapeDtypeStruct((GROUPS, VAUG_ROWS, s), BF16),
            jax.ShapeDtypeStruct((GROUPS, VAUG_ROWS, s), BF16),
            jax.ShapeDtypeStruct((s, 256), BF16),
            jax.ShapeDtypeStruct((s, MEM_W), BF16),
        ],
        compiler_params=pltpu.CompilerParams(
            dimension_semantics=("arbitrary",), vmem_limit_bytes=VMEM_LIMIT),
        name="layer1_proj",
    )(x, gkv, gmix, wkv, w_in, gb, qg, mqg, kng, mkT, mv, bd, hm, pat)


def _compress_kernel(t_ref, w1_ref, w2_ref, w2T_ref, pe_ref, kg_ref, kgT_ref, o_ref, oT_ref, hb_ref):
    nc = t_ref.shape[1]
    half = CMP_STRIDE * HEAD_DIM
    t = t_ref[0]
    ha = _dot(t, w1_ref[0, 0:half, :])
    hb_ref[0:nc, :] = _dot(t, w1_ref[0, half:2 * half, :])
    hb_ref[nc:nc + 8, :] = jnp.zeros((8, CMP_HID), F32)
    pe = jnp.broadcast_to(pe_ref[0], (8, 2 * half)).astype(BF16)
    pe_term = _dot(pe, w1_ref[0])[0:1, :]
    h = ha + hb_ref[1:nc + 1, :] + pe_term
    hg = 0.5 * h * (1.0 + jnp.tanh(0.7978845608028654 * (h + 0.044715 * (h * h * h))))
    hg = hg.astype(BF16)
    o = _dot(hg, w2_ref[0])
    oT = lax.dot_general(w2T_ref[0], hg, (((1,), (1,)), ((), ())), preferred_element_type=F32)
    is_k = pl.program_id(0) < GROUPS
    on = o * lax.rsqrt(jnp.mean(o * o, axis=1, keepdims=True) + EPS) * kg_ref[...]
    oTn = oT * lax.rsqrt(jnp.mean(oT * oT, axis=0, keepdims=True) + EPS) * kgT_ref[...]
    o_ref[0] = jnp.where(is_k, on, o).astype(BF16)
    oT_ref[0, 0:HEAD_DIM, :] = jnp.where(is_k, oTn, oT).astype(BF16)
    oT_ref[0, HEAD_DIM:VAUG_ROWS, :] = (
        lax.broadcasted_iota(jnp.int32, (VAUG_ROWS - HEAD_DIM, nc), 0) == 0).astype(BF16)


def _compress(t, w1, w2, w2T, pe, kg, kgT):
    n, nc, width = t.shape
    return pl.pallas_call(
        _compress_kernel,
        grid=(n,),
        in_specs=[
            pl.BlockSpec((1, nc, width), lambda i: (i, 0, 0)),
            pl.BlockSpec((1, 2 * width, CMP_HID), lambda i: (i // GROUPS, 0, 0)),
            pl.BlockSpec((1, CMP_HID, HEAD_DIM), lambda i: (i // GROUPS, 0, 0)),
            pl.BlockSpec((1, HEAD_DIM, CMP_HID), lambda i: (i // GROUPS, 0, 0)),
            pl.BlockSpec((1, 1, 2 * width), lambda i: (i // GROUPS, 0, 0)),
            pl.BlockSpec((1, HEAD_DIM), lambda i: (0, 0)),
            pl.BlockSpec((HEAD_DIM, 1), lambda i: (0, 0)),
        ],
        out_specs=[
            pl.BlockSpec((1, nc, HEAD_DIM), lambda i: (i, 0, 0)),
            pl.BlockSpec((1, VAUG_ROWS, nc), lambda i: (i, 0, 0)),
        ],
        out_shape=[
            jax.ShapeDtypeStruct((n, nc, HEAD_DIM), BF16),
            jax.ShapeDtypeStruct((n, VAUG_ROWS, nc), BF16),
        ],
        scratch_shapes=[pltpu.VMEM((nc + 8, CMP_HID), F32)],
        compiler_params=pltpu.CompilerParams(
            dimension_semantics=("arbitrary",), vmem_limit_bytes=VMEM_LIMIT),
        name="compress_kv",
    )(t, w1, w2, w2T, pe, kg, kgT)


def _nsa_kernel(qT_ref, gT_ref, kc_ref, vcT_ref, ksel_ref, vselT_ref, kwin_ref, vwinT_ref, out_ref,
                qaug_ref, psum_ref, bias_ref, acc_ref, m_ref, s_ref, oc_ref):
    c = pl.program_id(1)
    nc = kc_ref.shape[1]
    nsb = bias_ref.shape[0]
    hg = HEADS_PER_GROUP
    width = hg * Q_BLOCK

    for hh in range(hg):
        qaug_ref[0:HEAD_DIM, hh * Q_BLOCK:(hh + 1) * Q_BLOCK] = qT_ref[hh]
    qaug_ref[HEAD_DIM:128, :] = jnp.zeros((128 - HEAD_DIM, width), BF16)
    qT = qaug_ref[0:HEAD_DIM, :]
    t_q = c * Q_BLOCK + (lax.broadcasted_iota(jnp.int32, (1, width), 1) & (Q_BLOCK - 1))

    def cmp_branch(rows):
        s = _dot(kc_ref[0, 0:rows, :], qT)
        ci = lax.broadcasted_iota(jnp.int32, (rows, 1), 0)
        s = jnp.where(ci * CMP_STRIDE + (CMP_L - 1) <= t_q, s, -jnp.inf)
        m = jnp.max(s, axis=0, keepdims=True)
        m = jnp.where(m == -jnp.inf, 0.0, m)
        e = jnp.exp2(s - m)
        oa = _dot(vcT_ref[0, :, 0:rows], e.astype(BF16))
        rl = 1.0 / jnp.maximum(oa[HEAD_DIM:HEAD_DIM + 1, :], 1e-30)
        oc_ref[...] = oa[0:HEAD_DIM, :] * rl
        p = e * rl
        psum = p[:, 0:Q_BLOCK]
        for hh in range(1, hg):
            psum = psum + p[:, hh * Q_BLOCK:(hh + 1) * Q_BLOCK]
        psum_ref[8:8 + rows, :] = psum

    @pl.when(c == 0)
    def _():
        psum_ref[...] = jnp.zeros(psum_ref.shape, F32)

    cuts = [nc * k // CMP_PATHS for k in range(1, CMP_PATHS + 1)] if nc % (128 * CMP_PATHS) == 0 else [nc]
    need = (c + 1) * (Q_BLOCK // CMP_STRIDE)
    for k, rows in enumerate(cuts):
        lo = cuts[k - 1] if k else 0
        pl.when((need > lo) & (need <= rows))(functools.partial(cmp_branch, rows))
    o_cmp = oc_ref[...]

    row0 = pl.multiple_of(jnp.maximum(c * Q_BLOCK - WIN, 0), Q_BLOCK)
    sw = _dot(kwin_ref[0, pl.ds(row0, WIN_KEYS), :], qT)
    kp = row0 + lax.broadcasted_iota(jnp.int32, (WIN_KEYS, 1), 0)
    sw = jnp.where((kp <= t_q) & (kp > t_q - WIN), sw, -jnp.inf)
    mw = jnp.max(sw, axis=0, keepdims=True)
    mw = jnp.where(mw == -jnp.inf, 0.0, mw)
    ow = _dot(vwinT_ref[0, :, pl.ds(row0, WIN_KEYS)], jnp.exp2(sw - mw).astype(BF16))
    o_win = ow[0:HEAD_DIM, :] / jnp.maximum(ow[HEAD_DIM:HEAD_DIM + 1, :], 1e-30)

    imp = (psum_ref[pl.ds(7, nsb, stride=4), :] + psum_ref[pl.ds(11, nsb, stride=4), :]
           + 2.0 * (psum_ref[pl.ds(8, nsb, stride=4), :] + psum_ref[pl.ds(9, nsb, stride=4), :]
                    + psum_ref[pl.ds(10, nsb, stride=4), :]))

    blk = lax.broadcasted_iota(jnp.int32, (nsb, Q_BLOCK), 0)
    tb = (c * Q_BLOCK + lax.broadcasted_iota(jnp.int32, (1, Q_BLOCK), 1)) >> 6
    valid = blk <= tb
    forced = (blk == 0) | (blk == tb) | (blk == tb - 1)
    score = jnp.where(forced | ~valid, -jnp.inf, imp)

    blk_f = blk.astype(F32)

    def pick(_, sc):
        mx = jnp.max(sc, axis=0, keepdims=True)
        first = jnp.min(jnp.where(sc == mx, blk_f, float(nsb)), axis=0, keepdims=True)
        return jnp.where(blk_f == first, -jnp.inf, sc)

    left = lax.fori_loop(0, N_SEL - 3, pick, score)
    sel = (left == -jnp.inf) & valid
    bias_ref[...] = jnp.where(sel, 0.0, MASK_BIAS).astype(BF16)

    acc_ref[...] = jnp.zeros(acc_ref.shape, F32)
    m_ref[...] = jnp.full(m_ref.shape, M_INIT, F32)

    n_chunks = KEY_TILE // KEY_CHUNK

    def set_bias(j):
        span = j // (ONEHOT_SPAN // KEY_TILE)
        b16 = bias_ref[pl.ds(pl.multiple_of(span * SEL_PER_SPAN, SEL_PER_SPAN), SEL_PER_SPAN), :]
        for hh in range(hg):
            qaug_ref[HEAD_DIM:HEAD_DIM + SEL_PER_SPAN, hh * Q_BLOCK:(hh + 1) * Q_BLOCK] = b16

    def scores(j, r):
        k0 = pl.multiple_of(j * KEY_TILE + r * KEY_CHUNK, KEY_CHUNK)
        return _dot(ksel_ref[0, pl.ds(k0, KEY_CHUNK), :], qaug_ref[...])

    def chunk_max(cmax, sc):
        return jnp.maximum(cmax, jnp.max(sc.reshape(KEY_CHUNK // 8, 8, width), axis=0))

    def new_max(cmax):
        m_old = m_ref[...]
        m_new = jnp.maximum(m_old, jnp.max(cmax, axis=0, keepdims=True))
        m_ref[...] = m_new
        return m_new, jnp.exp2(m_old - m_new)

    def values(j, r):
        k0 = pl.multiple_of(j * KEY_TILE + r * KEY_CHUNK, KEY_CHUNK)
        return vselT_ref[0, :, pl.ds(k0, KEY_CHUNK)]

    cmax0 = jnp.full((8, width), MASK_BIAS, F32)
    set_bias(0)
    cmax = cmax0
    for r in range(n_chunks):
        sc = scores(0, r)
        s_ref[r * KEY_CHUNK:(r + 1) * KEY_CHUNK, :] = sc
        cmax = chunk_max(cmax, sc)

    last = c // (KEY_TILE // Q_BLOCK)

    def pipe_step(j, cmax):
        m_new, alpha = new_max(cmax)
        set_bias(j + 1)
        acc = acc_ref[...] * alpha
        cnext = cmax0
        for r in range(n_chunks):
            rs = slice(r * KEY_CHUNK, (r + 1) * KEY_CHUNK)
            sc = scores(j + 1, r)
            pj = jnp.exp2(s_ref[rs, :] - m_new).astype(BF16)
            s_ref[rs, :] = sc
            cnext = chunk_max(cnext, sc)
            acc = acc + _dot(values(j, r), pj)
        acc_ref[...] = acc
        return cnext

    lax.fori_loop(0, last, pipe_step, cmax)

    diag = pl.ds(pl.multiple_of((c % (KEY_TILE // Q_BLOCK)) * Q_BLOCK, Q_BLOCK), Q_BLOCK)
    ku = lax.broadcasted_iota(jnp.int32, (Q_BLOCK, 1), 0)
    s_ref[diag, :] = jnp.where(ku <= t_q - c * Q_BLOCK, s_ref[diag, :], MASK_BIAS)

    def finish(rows):
        s_last = s_ref[0:rows, :]
        m_new, alpha = new_max(jnp.max(s_last.reshape(rows // 8, 8, width), axis=0))
        p_last = jnp.exp2(s_last - m_new).astype(BF16)
        acc_ref[...] = acc_ref[...] * alpha + _dot(
            vselT_ref[0, :, pl.ds(pl.multiple_of(last * KEY_TILE, KEY_TILE), rows)], p_last)

    diag_chunk = (c % (KEY_TILE // Q_BLOCK)) // (KEY_CHUNK // Q_BLOCK)
    for k in range(n_chunks):
        pl.when(diag_chunk == k)(functools.partial(finish, (k + 1) * KEY_CHUNK))
    o_slc = acc_ref[0:HEAD_DIM, :] / jnp.maximum(acc_ref[HEAD_DIM:HEAD_DIM + 1, :], 1e-30)

    heads = []
    for hh in range(hg):
        hs = slice(hh * Q_BLOCK, (hh + 1) * Q_BLOCK)
        heads.append(o_cmp[:, hs] * gT_ref[hh:hh + 1, :] + o_slc[:, hs] * gT_ref[8 + hh:9 + hh, :]
                     + o_win[:, hs] * gT_ref[16 + hh:17 + hh, :])
    for pr in range(hg // 2):
        pair = jnp.concatenate([heads[2 * pr], heads[2 * pr + 1]], axis=0)
        out_ref[:, pr * 128:(pr + 1) * 128] = pair.T.astype(BF16)


def _nsa(qT, gT, kc, vcT, ksel, vselT, kwin, vwinT):
    s = ksel.shape[1]
    nc = kc.shape[1]
    nsb = s // SEL_L
    hg = HEADS_PER_GROUP
    grp = lambda g, c: (g, 0, 0)
    return pl.pallas_call(
        _nsa_kernel,
        grid=(GROUPS, s // Q_BLOCK),
        in_specs=[
            pl.BlockSpec((hg, HEAD_DIM, Q_BLOCK), lambda g, c: (g, 0, c)),
            pl.BlockSpec((GATE_ROWS, Q_BLOCK), lambda g, c: (g, c)),
            pl.BlockSpec((1, nc, HEAD_DIM), grp),
            pl.BlockSpec((1, VAUG_ROWS, nc), lambda g, c: (GROUPS + g, 0, 0)),
            pl.BlockSpec((1, s, 128), grp),
            pl.BlockSpec((1, VAUG_ROWS, s), grp),
            pl.BlockSpec((1, s, HEAD_DIM), grp),
            pl.BlockSpec((1, VAUG_ROWS, s), grp),
        ],
        out_specs=pl.BlockSpec((Q_BLOCK, hg * HEAD_DIM), lambda g, c: (c, g)),
        out_shape=jax.ShapeDtypeStruct((s, NSA_W), BF16),
        scratch_shapes=[
            pltpu.VMEM((128, hg * Q_BLOCK), BF16),
            pltpu.VMEM((nc + 16, Q_BLOCK), F32),
            pltpu.VMEM((nsb, Q_BLOCK), BF16),
            pltpu.VMEM((VAUG_ROWS, hg * Q_BLOCK), F32),
            pltpu.VMEM((1, hg * Q_BLOCK), F32),
            pltpu.VMEM((KEY_TILE, hg * Q_BLOCK), F32),
            pltpu.VMEM((HEAD_DIM, hg * Q_BLOCK), F32),
        ],
        compiler_params=pltpu.CompilerParams(
            dimension_semantics=("arbitrary", "arbitrary"), vmem_limit_bytes=VMEM_LIMIT),
        name="nsa_attention",
    )(qT, gT, kc, vcT, ksel, vselT, kwin, vwinT)


def _block_diag_ones():
    idx = np.arange(MEM_W) // HEAD_DIM
    return jnp.asarray((idx[:, None] == idx[None, :]).astype(np.float32), BF16)


def _head_masks():
    idx = np.arange(MEM_W) // HEAD_DIM
    return jnp.asarray((idx[None, :] == np.arange(MEM_HEADS)[:, None]).astype(np.float32))


def _onehot_pattern():
    rows = np.arange(ONEHOT_SPAN)[:, None] // SEL_L
    lanes = np.arange(128)[None, :] - HEAD_DIM
    return jnp.asarray((rows == lanes).astype(np.float32))


def _gate_layout():
    src = np.full((GATE_PAD,), -1, np.int64)
    for g in range(GROUPS):
        for b in range(3):
            for hh in range(HEADS_PER_GROUP):
                src[g * GATE_ROWS + b * 8 + hh] = (g * HEADS_PER_GROUP + hh) * 3 + b
    return src


def kernel(x, mem, norm_mix_g, norm_mlp_g, mem_norm_g, w_mem_kv, mem_q_norm_g, mem_k_norm_g, w_out, w_mlp_in, w_mlp_out, a_w_in, a_b_glu, a_dw, a_dw_b, a_ln_g, a_ln_b, b_w_in, b_gate_b, b_q_norm_g, kv_norm_g, w_kv, k_norm_g, cmp_pe_k, cmp_pe_v, cmp_w1_k, cmp_w2_k, cmp_w1_v, cmp_w2_v):
    batch, s, _ = x.shape
    assert batch == 1 and s % ONEHOT_SPAN == 0 and s >= WIN_KEYS
    assert w_out.shape[0] == 2 and a_w_in.shape[0] == 1 and b_w_in.shape[0] == 1
    nc = s // CMP_STRIDE
    row = lambda v: v.reshape(1, -1)
    bd = _block_diag_ones()
    hm = _head_masks()

    mkT, mv = _memkv(mem[0], row(mem_norm_g), w_mem_kv.astype(BF16),
                     jnp.tile(mem_k_norm_g, (1, MEM_HEADS))[:, None, :], bd)

    conv0, memo0 = _layer0(
        x[0], row(norm_mix_g[0]), a_w_in[0].astype(BF16), row(a_b_glu[0]), a_dw[0], row(a_dw_b[0]),
        row(a_ln_g[0]), row(a_ln_b[0]), row(jnp.tile(mem_q_norm_g[0], MEM_HEADS)), mkT, mv, bd, hm)
    w_out_bf = w_out.astype(BF16)
    w_in_bf = w_mlp_in.astype(BF16)
    w_o_bf = w_mlp_out.astype(BF16)
    x1 = _post(x[0], conv0, memo0, w_out_bf[0, :CONV_W], w_out_bf[0, CONV_W:], row(norm_mlp_g[0]),
               w_in_bf[0], w_o_bf[0])

    src = _gate_layout()
    used = src >= 0
    w_gate = jnp.where(used[None, :], b_w_in[0][:, NSA_W + MEM_W + np.maximum(src, 0)], 0.0)
    b_gate = jnp.where(used, b_gate_b[0][np.maximum(src, 0)], 0.0)
    w_in1 = jnp.concatenate([b_w_in[0][:, :NSA_W + MEM_W], w_gate], axis=1).astype(BF16)
    q_gain = jnp.tile(b_q_norm_g[0], NSA_HEADS) * (QK_SCALE * LOG2E)
    kng = jnp.stack([jnp.tile(k_norm_g[1], GROUPS), jnp.tile(k_norm_g[2], GROUPS)])
    qT, gT, ksel, kwin, vselT, vwinT, kvc, memo1 = _layer1_proj(
        x1, row(kv_norm_g), row(norm_mix_g[1]), w_kv.astype(BF16), w_in1, row(b_gate), row(q_gain),
        row(jnp.tile(mem_q_norm_g[1], MEM_HEADS)), kng, mkT, mv, bd, hm, _onehot_pattern())

    t = kvc.reshape(s, 2 * GROUPS, HEAD_DIM).transpose(1, 0, 2).reshape(2 * GROUPS, nc, CMP_STRIDE * HEAD_DIM)
    w1 = jnp.stack([cmp_w1_k, cmp_w1_v]).astype(BF16)
    w2 = jnp.stack([cmp_w2_k, cmp_w2_v]).astype(BF16)
    pe = jnp.stack([cmp_pe_k.reshape(1, -1), cmp_pe_v.reshape(1, -1)])
    cmp_rows, cmp_cols = _compress(t, w1, w2, w2.transpose(0, 2, 1), pe, row(k_norm_g[0]),
                                   k_norm_g[0].reshape(-1, 1))

    nsa = _nsa(qT, gT, cmp_rows, cmp_cols, ksel, vselT, kwin, vwinT)
    x2 = _post(x1, nsa, memo1, w_out_bf[1, :NSA_W], w_out_bf[1, NSA_W:], row(norm_mlp_g[1]),
               w_in_bf[1], w_o_bf[1])
    return x2[None]
```

```python
import functools

import numpy as np
import jax
import jax.numpy as jnp
from jax import lax
from jax.experimental import pallas as pl
from jax.experimental.pallas import tpu as pltpu

F32 = jnp.float32
BF16 = jnp.bfloat16

D_MODEL = 1024
HEAD_DIM = 64
MEM_LEN = 256
MEM_HEADS = 4
MEM_W = MEM_HEADS * HEAD_DIM
CONV_W = D_MODEL - MEM_W
CONV_K = 31
NSA_HEADS = CONV_W // HEAD_DIM
NSA_W = NSA_HEADS * HEAD_DIM
GROUPS = 2
HEADS_PER_GROUP = NSA_HEADS // GROUPS
CMP_L = 32
CMP_STRIDE = 16
CMP_HID = 256
SEL_L = 64
N_SEL = 16
WIN = 512
Q_BLOCK = 128
D_FF = 4 * D_MODEL
GATE_PAD = 128
GATE_ROWS = GATE_PAD // GROUPS

ROW_TILE = 512
FF_CHUNK = 1024
KEY_TILE = 1024
ONEHOT_SPAN = 1024
CMP_PATHS = 4
SEL_PATHS = 4
KEY_CHUNK = 256
SEL_PER_SPAN = ONEHOT_SPAN // SEL_L
WIN_KEYS = WIN + Q_BLOCK
CONV_HALO = 32
CONV_ROWS = 64
VAUG_ROWS = 80
VMEM_LIMIT = 56 * 1024 * 1024
MXU_WIDTH = 256

EPS = 1e-6
LN_EPS = 1e-5
LOG2E = 1.4426950408889634
QK_SCALE = HEAD_DIM ** -0.5
MASK_BIAS = -2.0 ** 126
M_INIT = -2.0 ** 100


def _dot(a, b):
    return jnp.dot(a, b, preferred_element_type=F32)


def _dot_3tiles(a, b):
    two = 2 * MXU_WIDTH
    half = a.shape[0] // 2
    return jnp.concatenate([
        _dot(a, b[:, 0:two]),
        jnp.concatenate([_dot(a[0:half], b[:, two:]), _dot(a[half:], b[:, two:])], axis=0),
    ], axis=1)


def _rms_scale(x):
    return lax.rsqrt(jnp.mean(x * x, axis=-1, keepdims=True) + EPS)


def _head_meansq(x, bd):
    x2 = x * x
    hi = x2.astype(BF16)
    lo = (x2 - hi.astype(F32)).astype(BF16)
    return (_dot(hi, bd) + _dot(lo, bd)) * (1.0 / HEAD_DIM)


def _mem_attention(qn, mkT, mv, hm_ref):
    out = jnp.zeros(qn.shape, F32)
    for h in range(MEM_HEADS):
        hm = hm_ref[h:h + 1, :]
        s = _dot((qn * hm).astype(BF16), mkT)
        e = jnp.exp2(s - jnp.max(s, axis=-1, keepdims=True))
        l = jnp.sum(e, axis=-1, keepdims=True)
        out = out + _dot(e.astype(BF16), mv) * (hm / l)
    return out


def _memkv_kernel(mem_ref, g_ref, w_ref, kg_ref, bd_ref, mkT_ref, mv_ref):
    m = mem_ref[...]
    mn = (m * _rms_scale(m) * g_ref[...]).astype(BF16)
    kv = _dot(mn, w_ref[0])
    k = kv[:, :MEM_W]
    kn = k * lax.rsqrt(_head_meansq(k, bd_ref[...]) + EPS) * kg_ref[0]
    mkT_ref[0] = (kn * (QK_SCALE * LOG2E)).T.astype(BF16)
    mv_ref[0] = kv[:, MEM_W:].astype(BF16)


def _memkv(mem, g, w, kg, bd):
    depth = w.shape[0]
    return pl.pallas_call(
        _memkv_kernel,
        grid=(depth,),
        in_specs=[
            pl.BlockSpec((MEM_LEN, D_MODEL), lambda l: (0, 0)),
            pl.BlockSpec((1, D_MODEL), lambda l: (0, 0)),
            pl.BlockSpec((1, D_MODEL, 2 * MEM_W), lambda l: (l, 0, 0)),
            pl.BlockSpec((1, 1, MEM_W), lambda l: (l, 0, 0)),
            pl.BlockSpec((MEM_W, MEM_W), lambda l: (0, 0)),
        ],
        out_specs=[
            pl.BlockSpec((1, MEM_W, MEM_LEN), lambda l: (l, 0, 0)),
            pl.BlockSpec((1, MEM_LEN, MEM_W), lambda l: (l, 0, 0)),
        ],
        out_shape=[
            jax.ShapeDtypeStruct((depth, MEM_W, MEM_LEN), BF16),
            jax.ShapeDtypeStruct((depth, MEM_LEN, MEM_W), BF16),
        ],
        name="mem_kv",
    )(mem, g, w, kg, bd)


def _layer0_kernel(x_ref, g_ref, win_ref, bglu_ref, dw_ref, dwb_ref, lng_ref, lnb_ref, qg_ref,
                   mkT_ref, mv_ref, bd_ref, hm_ref, conv_ref, memo_ref, buf_ref, cv_ref, sh_ref):
    i = pl.program_id(0)
    tm = x_ref.shape[0]
    x = x_ref[...]
    h = (x * _rms_scale(x) * g_ref[...]).astype(BF16)
    u = _dot(h, win_ref[...])
    a = u[:, :CONV_W] + bglu_ref[:, :CONV_W]
    gate = u[:, CONV_W:2 * CONV_W] + bglu_ref[:, CONV_W:]
    v = a * jax.nn.sigmoid(gate)

    @pl.when(i == 0)
    def _():
        buf_ref[0:CONV_HALO, :] = jnp.zeros((CONV_HALO, CONV_W), F32)

    buf_ref[CONV_HALO:CONV_HALO + tm, :] = v

    base = CONV_HALO - (CONV_K - 1)
    rows = tm + CONV_HALO
    for cb in range(CONV_W // 128):
        cs = slice(cb * 128, (cb + 1) * 128)
        xb = buf_ref[:, cs]
        sh_ref[0] = xb
        for b in range(1, 8):
            sh_ref[b] = pltpu.roll(xb, rows - b, axis=0)

        def conv_rows(r, carry, cs=cs):
            r0 = pl.multiple_of(r * CONV_ROWS, CONV_ROWS)
            acc = jnp.zeros((CONV_ROWS, 128), F32) + dwb_ref[:, cs]
            for k in range(CONV_K):
                a, b = divmod(base + k, 8)
                acc = acc + dw_ref[k:k + 1, cs] * sh_ref[b, pl.ds(r0 + 8 * a, CONV_ROWS), :]
            cv_ref[pl.ds(r0, CONV_ROWS), cs] = acc
            return carry

        lax.fori_loop(0, tm // CONV_ROWS, conv_rows, 0)
    buf_ref[0:CONV_HALO, :] = buf_ref[tm:tm + CONV_HALO, :]

    cv = cv_ref[...]
    mu = jnp.mean(cv, axis=-1, keepdims=True)
    d = cv - mu
    var = jnp.mean(d * d, axis=-1, keepdims=True)
    y = d * lax.rsqrt(var + LN_EPS) * lng_ref[...] + lnb_ref[...]
    conv_ref[...] = (y * jax.nn.sigmoid(y)).astype(BF16)

    qm = u[:, 2 * CONV_W:]
    qn = qm * lax.rsqrt(_head_meansq(qm, bd_ref[...]) + EPS) * qg_ref[...]
    memo_ref[...] = _mem_attention(qn, mkT_ref[0], mv_ref[0], hm_ref).astype(BF16)


def _layer0(x, g, w_in, bglu, dw, dwb, lng, lnb, qg, mkT, mv, bd, hm):
    s = x.shape[0]
    tm = ROW_TILE
    const = lambda i: (0, 0)
    return pl.pallas_call(
        _layer0_kernel,
        grid=(s // tm,),
        in_specs=[
            pl.BlockSpec((tm, D_MODEL), lambda i: (i, 0)),
            pl.BlockSpec((1, D_MODEL), const),
            pl.BlockSpec(w_in.shape, const),
            pl.BlockSpec((1, 2 * CONV_W), const),
            pl.BlockSpec((CONV_K, CONV_W), const),
            pl.BlockSpec((1, CONV_W), const),
            pl.BlockSpec((1, CONV_W), const),
            pl.BlockSpec((1, CONV_W), const),
            pl.BlockSpec((1, MEM_W), const),
            pl.BlockSpec((1, MEM_W, MEM_LEN), lambda i: (0, 0, 0)),
            pl.BlockSpec((1, MEM_LEN, MEM_W), lambda i: (0, 0, 0)),
            pl.BlockSpec((MEM_W, MEM_W), const),
            pl.BlockSpec((MEM_HEADS, MEM_W), const),
        ],
        out_specs=[
            pl.BlockSpec((tm, CONV_W), lambda i: (i, 0)),
            pl.BlockSpec((tm, MEM_W), lambda i: (i, 0)),
        ],
        out_shape=[
            jax.ShapeDtypeStruct((s, CONV_W), BF16),
            jax.ShapeDtypeStruct((s, MEM_W), BF16),
        ],
        scratch_shapes=[
            pltpu.VMEM((tm + CONV_HALO, CONV_W), F32),
            pltpu.VMEM((tm, CONV_W), F32),
            pltpu.VMEM((8, tm + CONV_HALO, 128), F32),
        ],
        compiler_params=pltpu.CompilerParams(
            dimension_semantics=("arbitrary",), vmem_limit_bytes=VMEM_LIMIT),
        name="layer0_mixer",
    )(x, g, w_in, bglu, dw, dwb, lng, lnb, qg, mkT, mv, bd, hm)


def _post_kernel(x_ref, ma_ref, mb_ref, wa_ref, wb_ref, g_ref, win_ref, wout_ref, o_ref):
    o_ref[...] = x_ref[...] + _dot(ma_ref[...], wa_ref[...]) + _dot(mb_ref[...], wb_ref[...])
    x1 = o_ref[...]
    h = (x1 * _rms_scale(x1) * g_ref[...]).astype(BF16)
    for c in range(D_FF // FF_CHUNK):
        cs = slice(c * FF_CHUNK, (c + 1) * FF_CHUNK)
        t = jnp.maximum(_dot(h, win_ref[:, cs]), 0.0)
        o_ref[...] += _dot((t * t).astype(BF16), wout_ref[cs, :])


def _post(x, ma, mb, wa, wb, g, w_in, w_out):
    s = x.shape[0]
    tm = ROW_TILE
    const = lambda i: (0, 0)
    single = pl.Buffered(1)
    return pl.pallas_call(
        _post_kernel,
        grid=(s // tm,),
        in_specs=[
            pl.BlockSpec((tm, D_MODEL), lambda i: (i, 0)),
            pl.BlockSpec((tm, CONV_W), lambda i: (i, 0)),
            pl.BlockSpec((tm, MEM_W), lambda i: (i, 0)),
            pl.BlockSpec((CONV_W, D_MODEL), const, pipeline_mode=single),
            pl.BlockSpec((MEM_W, D_MODEL), const, pipeline_mode=single),
            pl.BlockSpec((1, D_MODEL), const),
            pl.BlockSpec((D_MODEL, D_FF), const, pipeline_mode=single),
            pl.BlockSpec((D_FF, D_MODEL), const, pipeline_mode=single),
        ],
        out_specs=pl.BlockSpec((tm, D_MODEL), lambda i: (i, 0)),
        out_shape=jax.ShapeDtypeStruct((s, D_MODEL), F32),
        compiler_params=pltpu.CompilerParams(
            dimension_semantics=("arbitrary",), vmem_limit_bytes=VMEM_LIMIT),
        name="outproj_mlp",
    )(x, ma, mb, wa, wb, g, w_in, w_out)


def _layer1_proj_kernel(x_ref, gkv_ref, gmix_ref, wkv_ref, win_ref, gb_ref, qg_ref, mqg_ref, kng_ref,
                        mkT_ref, mv_ref, bd_ref, hm_ref, pat_ref,
                        qT_ref, gT_ref, ksel_ref, kwin_ref, vselT_ref, vwinT_ref, kvc_ref, memo_ref):
    tm = x_ref.shape[0]
    x = x_ref[...]
    xn = x * _rms_scale(x)
    bd = bd_ref[...]

    kv = _dot((xn * gkv_ref[...]).astype(BF16), wkv_ref[...])
    kvc_ref[...] = kv[:, 0:256].astype(BF16)
    k2 = kv[:, 256:384]
    v2 = kv[:, 384:512]
    kw = kv[:, 512:640]
    vw = kv[:, 640:768]
    bd2 = bd[0:128, 0:128]
    k2n = k2 * lax.rsqrt(_head_meansq(k2, bd2) + EPS) * kng_ref[0:1, :]
    kwn = kw * lax.rsqrt(_head_meansq(kw, bd2) + EPS) * kng_ref[1:2, :]

    lane = lax.broadcasted_iota(jnp.int32, (tm, 128), 1)
    pat = pat_ref[...]
    ksel_ref[0] = jnp.where(lane < HEAD_DIM, k2n, pat).astype(BF16)
    ksel_ref[1] = jnp.where(lane < HEAD_DIM, pltpu.roll(k2n, HEAD_DIM, axis=1), pat).astype(BF16)
    kwin_ref[0] = kwn[:, :HEAD_DIM].astype(BF16)
    kwin_ref[1] = kwn[:, HEAD_DIM:].astype(BF16)

    ones_rows = (lax.broadcasted_iota(jnp.int32, (VAUG_ROWS - HEAD_DIM, tm), 0) == 0).astype(BF16)
    v2T = v2.T.astype(BF16)
    vwT = vw.T.astype(BF16)
    for g in range(GROUPS):
        vselT_ref[g, 0:HEAD_DIM, :] = v2T[g * HEAD_DIM:(g + 1) * HEAD_DIM, :]
        vselT_ref[g, HEAD_DIM:VAUG_ROWS, :] = ones_rows
        vwinT_ref[g, 0:HEAD_DIM, :] = vwT[g * HEAD_DIM:(g + 1) * HEAD_DIM, :]
        vwinT_ref[g, HEAD_DIM:VAUG_ROWS, :] = ones_rows

    u = _dot((xn * gmix_ref[...]).astype(BF16), win_ref[...])
    for cb in range(NSA_W // 256):
        cs = slice(cb * 256, (cb + 1) * 256)
        qc = u[:, cs]
        qn = qc * lax.rsqrt(_head_meansq(qc, bd) + EPS) * qg_ref[:, cs]
        qT_ref[4 * cb:4 * cb + 4] = qn.T.astype(BF16).reshape(4, HEAD_DIM, tm)

    gates = jax.nn.sigmoid(u[:, NSA_W + MEM_W:] + gb_ref[...])
    gT_ref[...] = gates.T

    qm = u[:, NSA_W:NSA_W + MEM_W]
    qmn = qm * lax.rsqrt(_head_meansq(qm, bd) + EPS) * mqg_ref[...]
    memo_ref[...] = _mem_attention(qmn, mkT_ref[0], mv_ref[0], hm_ref).astype(BF16)


def _layer1_proj(x, gkv, gmix, wkv, w_in, gb, qg, mqg, kng, mkT, mv, bd, hm, pat):
    s = x.shape[0]
    tm = ROW_TILE
    const = lambda i: (0, 0)
    pat_blocks = pat.shape[0] // tm
    return pl.pallas_call(
        _layer1_proj_kernel,
        grid=(s // tm,),
        in_specs=[
            pl.BlockSpec((tm, D_MODEL), lambda i: (i, 0)),
            pl.BlockSpec((1, D_MODEL), const),
            pl.BlockSpec((1, D_MODEL), const),
            pl.BlockSpec(wkv.shape, const),
            pl.BlockSpec(w_in.shape, const),
            pl.BlockSpec((1, GATE_PAD), const),
            pl.BlockSpec((1, NSA_W), const),
            pl.BlockSpec((1, MEM_W), const),
            pl.BlockSpec((2, 128), const),
            pl.BlockSpec((1, MEM_W, MEM_LEN), lambda i: (1, 0, 0)),
            pl.BlockSpec((1, MEM_LEN, MEM_W), lambda i: (1, 0, 0)),
            pl.BlockSpec((MEM_W, MEM_W), const),
            pl.BlockSpec((MEM_HEADS, MEM_W), const),
            pl.BlockSpec((tm, 128), lambda i: (i % pat_blocks, 0)),
        ],
        out_specs=[
            pl.BlockSpec((NSA_HEADS, HEAD_DIM, tm), lambda i: (0, 0, i)),
            pl.BlockSpec((GATE_PAD, tm), lambda i: (0, i)),
            pl.BlockSpec((GROUPS, tm, 128), lambda i: (0, i, 0)),
            pl.BlockSpec((GROUPS, tm, HEAD_DIM), lambda i: (0, i, 0)),
            pl.BlockSpec((GROUPS, VAUG_ROWS, tm), lambda i: (0, 0, i)),
            pl.BlockSpec((GROUPS, VAUG_ROWS, tm), lambda i: (0, 0, i)),
            pl.BlockSpec((tm, 256), lambda i: (i, 0)),
            pl.BlockSpec((tm, MEM_W), lambda i: (i, 0)),
        ],
        out_shape=[
            jax.ShapeDtypeStruct((NSA_HEADS, HEAD_DIM, s), BF16),
            jax.ShapeDtypeStruct((GATE_PAD, s), F32),
            jax.ShapeDtypeStruct((GROUPS, s, 128), BF16),
            jax.ShapeDtypeStruct((GROUPS, s, HEAD_DIM), BF16),
            jax.ShapeDtypeStruct((GROUPS, VAUG_ROWS, s), BF16),
            jax.ShapeDtypeStruct((GROUPS, VAUG_ROWS, s), BF16),
            jax.ShapeDtypeStruct((s, 256), BF16),
            jax.ShapeDtypeStruct((s, MEM_W), BF16),
        ],
        compiler_params=pltpu.CompilerParams(
            dimension_semantics=("arbitrary",), vmem_limit_bytes=VMEM_LIMIT),
        name="layer1_proj",
    )(x, gkv, gmix, wkv, w_in, gb, qg, mqg, kng, mkT, mv, bd, hm, pat)


def _compress_kernel(t_ref, w1_ref, w2_ref, w2T_ref, pe_ref, kg_ref, kgT_ref, o_ref, oT_ref, hb_ref):
    nc = t_ref.shape[1]
    half = CMP_STRIDE * HEAD_DIM
    t = t_ref[0]
    ha = _dot(t, w1_ref[0, 0:half, :])
    hb_ref[0:nc, :] = _dot(t, w1_ref[0, half:2 * half, :])
    hb_ref[nc:nc + 8, :] = jnp.zeros((8, CMP_HID), F32)
    pe = jnp.broadcast_to(pe_ref[0], (8, 2 * half)).astype(BF16)
    pe_term = _dot(pe, w1_ref[0])[0:1, :]
    h = ha + hb_ref[1:nc + 1, :] + pe_term
    hg = 0.5 * h * (1.0 + jnp.tanh(0.7978845608028654 * (h + 0.044715 * (h * h * h))))
    hg = hg.astype(BF16)
    o = _dot(hg, w2_ref[0])
    oT = lax.dot_general(w2T_ref[0], hg, (((1,), (1,)), ((), ())), preferred_element_type=F32)
    is_k = pl.program_id(0) < GROUPS
    on = o * lax.rsqrt(jnp.mean(o * o, axis=1, keepdims=True) + EPS) * kg_ref[...]
    oTn = oT * lax.rsqrt(jnp.mean(oT * oT, axis=0, keepdims=True) + EPS) * kgT_ref[...]
    o_ref[0] = jnp.where(is_k, on, o).astype(BF16)
    oT_ref[0, 0:HEAD_DIM, :] = jnp.where(is_k, oTn, oT).astype(BF16)
    oT_ref[0, HEAD_DIM:VAUG_ROWS, :] = (
        lax.broadcasted_iota(jnp.int32, (VAUG_ROWS - HEAD_DIM, nc), 0) == 0).astype(BF16)


def _compress(t, w1, w2, w2T, pe, kg, kgT):
    n, nc, width = t.shape
    return pl.pallas_call(
        _compress_kernel,
        grid=(n,),
        in_specs=[
            pl.BlockSpec((1, nc, width), lambda i: (i, 0, 0)),
            pl.BlockSpec((1, 2 * width, CMP_HID), lambda i: (i // GROUPS, 0, 0)),
            pl.BlockSpec((1, CMP_HID, HEAD_DIM), lambda i: (i // GROUPS, 0, 0)),
            pl.BlockSpec((1, HEAD_DIM, CMP_HID), lambda i: (i // GROUPS, 0, 0)),
            pl.BlockSpec((1, 1, 2 * width), lambda i: (i // GROUPS, 0, 0)),
            pl.BlockSpec((1, HEAD_DIM), lambda i: (0, 0)),
            pl.BlockSpec((HEAD_DIM, 1), lambda i: (0, 0)),
        ],
        out_specs=[
            pl.BlockSpec((1, nc, HEAD_DIM), lambda i: (i, 0, 0)),
            pl.BlockSpec((1, VAUG_ROWS, nc), lambda i: (i, 0, 0)),
        ],
        out_shape=[
            jax.ShapeDtypeStruct((n, nc, HEAD_DIM), BF16),
            jax.ShapeDtypeStruct((n, VAUG_ROWS, nc), BF16),
        ],
        scratch_shapes=[pltpu.VMEM((nc + 8, CMP_HID), F32)],
        compiler_params=pltpu.CompilerParams(
            dimension_semantics=("arbitrary",), vmem_limit_bytes=VMEM_LIMIT),
        name="compress_kv",
    )(t, w1, w2, w2T, pe, kg, kgT)


def _nsa_kernel(qT_ref, gT_ref, kc_ref, vcT_ref, ksel_ref, vselT_ref, kwin_ref, vwinT_ref, out_ref,
                qaug_ref, psum_ref, bias_ref, acc_ref, m_ref, s_ref, oc_ref, sw_ref):
    c = pl.program_id(1)
    nc = kc_ref.shape[1]
    nsb = bias_ref.shape[0]
    hg = HEADS_PER_GROUP
    width = hg * Q_BLOCK

    for hh in range(hg):
        qaug_ref[0:HEAD_DIM, hh * Q_BLOCK:(hh + 1) * Q_BLOCK] = qT_ref[hh]
    qaug_ref[HEAD_DIM:128, :] = jnp.zeros((128 - HEAD_DIM, width), BF16)
    qT = qaug_ref[0:HEAD_DIM, :]
    t_q = c * Q_BLOCK + (lax.broadcasted_iota(jnp.int32, (1, width), 1) & (Q_BLOCK - 1))

    def cmp_branch(rows):
        s = _dot_3tiles(kc_ref[0, 0:rows, :], qT)
        ci = lax.broadcasted_iota(jnp.int32, (rows, 1), 0)
        s = jnp.where(ci * CMP_STRIDE + (CMP_L - 1) <= t_q, s, -jnp.inf)
        m = jnp.max(s, axis=0, keepdims=True)
        m = jnp.where(m == -jnp.inf, 0.0, m)
        e = jnp.exp2(s - m)
        oa = _dot(vcT_ref[0, :, 0:rows], e.astype(BF16))
        rl = 1.0 / jnp.maximum(oa[HEAD_DIM:HEAD_DIM + 1, :], 1e-30)
        oc_ref[...] = oa[0:HEAD_DIM, :] * rl
        p = e * rl
        psum = p[:, 0:Q_BLOCK]
        for hh in range(1, hg):
            psum = psum + p[:, hh * Q_BLOCK:(hh + 1) * Q_BLOCK]
        psum_ref[8:8 + rows, :] = psum

    @pl.when(c == 0)
    def _():
        psum_ref[...] = jnp.zeros(psum_ref.shape, F32)

    cuts = [nc * k // CMP_PATHS for k in range(1, CMP_PATHS + 1)] if nc % (128 * CMP_PATHS) == 0 else [nc]
    need = (c + 1) * (Q_BLOCK // CMP_STRIDE)
    for k, rows in enumerate(cuts):
        lo = cuts[k - 1] if k else 0
        pl.when((need > lo) & (need <= rows))(functools.partial(cmp_branch, rows))
    o_cmp = oc_ref[...]

    row0 = pl.multiple_of(jnp.maximum(c * Q_BLOCK - WIN, 0), Q_BLOCK)
    sw = _dot_3tiles(kwin_ref[0, pl.ds(row0, WIN_KEYS), :], qT)
    kp = row0 + lax.broadcasted_iota(jnp.int32, (WIN_KEYS, 1), 0)
    sw = jnp.where((kp <= t_q) & (kp > t_q - WIN), sw, -jnp.inf)
    sw_ref[...] = sw
    mw = jnp.max(jnp.max(sw.reshape(WIN_KEYS // 8, 8, width), axis=0), axis=0, keepdims=True)
    mw = jnp.where(mw == -jnp.inf, 0.0, mw)

    def select_blocks(rows):
        imp = (psum_ref[pl.ds(7, rows, stride=4), :] + psum_ref[pl.ds(11, rows, stride=4), :]
               + 2.0 * (psum_ref[pl.ds(8, rows, stride=4), :] + psum_ref[pl.ds(9, rows, stride=4), :]
                        + psum_ref[pl.ds(10, rows, stride=4), :]))
        blk = lax.broadcasted_iota(jnp.int32, (rows, Q_BLOCK), 0)
        tb = (c * Q_BLOCK + lax.broadcasted_iota(jnp.int32, (1, Q_BLOCK), 1)) >> 6
        valid = blk <= tb
        forced = (blk == 0) | (blk == tb) | (blk == tb - 1)
        score = jnp.where(forced | ~valid, -jnp.inf, imp)
        blk_f = blk.astype(F32)

        def pick(_, sc):
            mx = jnp.max(sc, axis=0, keepdims=True)
            first = jnp.min(jnp.where(sc == mx, blk_f, float(rows)), axis=0, keepdims=True)
            return jnp.where(blk_f == first, -jnp.inf, sc)

        left = lax.fori_loop(0, N_SEL - 3, pick, score)
        sel = (left == -jnp.inf) & valid
        bias_ref[0:rows, :] = jnp.where(sel, 0.0, MASK_BIAS).astype(BF16)

    @pl.when(c == 0)
    def _():
        bias_ref[...] = jnp.full(bias_ref.shape, MASK_BIAS, BF16)

    b_cuts = [nsb * k // SEL_PATHS for k in range(1, SEL_PATHS + 1)] if nsb % (16 * SEL_PATHS) == 0 else [nsb]
    need_b = 2 * c + 2
    for k, rows in enumerate(b_cuts):
        lo = b_cuts[k - 1] if k else 0
        pl.when((need_b > lo) & (need_b <= rows))(functools.partial(select_blocks, rows))

    acc_ref[...] = jnp.zeros(acc_ref.shape, F32)
    m_ref[...] = jnp.full(m_ref.shape, M_INIT, F32)

    n_chunks = KEY_TILE // KEY_CHUNK

    def set_bias(j):
        span = j // (ONEHOT_SPAN // KEY_TILE)
        b16 = bias_ref[pl.ds(pl.multiple_of(span * SEL_PER_SPAN, SEL_PER_SPAN), SEL_PER_SPAN), :]
        for hh in range(hg):
            qaug_ref[HEAD_DIM:HEAD_DIM + SEL_PER_SPAN, hh * Q_BLOCK:(hh + 1) * Q_BLOCK] = b16

    def scores(j, r):
        k0 = pl.multiple_of(j * KEY_TILE + r * KEY_CHUNK, KEY_CHUNK)
        return _dot(ksel_ref[0, pl.ds(k0, KEY_CHUNK), :], qaug_ref[...])

    def chunk_max(cmax, sc):
        return jnp.maximum(cmax, jnp.max(sc.reshape(KEY_CHUNK // 8, 8, width), axis=0))

    def new_max(cmax):
        m_old = m_ref[...]
        m_new = jnp.maximum(m_old, jnp.max(cmax, axis=0, keepdims=True))
        m_ref[...] = m_new
        return m_new, jnp.exp2(m_old - m_new)

    def values(j, r):
        k0 = pl.multiple_of(j * KEY_TILE + r * KEY_CHUNK, KEY_CHUNK)
        return vselT_ref[0, :, pl.ds(k0, KEY_CHUNK)]

    cmax0 = jnp.full((8, width), MASK_BIAS, F32)
    set_bias(0)
    cmax = cmax0
    ow = jnp.zeros((VAUG_ROWS, width), F32)
    w_cuts = [WIN_KEYS * r // n_chunks // 128 * 128 for r in range(n_chunks)] + [WIN_KEYS]
    for r in range(n_chunks):
        sc = _dot_3tiles(ksel_ref[0, r * KEY_CHUNK:(r + 1) * KEY_CHUNK, :], qaug_ref[...])
        ws = slice(w_cuts[r], w_cuts[r + 1])
        pw = jnp.exp2(sw_ref[ws, :] - mw).astype(BF16)
        s_ref[r * KEY_CHUNK:(r + 1) * KEY_CHUNK, :] = sc
        cmax = chunk_max(cmax, sc)
        ow = ow + _dot(vwinT_ref[0, :, pl.ds(pl.multiple_of(row0 + w_cuts[r], 128), w_cuts[r + 1] - w_cuts[r])], pw)
    o_win = ow[0:HEAD_DIM, :] / jnp.maximum(ow[HEAD_DIM:HEAD_DIM + 1, :], 1e-30)

    last = c // (KEY_TILE // Q_BLOCK)

    def pipe_step(j, cmax):
        m_new, alpha = new_max(cmax)
        set_bias(j + 1)
        acc = acc_ref[...] * alpha
        cnext = cmax0
        for r in range(n_chunks):
            rs = slice(r * KEY_CHUNK, (r + 1) * KEY_CHUNK)
            sc = scores(j + 1, r)
            pj = jnp.exp2(s_ref[rs, :] - m_new).astype(BF16)
            s_ref[rs, :] = sc
            cnext = chunk_max(cnext, sc)
            acc = acc + _dot(values(j, r), pj)
        acc_ref[...] = acc
        return cnext

    lax.fori_loop(0, last, pipe_step, cmax)

    diag = pl.ds(pl.multiple_of((c % (KEY_TILE // Q_BLOCK)) * Q_BLOCK, Q_BLOCK), Q_BLOCK)
    ku = lax.broadcasted_iota(jnp.int32, (Q_BLOCK, 1), 0)
    s_ref[diag, :] = jnp.where(ku <= t_q - c * Q_BLOCK, s_ref[diag, :], MASK_BIAS)

    def finish(rows):
        s_last = s_ref[0:rows, :]
        m_new, alpha = new_max(jnp.max(s_last.reshape(rows // 8, 8, width), axis=0))
        p_last = jnp.exp2(s_last - m_new).astype(BF16)
        acc_ref[...] = acc_ref[...] * alpha + _dot(
            vselT_ref[0, :, pl.ds(pl.multiple_of(last * KEY_TILE, KEY_TILE), rows)], p_last)

    diag_chunk = (c % (KEY_TILE // Q_BLOCK)) // (KEY_CHUNK // Q_BLOCK)
    for k in range(n_chunks):
        pl.when(diag_chunk == k)(functools.partial(finish, (k + 1) * KEY_CHUNK))
    o_slc = acc_ref[0:HEAD_DIM, :] / jnp.maximum(acc_ref[HEAD_DIM:HEAD_DIM + 1, :], 1e-30)

    heads = []
    for hh in range(hg):
        hs = slice(hh * Q_BLOCK, (hh + 1) * Q_BLOCK)
        heads.append(o_cmp[:, hs] * gT_ref[hh:hh + 1, :] + o_slc[:, hs] * gT_ref[8 + hh:9 + hh, :]
                     + o_win[:, hs] * gT_ref[16 + hh:17 + hh, :])
    for pr in range(hg // 2):
        pair = jnp.concatenate([heads[2 * pr], heads[2 * pr + 1]], axis=0)
        out_ref[:, pr * 128:(pr + 1) * 128] = pair.T.astype(BF16)


def _nsa(qT, gT, kc, vcT, ksel, vselT, kwin, vwinT):
    s = ksel.shape[1]
    nc = kc.shape[1]
    nsb = s // SEL_L
    hg = HEADS_PER_GROUP
    grp = lambda g, c: (g, 0, 0)
    return pl.pallas_call(
        _nsa_kernel,
        grid=(GROUPS, s // Q_BLOCK),
        in_specs=[
            pl.BlockSpec((hg, HEAD_DIM, Q_BLOCK), lambda g, c: (g, 0, c)),
            pl.BlockSpec((GATE_ROWS, Q_BLOCK), lambda g, c: (g, c)),
            pl.BlockSpec((1, nc, HEAD_DIM), grp),
            pl.BlockSpec((1, VAUG_ROWS, nc), lambda g, c: (GROUPS + g, 0, 0)),
            pl.BlockSpec((1, s, 128), grp),
            pl.BlockSpec((1, VAUG_ROWS, s), grp),
            pl.BlockSpec((1, s, HEAD_DIM), grp),
            pl.BlockSpec((1, VAUG_ROWS, s), grp),
        ],
        out_specs=pl.BlockSpec((Q_BLOCK, hg * HEAD_DIM), lambda g, c: (c, g)),
        out_shape=jax.ShapeDtypeStruct((s, NSA_W), BF16),
        scratch_shapes=[
            pltpu.VMEM((128, hg * Q_BLOCK), BF16),
            pltpu.VMEM((nc + 16, Q_BLOCK), F32),
            pltpu.VMEM((nsb, Q_BLOCK), BF16),
            pltpu.VMEM((VAUG_ROWS, hg * Q_BLOCK), F32),
            pltpu.VMEM((1, hg * Q_BLOCK), F32),
            pltpu.VMEM((KEY_TILE, hg * Q_BLOCK), F32),
            pltpu.VMEM((HEAD_DIM, hg * Q_BLOCK), F32),
            pltpu.VMEM((WIN_KEYS, hg * Q_BLOCK), F32),
        ],
        compiler_params=pltpu.CompilerParams(
            dimension_semantics=("arbitrary", "arbitrary"), vmem_limit_bytes=VMEM_LIMIT),
        name="nsa_attention",
    )(qT, gT, kc, vcT, ksel, vselT, kwin, vwinT)


def _block_diag_ones():
    idx = np.arange(MEM_W) // HEAD_DIM
    return jnp.asarray((idx[:, None] == idx[None, :]).astype(np.float32), BF16)


def _head_masks():
    idx = np.arange(MEM_W) // HEAD_DIM
    return jnp.asarray((idx[None, :] == np.arange(MEM_HEADS)[:, None]).astype(np.float32))


def _onehot_pattern():
    rows = np.arange(ONEHOT_SPAN)[:, None] // SEL_L
    lanes = np.arange(128)[None, :] - HEAD_DIM
    return jnp.asarray((rows == lanes).astype(np.float32))


def _gate_layout():
    src = np.full((GATE_PAD,), -1, np.int64)
    for g in range(GROUPS):
        for b in range(3):
            for hh in range(HEADS_PER_GROUP):
                src[g * GATE_ROWS + b * 8 + hh] = (g * HEADS_PER_GROUP + hh) * 3 + b
    return src


def kernel(x, mem, norm_mix_g, norm_mlp_g, mem_norm_g, w_mem_kv, mem_q_norm_g, mem_k_norm_g, w_out, w_mlp_in, w_mlp_out, a_w_in, a_b_glu, a_dw, a_dw_b, a_ln_g, a_ln_b, b_w_in, b_gate_b, b_q_norm_g, kv_norm_g, w_kv, k_norm_g, cmp_pe_k, cmp_pe_v, cmp_w1_k, cmp_w2_k, cmp_w1_v, cmp_w2_v):
    batch, s, _ = x.shape
    assert batch == 1 and s % ONEHOT_SPAN == 0 and s >= WIN_KEYS
    assert w_out.shape[0] == 2 and a_w_in.shape[0] == 1 and b_w_in.shape[0] == 1
    nc = s // CMP_STRIDE
    row = lambda v: v.reshape(1, -1)
    bd = _block_diag_ones()
    hm = _head_masks()

    mkT, mv = _memkv(mem[0], row(mem_norm_g), w_mem_kv.astype(BF16),
                     jnp.tile(mem_k_norm_g, (1, MEM_HEADS))[:, None, :], bd)

    conv0, memo0 = _layer0(
        x[0], row(norm_mix_g[0]), a_w_in[0].astype(BF16), row(a_b_glu[0]), a_dw[0], row(a_dw_b[0]),
        row(a_ln_g[0]), row(a_ln_b[0]), row(jnp.tile(mem_q_norm_g[0], MEM_HEADS)), mkT, mv, bd, hm)
    w_out_bf = w_out.astype(BF16)
    w_in_bf = w_mlp_in.astype(BF16)
    w_o_bf = w_mlp_out.astype(BF16)
    x1 = _post(x[0], conv0, memo0, w_out_bf[0, :CONV_W], w_out_bf[0, CONV_W:], row(norm_mlp_g[0]),
               w_in_bf[0], w_o_bf[0])

    src = _gate_layout()
    used = src >= 0
    w_gate = jnp.where(used[None, :], b_w_in[0][:, NSA_W + MEM_W + np.maximum(src, 0)], 0.0)
    b_gate = jnp.where(used, b_gate_b[0][np.maximum(src, 0)], 0.0)
    w_in1 = jnp.concatenate([b_w_in[0][:, :NSA_W + MEM_W], w_gate], axis=1).astype(BF16)
    q_gain = jnp.tile(b_q_norm_g[0], NSA_HEADS) * (QK_SCALE * LOG2E)
    kng = jnp.stack([jnp.tile(k_norm_g[1], GROUPS), jnp.tile(k_norm_g[2], GROUPS)])
    qT, gT, ksel, kwin, vselT, vwinT, kvc, memo1 = _layer1_proj(
        x1, row(kv_norm_g), row(norm_mix_g[1]), w_kv.astype(BF16), w_in1, row(b_gate), row(q_gain),
        row(jnp.tile(mem_q_norm_g[1], MEM_HEADS)), kng, mkT, mv, bd, hm, _onehot_pattern())

    t = kvc.reshape(s, 2 * GROUPS, HEAD_DIM).transpose(1, 0, 2).reshape(2 * GROUPS, nc, CMP_STRIDE * HEAD_DIM)
    w1 = jnp.stack([cmp_w1_k, cmp_w1_v]).astype(BF16)
    w2 = jnp.stack([cmp_w2_k, cmp_w2_v]).astype(BF16)
    pe = jnp.stack([cmp_pe_k.reshape(1, -1), cmp_pe_v.reshape(1, -1)])
    cmp_rows, cmp_cols = _compress(t, w1, w2, w2.transpose(0, 2, 1), pe, row(k_norm_g[0]),
                                   k_norm_g[0].reshape(-1, 1))

    nsa = _nsa(qT, gT, cmp_rows, cmp_cols, ksel, vselT, kwin, vwinT)
    x2 = _post(x1, nsa, memo1, w_out_bf[1, :NSA_W], w_out_bf[1, NSA_W:], row(norm_mlp_g[1]),
               w_in_bf[1], w_o_bf[1])
    return x2[None]
```

```python
import functools

import numpy as np
import jax
import jax.numpy as jnp
from jax import lax
from jax.experimental import pallas as pl
from jax.experimental.pallas import tpu as pltpu

F32 = jnp.float32
BF16 = jnp.bfloat16

D_MODEL = 1024
HEAD_DIM = 64
MEM_LEN = 256
MEM_HEADS = 4
MEM_W = MEM_HEADS * HEAD_DIM
CONV_W = D_MODEL - MEM_W
CONV_K = 31
NSA_HEADS = CONV_W // HEAD_DIM
NSA_W = NSA_HEADS * HEAD_DIM
GROUPS = 2
HEADS_PER_GROUP = NSA_HEADS // GROUPS
CMP_L = 32
CMP_STRIDE = 16
CMP_HID = 256
SEL_L = 64
N_SEL = 16
WIN = 512
Q_BLOCK = 128
D_FF = 4 * D_MODEL
GATE_PAD = 128
GATE_ROWS = GATE_PAD // GROUPS

ROW_TILE = 512
FF_CHUNK = 1024
KEY_TILE = 1024
ONEHOT_SPAN = 1024
CMP_PATHS = 4
SEL_PATHS = 4
N_STREAMS = 2
KEY_CHUNK = 256
SEL_PER_SPAN = ONEHOT_SPAN // SEL_L
WIN_KEYS = WIN + Q_BLOCK
CONV_HALO = 32
CONV_ROWS = 64
VAUG_ROWS = 80
VMEM_LIMIT = 56 * 1024 * 1024
MXU_WIDTH = 256

EPS = 1e-6
LN_EPS = 1e-5
LOG2E = 1.4426950408889634
QK_SCALE = HEAD_DIM ** -0.5
MASK_BIAS = -2.0 ** 126
M_INIT = -2.0 ** 100


def _dot(a, b):
    return jnp.dot(a, b, preferred_element_type=F32)


def _dot_3tiles(a, b):
    two = 2 * MXU_WIDTH
    half = a.shape[0] // 2
    return jnp.concatenate([
        _dot(a, b[:, 0:two]),
        jnp.concatenate([_dot(a[0:half], b[:, two:]), _dot(a[half:], b[:, two:])], axis=0),
    ], axis=1)


def _rms_scale(x):
    return lax.rsqrt(jnp.mean(x * x, axis=-1, keepdims=True) + EPS)


def _head_meansq(x, bd):
    x2 = x * x
    hi = x2.astype(BF16)
    lo = (x2 - hi.astype(F32)).astype(BF16)
    return (_dot(hi, bd) + _dot(lo, bd)) * (1.0 / HEAD_DIM)


def _mem_attention(qn, mkT, mv, hm_ref):
    out = jnp.zeros(qn.shape, F32)
    for h in range(MEM_HEADS):
        hm = hm_ref[h:h + 1, :]
        s = _dot((qn * hm).astype(BF16), mkT)
        e = jnp.exp2(s - jnp.max(s, axis=-1, keepdims=True))
        l = jnp.sum(e, axis=-1, keepdims=True)
        out = out + _dot(e.astype(BF16), mv) * (hm / l)
    return out


def _memkv_kernel(mem_ref, g_ref, w_ref, kg_ref, bd_ref, mkT_ref, mv_ref):
    m = mem_ref[...]
    mn = (m * _rms_scale(m) * g_ref[...]).astype(BF16)
    kv = _dot(mn, w_ref[0])
    k = kv[:, :MEM_W]
    kn = k * lax.rsqrt(_head_meansq(k, bd_ref[...]) + EPS) * kg_ref[0]
    mkT_ref[0] = (kn * (QK_SCALE * LOG2E)).T.astype(BF16)
    mv_ref[0] = kv[:, MEM_W:].astype(BF16)


def _memkv(mem, g, w, kg, bd):
    depth = w.shape[0]
    return pl.pallas_call(
        _memkv_kernel,
        grid=(depth,),
        in_specs=[
            pl.BlockSpec((MEM_LEN, D_MODEL), lambda l: (0, 0)),
            pl.BlockSpec((1, D_MODEL), lambda l: (0, 0)),
            pl.BlockSpec((1, D_MODEL, 2 * MEM_W), lambda l: (l, 0, 0)),
            pl.BlockSpec((1, 1, MEM_W), lambda l: (l, 0, 0)),
            pl.BlockSpec((MEM_W, MEM_W), lambda l: (0, 0)),
        ],
        out_specs=[
            pl.BlockSpec((1, MEM_W, MEM_LEN), lambda l: (l, 0, 0)),
            pl.BlockSpec((1, MEM_LEN, MEM_W), lambda l: (l, 0, 0)),
        ],
        out_shape=[
            jax.ShapeDtypeStruct((depth, MEM_W, MEM_LEN), BF16),
            jax.ShapeDtypeStruct((depth, MEM_LEN, MEM_W), BF16),
        ],
        name="mem_kv",
    )(mem, g, w, kg, bd)


def _layer0_kernel(x_ref, g_ref, win_ref, bglu_ref, dw_ref, dwb_ref, lng_ref, lnb_ref, qg_ref,
                   mkT_ref, mv_ref, bd_ref, hm_ref, conv_ref, memo_ref, buf_ref, cv_ref, sh_ref):
    i = pl.program_id(0)
    tm = x_ref.shape[0]
    x = x_ref[...]
    h = (x * _rms_scale(x) * g_ref[...]).astype(BF16)
    u = _dot(h, win_ref[...])
    a = u[:, :CONV_W] + bglu_ref[:, :CONV_W]
    gate = u[:, CONV_W:2 * CONV_W] + bglu_ref[:, CONV_W:]
    v = a * jax.nn.sigmoid(gate)

    @pl.when(i == 0)
    def _():
        buf_ref[0:CONV_HALO, :] = jnp.zeros((CONV_HALO, CONV_W), F32)

    buf_ref[CONV_HALO:CONV_HALO + tm, :] = v

    base = CONV_HALO - (CONV_K - 1)
    rows = tm + CONV_HALO
    for cb in range(CONV_W // 128):
        cs = slice(cb * 128, (cb + 1) * 128)
        xb = buf_ref[:, cs]
        sh_ref[0] = xb
        for b in range(1, 8):
            sh_ref[b] = pltpu.roll(xb, rows - b, axis=0)

        def conv_rows(r, carry, cs=cs):
            r0 = pl.multiple_of(r * CONV_ROWS, CONV_ROWS)
            acc = jnp.zeros((CONV_ROWS, 128), F32) + dwb_ref[:, cs]
            for k in range(CONV_K):
                a, b = divmod(base + k, 8)
                acc = acc + dw_ref[k:k + 1, cs] * sh_ref[b, pl.ds(r0 + 8 * a, CONV_ROWS), :]
            cv_ref[pl.ds(r0, CONV_ROWS), cs] = acc
            return carry

        lax.fori_loop(0, tm // CONV_ROWS, conv_rows, 0)
    buf_ref[0:CONV_HALO, :] = buf_ref[tm:tm + CONV_HALO, :]

    cv = cv_ref[...]
    mu = jnp.mean(cv, axis=-1, keepdims=True)
    d = cv - mu
    var = jnp.mean(d * d, axis=-1, keepdims=True)
    y = d * lax.rsqrt(var + LN_EPS) * lng_ref[...] + lnb_ref[...]
    conv_ref[...] = (y * jax.nn.sigmoid(y)).astype(BF16)

    qm = u[:, 2 * CONV_W:]
    qn = qm * lax.rsqrt(_head_meansq(qm, bd_ref[...]) + EPS) * qg_ref[...]
    memo_ref[...] = _mem_attention(qn, mkT_ref[0], mv_ref[0], hm_ref).astype(BF16)


def _layer0(x, g, w_in, bglu, dw, dwb, lng, lnb, qg, mkT, mv, bd, hm):
    s = x.shape[0]
    tm = ROW_TILE
    const = lambda i: (0, 0)
    return pl.pallas_call(
        _layer0_kernel,
        grid=(s // tm,),
        in_specs=[
            pl.BlockSpec((tm, D_MODEL), lambda i: (i, 0)),
            pl.BlockSpec((1, D_MODEL), const),
            pl.BlockSpec(w_in.shape, const),
            pl.BlockSpec((1, 2 * CONV_W), const),
            pl.BlockSpec((CONV_K, CONV_W), const),
            pl.BlockSpec((1, CONV_W), const),
            pl.BlockSpec((1, CONV_W), const),
            pl.BlockSpec((1, CONV_W), const),
            pl.BlockSpec((1, MEM_W), const),
            pl.BlockSpec((1, MEM_W, MEM_LEN), lambda i: (0, 0, 0)),
            pl.BlockSpec((1, MEM_LEN, MEM_W), lambda i: (0, 0, 0)),
            pl.BlockSpec((MEM_W, MEM_W), const),
            pl.BlockSpec((MEM_HEADS, MEM_W), const),
        ],
        out_specs=[
            pl.BlockSpec((tm, CONV_W), lambda i: (i, 0)),
            pl.BlockSpec((tm, MEM_W), lambda i: (i, 0)),
        ],
        out_shape=[
            jax.ShapeDtypeStruct((s, CONV_W), BF16),
            jax.ShapeDtypeStruct((s, MEM_W), BF16),
        ],
        scratch_shapes=[
            pltpu.VMEM((tm + CONV_HALO, CONV_W), F32),
            pltpu.VMEM((tm, CONV_W), F32),
            pltpu.VMEM((8, tm + CONV_HALO, 128), F32),
        ],
        compiler_params=pltpu.CompilerParams(
            dimension_semantics=("arbitrary",), vmem_limit_bytes=VMEM_LIMIT),
        name="layer0_mixer",
    )(x, g, w_in, bglu, dw, dwb, lng, lnb, qg, mkT, mv, bd, hm)


def _post_kernel(x_ref, ma_ref, mb_ref, wa_ref, wb_ref, g_ref, win_ref, wout_ref, o_ref):
    o_ref[...] = x_ref[...] + _dot(ma_ref[...], wa_ref[...]) + _dot(mb_ref[...], wb_ref[...])
    x1 = o_ref[...]
    h = (x1 * _rms_scale(x1) * g_ref[...]).astype(BF16)
    for c in range(D_FF // FF_CHUNK):
        cs = slice(c * FF_CHUNK, (c + 1) * FF_CHUNK)
        t = jnp.maximum(_dot(h, win_ref[:, cs]), 0.0)
        o_ref[...] += _dot((t * t).astype(BF16), wout_ref[cs, :])


def _post(x, ma, mb, wa, wb, g, w_in, w_out):
    s = x.shape[0]
    tm = ROW_TILE
    const = lambda i: (0, 0)
    single = pl.Buffered(1)
    return pl.pallas_call(
        _post_kernel,
        grid=(s // tm,),
        in_specs=[
            pl.BlockSpec((tm, D_MODEL), lambda i: (i, 0)),
            pl.BlockSpec((tm, CONV_W), lambda i: (i, 0)),
            pl.BlockSpec((tm, MEM_W), lambda i: (i, 0)),
            pl.BlockSpec((CONV_W, D_MODEL), const, pipeline_mode=single),
            pl.BlockSpec((MEM_W, D_MODEL), const, pipeline_mode=single),
            pl.BlockSpec((1, D_MODEL), const),
            pl.BlockSpec((D_MODEL, D_FF), const, pipeline_mode=single),
            pl.BlockSpec((D_FF, D_MODEL), const, pipeline_mode=single),
        ],
        out_specs=pl.BlockSpec((tm, D_MODEL), lambda i: (i, 0)),
        out_shape=jax.ShapeDtypeStruct((s, D_MODEL), F32),
        compiler_params=pltpu.CompilerParams(
            dimension_semantics=("arbitrary",), vmem_limit_bytes=VMEM_LIMIT),
        name="outproj_mlp",
    )(x, ma, mb, wa, wb, g, w_in, w_out)


def _layer1_proj_kernel(x_ref, gkv_ref, gmix_ref, wkv_ref, win_ref, gb_ref, qg_ref, mqg_ref, kng_ref,
                        mkT_ref, mv_ref, bd_ref, hm_ref, pat_ref,
                        qT_ref, gT_ref, ksel_ref, kwin_ref, vselT_ref, vwinT_ref, kvc_ref, memo_ref):
    tm = x_ref.shape[0]
    x = x_ref[...]
    xn = x * _rms_scale(x)
    bd = bd_ref[...]

    kv = _dot((xn * gkv_ref[...]).astype(BF16), wkv_ref[...])
    kvc_ref[...] = kv[:, 0:256].astype(BF16)
    k2 = kv[:, 256:384]
    v2 = kv[:, 384:512]
    kw = kv[:, 512:640]
    vw = kv[:, 640:768]
    bd2 = bd[0:128, 0:128]
    k2n = k2 * lax.rsqrt(_head_meansq(k2, bd2) + EPS) * kng_ref[0:1, :]
    kwn = kw * lax.rsqrt(_head_meansq(kw, bd2) + EPS) * kng_ref[1:2, :]

    lane = lax.broadcasted_iota(jnp.int32, (tm, 128), 1)
    pat = pat_ref[...]
    ksel_ref[0] = jnp.where(lane < HEAD_DIM, k2n, pat).astype(BF16)
    ksel_ref[1] = jnp.where(lane < HEAD_DIM, pltpu.roll(k2n, HEAD_DIM, axis=1), pat).astype(BF16)
    kwin_ref[0] = kwn[:, :HEAD_DIM].astype(BF16)
    kwin_ref[1] = kwn[:, HEAD_DIM:].astype(BF16)

    ones_rows = (lax.broadcasted_iota(jnp.int32, (VAUG_ROWS - HEAD_DIM, tm), 0) == 0).astype(BF16)
    v2T = v2.T.astype(BF16)
    vwT = vw.T.astype(BF16)
    for g in range(GROUPS):
        vselT_ref[g, 0:HEAD_DIM, :] = v2T[g * HEAD_DIM:(g + 1) * HEAD_DIM, :]
        vselT_ref[g, HEAD_DIM:VAUG_ROWS, :] = ones_rows
        vwinT_ref[g, 0:HEAD_DIM, :] = vwT[g * HEAD_DIM:(g + 1) * HEAD_DIM, :]
        vwinT_ref[g, HEAD_DIM:VAUG_ROWS, :] = ones_rows

    u = _dot((xn * gmix_ref[...]).astype(BF16), win_ref[...])
    for cb in range(NSA_W // 256):
        cs = slice(cb * 256, (cb + 1) * 256)
        qc = u[:, cs]
        qn = qc * lax.rsqrt(_head_meansq(qc, bd) + EPS) * qg_ref[:, cs]
        qT_ref[4 * cb:4 * cb + 4] = qn.T.astype(BF16).reshape(4, HEAD_DIM, tm)

    gates = jax.nn.sigmoid(u[:, NSA_W + MEM_W:] + gb_ref[...])
    gT_ref[...] = gates.T

    qm = u[:, NSA_W:NSA_W + MEM_W]
    qmn = qm * lax.rsqrt(_head_meansq(qm, bd) + EPS) * mqg_ref[...]
    memo_ref[...] = _mem_attention(qmn, mkT_ref[0], mv_ref[0], hm_ref).astype(BF16)


def _layer1_proj(x, gkv, gmix, wkv, w_in, gb, qg, mqg, kng, mkT, mv, bd, hm, pat):
    s = x.shape[0]
    tm = ROW_TILE
    const = lambda i: (0, 0)
    pat_blocks = pat.shape[0] // tm
    return pl.pallas_call(
        _layer1_proj_kernel,
        grid=(s // tm,),
        in_specs=[
            pl.BlockSpec((tm, D_MODEL), lambda i: (i, 0)),
            pl.BlockSpec((1, D_MODEL), const),
            pl.BlockSpec((1, D_MODEL), const),
            pl.BlockSpec(wkv.shape, const),
            pl.BlockSpec(w_in.shape, const),
            pl.BlockSpec((1, GATE_PAD), const),
            pl.BlockSpec((1, NSA_W), const),
            pl.BlockSpec((1, MEM_W), const),
            pl.BlockSpec((2, 128), const),
            pl.BlockSpec((1, MEM_W, MEM_LEN), lambda i: (1, 0, 0)),
            pl.BlockSpec((1, MEM_LEN, MEM_W), lambda i: (1, 0, 0)),
            pl.BlockSpec((MEM_W, MEM_W), const),
            pl.BlockSpec((MEM_HEADS, MEM_W), const),
            pl.BlockSpec((tm, 128), lambda i: (i % pat_blocks, 0)),
        ],
        out_specs=[
            pl.BlockSpec((NSA_HEADS, HEAD_DIM, tm), lambda i: (0, 0, i)),
            pl.BlockSpec((GATE_PAD, tm), lambda i: (0, i)),
            pl.BlockSpec((GROUPS, tm, 128), lambda i: (0, i, 0)),
            pl.BlockSpec((GROUPS, tm, HEAD_DIM), lambda i: (0, i, 0)),
            pl.BlockSpec((GROUPS, VAUG_ROWS, tm), lambda i: (0, 0, i)),
            pl.BlockSpec((GROUPS, VAUG_ROWS, tm), lambda i: (0, 0, i)),
            pl.BlockSpec((tm, 256), lambda i: (i, 0)),
            pl.BlockSpec((tm, MEM_W), lambda i: (i, 0)),
        ],
        out_shape=[
            jax.ShapeDtypeStruct((NSA_HEADS, HEAD_DIM, s), BF16),
            jax.ShapeDtypeStruct((GATE_PAD, s), F32),
            jax.ShapeDtypeStruct((GROUPS, s, 128), BF16),
            jax.ShapeDtypeStruct((GROUPS, s, HEAD_DIM), BF16),
            jax.ShapeDtypeStruct((GROUPS, VAUG_ROWS, s), BF16),
            jax.ShapeDtypeStruct((GROUPS, VAUG_ROWS, s), BF16),
            jax.ShapeDtypeStruct((s, 256), BF16),
            jax.ShapeDtypeStruct((s, MEM_W), BF16),
        ],
        compiler_params=pltpu.CompilerParams(
            dimension_semantics=("arbitrary",), vmem_limit_bytes=VMEM_LIMIT),
        name="layer1_proj",
    )(x, gkv, gmix, wkv, w_in, gb, qg, mqg, kng, mkT, mv, bd, hm, pat)


def _compress_kernel(t_ref, w1_ref, w2_ref, w2T_ref, pe_ref, kg_ref, kgT_ref, o_ref, oT_ref, hb_ref):
    nc = t_ref.shape[1]
    half = CMP_STRIDE * HEAD_DIM
    t = t_ref[0]
    ha = _dot(t, w1_ref[0, 0:half, :])
    hb_ref[0:nc, :] = _dot(t, w1_ref[0, half:2 * half, :])
    hb_ref[nc:nc + 8, :] = jnp.zeros((8, CMP_HID), F32)
    pe = jnp.broadcast_to(pe_ref[0], (8, 2 * half)).astype(BF16)
    pe_term = _dot(pe, w1_ref[0])[0:1, :]
    h = ha + hb_ref[1:nc + 1, :] + pe_term
    hg = 0.5 * h * (1.0 + jnp.tanh(0.7978845608028654 * (h + 0.044715 * (h * h * h))))
    hg = hg.astype(BF16)
    o = _dot(hg, w2_ref[0])
    oT = lax.dot_general(w2T_ref[0], hg, (((1,), (1,)), ((), ())), preferred_element_type=F32)
    is_k = pl.program_id(0) < GROUPS
    on = o * lax.rsqrt(jnp.mean(o * o, axis=1, keepdims=True) + EPS) * kg_ref[...]
    oTn = oT * lax.rsqrt(jnp.mean(oT * oT, axis=0, keepdims=True) + EPS) * kgT_ref[...]
    o_ref[0] = jnp.where(is_k, on, o).astype(BF16)
    oT_ref[0, 0:HEAD_DIM, :] = jnp.where(is_k, oTn, oT).astype(BF16)
    oT_ref[0, HEAD_DIM:VAUG_ROWS, :] = (
        lax.broadcasted_iota(jnp.int32, (VAUG_ROWS - HEAD_DIM, nc), 0) == 0).astype(BF16)


def _compress(t, w1, w2, w2T, pe, kg, kgT):
    n, nc, width = t.shape
    return pl.pallas_call(
        _compress_kernel,
        grid=(n,),
        in_specs=[
            pl.BlockSpec((1, nc, width), lambda i: (i, 0, 0)),
            pl.BlockSpec((1, 2 * width, CMP_HID), lambda i: (i // GROUPS, 0, 0)),
            pl.BlockSpec((1, CMP_HID, HEAD_DIM), lambda i: (i // GROUPS, 0, 0)),
            pl.BlockSpec((1, HEAD_DIM, CMP_HID), lambda i: (i // GROUPS, 0, 0)),
            pl.BlockSpec((1, 1, 2 * width), lambda i: (i // GROUPS, 0, 0)),
            pl.BlockSpec((1, HEAD_DIM), lambda i: (0, 0)),
            pl.BlockSpec((HEAD_DIM, 1), lambda i: (0, 0)),
        ],
        out_specs=[
            pl.BlockSpec((1, nc, HEAD_DIM), lambda i: (i, 0, 0)),
            pl.BlockSpec((1, VAUG_ROWS, nc), lambda i: (i, 0, 0)),
        ],
        out_shape=[
            jax.ShapeDtypeStruct((n, nc, HEAD_DIM), BF16),
            jax.ShapeDtypeStruct((n, VAUG_ROWS, nc), BF16),
        ],
        scratch_shapes=[pltpu.VMEM((nc + 8, CMP_HID), F32)],
        compiler_params=pltpu.CompilerParams(
            dimension_semantics=("arbitrary",), vmem_limit_bytes=VMEM_LIMIT),
        name="compress_kv",
    )(t, w1, w2, w2T, pe, kg, kgT)


def _nsa_kernel(qT_ref, gT_ref, kc_ref, vcT_ref, ksel_ref, vselT_ref, kwin_ref, vwinT_ref, out_ref,
                qaug_st, psum_st, bias_st, acc_st, m_st, s_st, oc_st, sw_st, ow_st):
    cc = pl.program_id(1)
    nc = kc_ref.shape[1]
    nsb = bias_st.shape[1]
    hg = HEADS_PER_GROUP
    width = hg * Q_BLOCK
    n_chunks = KEY_TILE // KEY_CHUNK
    last = (cc * N_STREAMS) // (KEY_TILE // Q_BLOCK)

    def stream(st):
        c = cc * N_STREAMS + st
        qs = slice(st * Q_BLOCK, (st + 1) * Q_BLOCK)
        qaug_ref, psum_ref, bias_ref, acc_ref = qaug_st.at[st], psum_st.at[st], bias_st.at[st], acc_st.at[st]
        m_ref, s_ref, oc_ref, sw_ref, ow_ref = m_st.at[st], s_st.at[st], oc_st.at[st], sw_st.at[st], ow_st.at[st]

        for hh in range(hg):
            qaug_ref[0:HEAD_DIM, hh * Q_BLOCK:(hh + 1) * Q_BLOCK] = qT_ref[hh, :, qs]
        qaug_ref[HEAD_DIM:128, :] = jnp.zeros((128 - HEAD_DIM, width), BF16)
        qT = qaug_ref[0:HEAD_DIM, :]
        t_q = c * Q_BLOCK + (lax.broadcasted_iota(jnp.int32, (1, width), 1) & (Q_BLOCK - 1))

        def cmp_branch(rows):
            s = _dot_3tiles(kc_ref[0, 0:rows, :], qT)
            ci = lax.broadcasted_iota(jnp.int32, (rows, 1), 0)
            s = jnp.where(ci * CMP_STRIDE + (CMP_L - 1) <= t_q, s, -jnp.inf)
            m = jnp.max(s, axis=0, keepdims=True)
            m = jnp.where(m == -jnp.inf, 0.0, m)
            e = jnp.exp2(s - m)
            oa = _dot(vcT_ref[0, :, 0:rows], e.astype(BF16))
            rl = 1.0 / jnp.maximum(oa[HEAD_DIM:HEAD_DIM + 1, :], 1e-30)
            oc_ref[...] = oa[0:HEAD_DIM, :] * rl
            p = e * rl
            psum = p[:, 0:Q_BLOCK]
            for hh in range(1, hg):
                psum = psum + p[:, hh * Q_BLOCK:(hh + 1) * Q_BLOCK]
            psum_ref[8:8 + rows, :] = psum

        @pl.when(cc == 0)
        def _():
            psum_ref[...] = jnp.zeros(psum_ref.shape, F32)

        cuts = [nc * k // CMP_PATHS for k in range(1, CMP_PATHS + 1)] if nc % (128 * CMP_PATHS) == 0 else [nc]
        need = (c + 1) * (Q_BLOCK // CMP_STRIDE)
        for k, rows in enumerate(cuts):
            lo = cuts[k - 1] if k else 0
            pl.when((need > lo) & (need <= rows))(functools.partial(cmp_branch, rows))

        row0 = pl.multiple_of(jnp.maximum(c * Q_BLOCK - WIN, 0), Q_BLOCK)
        sw = _dot_3tiles(kwin_ref[0, pl.ds(row0, WIN_KEYS), :], qT)
        kp = row0 + lax.broadcasted_iota(jnp.int32, (WIN_KEYS, 1), 0)
        sw = jnp.where((kp <= t_q) & (kp > t_q - WIN), sw, -jnp.inf)
        sw_ref[...] = sw
        mw = jnp.max(jnp.max(sw.reshape(WIN_KEYS // 8, 8, width), axis=0), axis=0, keepdims=True)
        mw = jnp.where(mw == -jnp.inf, 0.0, mw)

        def select_blocks(rows):
            imp = (psum_ref[pl.ds(7, rows, stride=4), :] + psum_ref[pl.ds(11, rows, stride=4), :]
                   + 2.0 * (psum_ref[pl.ds(8, rows, stride=4), :] + psum_ref[pl.ds(9, rows, stride=4), :]
                            + psum_ref[pl.ds(10, rows, stride=4), :]))
            blk = lax.broadcasted_iota(jnp.int32, (rows, Q_BLOCK), 0)
            tb = (c * Q_BLOCK + lax.broadcasted_iota(jnp.int32, (1, Q_BLOCK), 1)) >> 6
            valid = blk <= tb
            forced = (blk == 0) | (blk == tb) | (blk == tb - 1)
            score = jnp.where(forced | ~valid, -jnp.inf, imp)
            blk_f = blk.astype(F32)

            def pick(_, sc):
                mx = jnp.max(sc, axis=0, keepdims=True)
                first = jnp.min(jnp.where(sc == mx, blk_f, float(rows)), axis=0, keepdims=True)
                return jnp.where(blk_f == first, -jnp.inf, sc)

            left = lax.fori_loop(0, N_SEL - 3, pick, score)
            sel = (left == -jnp.inf) & valid
            bias_ref[0:rows, :] = jnp.where(sel, 0.0, MASK_BIAS).astype(BF16)

        @pl.when(cc == 0)
        def _():
            bias_ref[...] = jnp.full(bias_ref.shape, MASK_BIAS, BF16)

        b_cuts = [nsb * k // SEL_PATHS for k in range(1, SEL_PATHS + 1)] if nsb % (16 * SEL_PATHS) == 0 else [nsb]
        need_b = 2 * c + 2
        for k, rows in enumerate(b_cuts):
            lo = b_cuts[k - 1] if k else 0
            pl.when((need_b > lo) & (need_b <= rows))(functools.partial(select_blocks, rows))

        acc_ref[...] = jnp.zeros(acc_ref.shape, F32)
        m_ref[...] = jnp.full(m_ref.shape, M_INIT, F32)

        def set_bias(j):
            span = j // (ONEHOT_SPAN // KEY_TILE)
            b16 = bias_ref[pl.ds(pl.multiple_of(span * SEL_PER_SPAN, SEL_PER_SPAN), SEL_PER_SPAN), :]
            for hh in range(hg):
                qaug_ref[HEAD_DIM:HEAD_DIM + SEL_PER_SPAN, hh * Q_BLOCK:(hh + 1) * Q_BLOCK] = b16

        def scores(j, r):
            k0 = pl.multiple_of(j * KEY_TILE + r * KEY_CHUNK, KEY_CHUNK)
            return _dot(ksel_ref[0, pl.ds(k0, KEY_CHUNK), :], qaug_ref[...])

        def chunk_max(cmax, sc):
            return jnp.maximum(cmax, jnp.max(sc.reshape(KEY_CHUNK // 8, 8, width), axis=0))

        def new_max(cmax):
            m_old = m_ref[...]
            m_new = jnp.maximum(m_old, jnp.max(cmax, axis=0, keepdims=True))
            m_ref[...] = m_new
            return m_new, jnp.exp2(m_old - m_new)

        def values(j, r):
            k0 = pl.multiple_of(j * KEY_TILE + r * KEY_CHUNK, KEY_CHUNK)
            return vselT_ref[0, :, pl.ds(k0, KEY_CHUNK)]

        cmax0 = jnp.full((8, width), MASK_BIAS, F32)
        set_bias(0)
        cmax = cmax0
        ow = jnp.zeros((VAUG_ROWS, width), F32)
        w_cuts = [WIN_KEYS * r // n_chunks // 128 * 128 for r in range(n_chunks)] + [WIN_KEYS]
        for r in range(n_chunks):
            sc = _dot_3tiles(ksel_ref[0, r * KEY_CHUNK:(r + 1) * KEY_CHUNK, :], qaug_ref[...])
            ws = slice(w_cuts[r], w_cuts[r + 1])
            pw = jnp.exp2(sw_ref[ws, :] - mw).astype(BF16)
            s_ref[r * KEY_CHUNK:(r + 1) * KEY_CHUNK, :] = sc
            cmax = chunk_max(cmax, sc)
            ow = ow + _dot(vwinT_ref[0, :, pl.ds(pl.multiple_of(row0 + w_cuts[r], 128), w_cuts[r + 1] - w_cuts[r])], pw)
        ow_ref[...] = ow[0:HEAD_DIM, :] / jnp.maximum(ow[HEAD_DIM:HEAD_DIM + 1, :], 1e-30)

        def pipe_step(j, cmax):
            m_new, alpha = new_max(cmax)
            set_bias(j + 1)
            acc = acc_ref[...] * alpha
            cnext = cmax0
            for r in range(n_chunks):
                rs = slice(r * KEY_CHUNK, (r + 1) * KEY_CHUNK)
                sc = scores(j + 1, r)
                pj = jnp.exp2(s_ref[rs, :] - m_new).astype(BF16)
                s_ref[rs, :] = sc
                cnext = chunk_max(cnext, sc)
                acc = acc + _dot(values(j, r), pj)
            acc_ref[...] = acc
            return cnext

        def finish():
            diag = pl.ds(pl.multiple_of((c % (KEY_TILE // Q_BLOCK)) * Q_BLOCK, Q_BLOCK), Q_BLOCK)
            ku = lax.broadcasted_iota(jnp.int32, (Q_BLOCK, 1), 0)
            s_ref[diag, :] = jnp.where(ku <= t_q - c * Q_BLOCK, s_ref[diag, :], MASK_BIAS)

            def finish_tile(rows):
                s_last = s_ref[0:rows, :]
                m_new, alpha = new_max(jnp.max(s_last.reshape(rows // 8, 8, width), axis=0))
                p_last = jnp.exp2(s_last - m_new).astype(BF16)
                acc_ref[...] = acc_ref[...] * alpha + _dot(
                    vselT_ref[0, :, pl.ds(pl.multiple_of(last * KEY_TILE, KEY_TILE), rows)], p_last)

            diag_chunk = (c % (KEY_TILE // Q_BLOCK)) // (KEY_CHUNK // Q_BLOCK)
            for k in range(n_chunks):
                pl.when(diag_chunk == k)(functools.partial(finish_tile, (k + 1) * KEY_CHUNK))
            o_cmp = oc_ref[...]
            o_win = ow_ref[...]
            o_slc = acc_ref[0:HEAD_DIM, :] / jnp.maximum(acc_ref[HEAD_DIM:HEAD_DIM + 1, :], 1e-30)

            heads = []
            for hh in range(hg):
                hs = slice(hh * Q_BLOCK, (hh + 1) * Q_BLOCK)
                heads.append(o_cmp[:, hs] * gT_ref[hh:hh + 1, qs] + o_slc[:, hs] * gT_ref[8 + hh:9 + hh, qs]
                             + o_win[:, hs] * gT_ref[16 + hh:17 + hh, qs])
            for pr in range(hg // 2):
                pair = jnp.concatenate([heads[2 * pr], heads[2 * pr + 1]], axis=0)
                out_ref[qs, pr * 128:(pr + 1) * 128] = pair.T.astype(BF16)

        return cmax, pipe_step, finish

    streams = [stream(st) for st in range(N_STREAMS)]

    def sweep_steps(j, cmaxes):
        return tuple(step(j, cm) for (_, step, _), cm in zip(streams, cmaxes))

    lax.fori_loop(0, last, sweep_steps, tuple(cm for cm, _, _ in streams))
    for _, _, finish in streams:
        finish()


def _nsa(qT, gT, kc, vcT, ksel, vselT, kwin, vwinT):
    s = ksel.shape[1]
    nc = kc.shape[1]
    nsb = s // SEL_L
    hg = HEADS_PER_GROUP
    ns = N_STREAMS
    qb = ns * Q_BLOCK
    width = hg * Q_BLOCK
    grp = lambda g, c: (g, 0, 0)
    return pl.pallas_call(
        _nsa_kernel,
        grid=(GROUPS, s // qb),
        in_specs=[
            pl.BlockSpec((hg, HEAD_DIM, qb), lambda g, c: (g, 0, c)),
            pl.BlockSpec((GATE_ROWS, qb), lambda g, c: (g, c)),
            pl.BlockSpec((1, nc, HEAD_DIM), grp),
            pl.BlockSpec((1, VAUG_ROWS, nc), lambda g, c: (GROUPS + g, 0, 0)),
            pl.BlockSpec((1, s, 128), grp),
            pl.BlockSpec((1, VAUG_ROWS, s), grp),
            pl.BlockSpec((1, s, HEAD_DIM), grp),
            pl.BlockSpec((1, VAUG_ROWS, s), grp),
        ],
        out_specs=pl.BlockSpec((qb, hg * HEAD_DIM), lambda g, c: (c, g)),
        out_shape=jax.ShapeDtypeStruct((s, NSA_W), BF16),
        scratch_shapes=[
            pltpu.VMEM((ns, 128, width), BF16),
            pltpu.VMEM((ns, nc + 16, Q_BLOCK), F32),
            pltpu.VMEM((ns, nsb, Q_BLOCK), BF16),
            pltpu.VMEM((ns, VAUG_ROWS, width), F32),
            pltpu.VMEM((ns, 1, width), F32),
            pltpu.VMEM((ns, KEY_TILE, width), F32),
            pltpu.VMEM((ns, HEAD_DIM, width), F32),
            pltpu.VMEM((ns, WIN_KEYS, width), F32),
            pltpu.VMEM((ns, HEAD_DIM, width), F32),
        ],
        compiler_params=pltpu.CompilerParams(
            dimension_semantics=("arbitrary", "arbitrary"), vmem_limit_bytes=VMEM_LIMIT),
        name="nsa_attention",
    )(qT, gT, kc, vcT, ksel, vselT, kwin, vwinT)


def _block_diag_ones():
    idx = np.arange(MEM_W) // HEAD_DIM
    return jnp.asarray((idx[:, None] == idx[None, :]).astype(np.float32), BF16)


def _head_masks():
    idx = np.arange(MEM_W) // HEAD_DIM
    return jnp.asarray((idx[None, :] == np.arange(MEM_HEADS)[:, None]).astype(np.float32))


def _onehot_pattern():
    rows = np.arange(ONEHOT_SPAN)[:, None] // SEL_L
    lanes = np.arange(128)[None, :] - HEAD_DIM
    return jnp.asarray((rows == lanes).astype(np.float32))


def _gate_layout():
    src = np.full((GATE_PAD,), -1, np.int64)
    for g in range(GROUPS):
        for b in range(3):
            for hh in range(HEADS_PER_GROUP):
                src[g * GATE_ROWS + b * 8 + hh] = (g * HEADS_PER_GROUP + hh) * 3 + b
    return src


def kernel(x, mem, norm_mix_g, norm_mlp_g, mem_norm_g, w_mem_kv, mem_q_norm_g, mem_k_norm_g, w_out, w_mlp_in, w_mlp_out, a_w_in, a_b_glu, a_dw, a_dw_b, a_ln_g, a_ln_b, b_w_in, b_gate_b, b_q_norm_g, kv_norm_g, w_kv, k_norm_g, cmp_pe_k, cmp_pe_v, cmp_w1_k, cmp_w2_k, cmp_w1_v, cmp_w2_v):
    batch, s, _ = x.shape
    assert batch == 1 and s % ONEHOT_SPAN == 0 and s >= WIN_KEYS and (KEY_TILE // Q_BLOCK) % N_STREAMS == 0
    assert w_out.shape[0] == 2 and a_w_in.shape[0] == 1 and b_w_in.shape[0] == 1
    nc = s // CMP_STRIDE
    row = lambda v: v.reshape(1, -1)
    bd = _block_diag_ones()
    hm = _head_masks()

    mkT, mv = _memkv(mem[0], row(mem_norm_g), w_mem_kv.astype(BF16),
                     jnp.tile(mem_k_norm_g, (1, MEM_HEADS))[:, None, :], bd)

    conv0, memo0 = _layer0(
        x[0], row(norm_mix_g[0]), a_w_in[0].astype(BF16), row(a_b_glu[0]), a_dw[0], row(a_dw_b[0]),
        row(a_ln_g[0]), row(a_ln_b[0]), row(jnp.tile(mem_q_norm_g[0], MEM_HEADS)), mkT, mv, bd, hm)
    w_out_bf = w_out.astype(BF16)
    w_in_bf = w_mlp_in.astype(BF16)
    w_o_bf = w_mlp_out.astype(BF16)
    x1 = _post(x[0], conv0, memo0, w_out_bf[0, :CONV_W], w_out_bf[0, CONV_W:], row(norm_mlp_g[0]),
               w_in_bf[0], w_o_bf[0])

    src = _gate_layout()
    used = src >= 0
    w_gate = jnp.where(used[None, :], b_w_in[0][:, NSA_W + MEM_W + np.maximum(src, 0)], 0.0)
    b_gate = jnp.where(used, b_gate_b[0][np.maximum(src, 0)], 0.0)
    w_in1 = jnp.concatenate([b_w_in[0][:, :NSA_W + MEM_W], w_gate], axis=1).astype(BF16)
    q_gain = jnp.tile(b_q_norm_g[0], NSA_HEADS) * (QK_SCALE * LOG2E)
    kng = jnp.stack([jnp.tile(k_norm_g[1], GROUPS), jnp.tile(k_norm_g[2], GROUPS)])
    qT, gT, ksel, kwin, vselT, vwinT, kvc, memo1 = _layer1_proj(
        x1, row(kv_norm_g), row(norm_mix_g[1]), w_kv.astype(BF16), w_in1, row(b_gate), row(q_gain),
        row(jnp.tile(mem_q_norm_g[1], MEM_HEADS)), kng, mkT, mv, bd, hm, _onehot_pattern())

    t = kvc.reshape(s, 2 * GROUPS, HEAD_DIM).transpose(1, 0, 2).reshape(2 * GROUPS, nc, CMP_STRIDE * HEAD_DIM)
    w1 = jnp.stack([cmp_w1_k, cmp_w1_v]).astype(BF16)
    w2 = jnp.stack([cmp_w2_k, cmp_w2_v]).astype(BF16)
    pe = jnp.stack([cmp_pe_k.reshape(1, -1), cmp_pe_v.reshape(1, -1)])
    cmp_rows, cmp_cols = _compress(t, w1, w2, w2.transpose(0, 2, 1), pe, row(k_norm_g[0]),
                                   k_norm_g[0].reshape(-1, 1))

    nsa = _nsa(qT, gT, cmp_rows, cmp_cols, ksel, vselT, kwin, vwinT)
    x2 = _post(x1, nsa, memo1, w_out_bf[1, :NSA_W], w_out_bf[1, NSA_W:], row(norm_mlp_g[1]),
               w_in_bf[1], w_o_bf[1])
    return x2[None]
```

```python
import functools

import numpy as np
import jax
import jax.numpy as jnp
from jax import lax
from jax.experimental import pallas as pl
from jax.experimental.pallas import tpu as pltpu

F32 = jnp.float32
BF16 = jnp.bfloat16

D_MODEL = 1024
HEAD_DIM = 64
MEM_LEN = 256
MEM_HEADS = 4
MEM_W = MEM_HEADS * HEAD_DIM
CONV_W = D_MODEL - MEM_W
CONV_K = 31
NSA_HEADS = CONV_W // HEAD_DIM
NSA_W = NSA_HEADS * HEAD_DIM
GROUPS = 2
HEADS_PER_GROUP = NSA_HEADS // GROUPS
CMP_L = 32
CMP_STRIDE = 16
CMP_HID = 256
SEL_L = 64
N_SEL = 16
WIN = 512
Q_BLOCK = 128
D_FF = 4 * D_MODEL
GATE_PAD = 128
GATE_ROWS = GATE_PAD // GROUPS

ROW_TILE = 512
FF_CHUNK = 1024
KEY_TILE = 1024
ONEHOT_SPAN = 1024
CMP_PATHS = 4
SEL_PATHS = 4
N_STREAMS = 4
KEY_CHUNK = 256
SEL_PER_SPAN = ONEHOT_SPAN // SEL_L
WIN_KEYS = WIN + Q_BLOCK
CONV_HALO = 32
CONV_ROWS = 64
VAUG_ROWS = 80
VMEM_LIMIT = 56 * 1024 * 1024
MXU_WIDTH = 256

EPS = 1e-6
LN_EPS = 1e-5
LOG2E = 1.4426950408889634
QK_SCALE = HEAD_DIM ** -0.5
MASK_BIAS = -2.0 ** 126
M_INIT = -2.0 ** 100


def _dot(a, b):
    return jnp.dot(a, b, preferred_element_type=F32)


def _dot_3tiles(a, b):
    two = 2 * MXU_WIDTH
    half = a.shape[0] // 2
    return jnp.concatenate([
        _dot(a, b[:, 0:two]),
        jnp.concatenate([_dot(a[0:half], b[:, two:]), _dot(a[half:], b[:, two:])], axis=0),
    ], axis=1)


def _rms_scale(x):
    return lax.rsqrt(jnp.mean(x * x, axis=-1, keepdims=True) + EPS)


def _head_meansq(x, bd):
    x2 = x * x
    hi = x2.astype(BF16)
    lo = (x2 - hi.astype(F32)).astype(BF16)
    return (_dot(hi, bd) + _dot(lo, bd)) * (1.0 / HEAD_DIM)


def _mem_attention(qn, mkT, mv, hm_ref):
    out = jnp.zeros(qn.shape, F32)
    for h in range(MEM_HEADS):
        hm = hm_ref[h:h + 1, :]
        s = _dot((qn * hm).astype(BF16), mkT)
        e = jnp.exp2(s - jnp.max(s, axis=-1, keepdims=True))
        l = jnp.sum(e, axis=-1, keepdims=True)
        out = out + _dot(e.astype(BF16), mv) * (hm / l)
    return out


def _memkv_kernel(mem_ref, g_ref, w_ref, kg_ref, bd_ref, mkT_ref, mv_ref):
    m = mem_ref[...]
    mn = (m * _rms_scale(m) * g_ref[...]).astype(BF16)
    kv = _dot(mn, w_ref[0])
    k = kv[:, :MEM_W]
    kn = k * lax.rsqrt(_head_meansq(k, bd_ref[...]) + EPS) * kg_ref[0]
    mkT_ref[0] = (kn * (QK_SCALE * LOG2E)).T.astype(BF16)
    mv_ref[0] = kv[:, MEM_W:].astype(BF16)


def _memkv(mem, g, w, kg, bd):
    depth = w.shape[0]
    return pl.pallas_call(
        _memkv_kernel,
        grid=(depth,),
        in_specs=[
            pl.BlockSpec((MEM_LEN, D_MODEL), lambda l: (0, 0)),
            pl.BlockSpec((1, D_MODEL), lambda l: (0, 0)),
            pl.BlockSpec((1, D_MODEL, 2 * MEM_W), lambda l: (l, 0, 0)),
            pl.BlockSpec((1, 1, MEM_W), lambda l: (l, 0, 0)),
            pl.BlockSpec((MEM_W, MEM_W), lambda l: (0, 0)),
        ],
        out_specs=[
            pl.BlockSpec((1, MEM_W, MEM_LEN), lambda l: (l, 0, 0)),
            pl.BlockSpec((1, MEM_LEN, MEM_W), lambda l: (l, 0, 0)),
        ],
        out_shape=[
            jax.ShapeDtypeStruct((depth, MEM_W, MEM_LEN), BF16),
            jax.ShapeDtypeStruct((depth, MEM_LEN, MEM_W), BF16),
        ],
        name="mem_kv",
    )(mem, g, w, kg, bd)


def _layer0_kernel(x_ref, g_ref, win_ref, bglu_ref, dw_ref, dwb_ref, lng_ref, lnb_ref, qg_ref,
                   mkT_ref, mv_ref, bd_ref, hm_ref, conv_ref, memo_ref, buf_ref, cv_ref, sh_ref):
    i = pl.program_id(0)
    tm = x_ref.shape[0]
    x = x_ref[...]
    h = (x * _rms_scale(x) * g_ref[...]).astype(BF16)
    u = _dot(h, win_ref[...])
    a = u[:, :CONV_W] + bglu_ref[:, :CONV_W]
    gate = u[:, CONV_W:2 * CONV_W] + bglu_ref[:, CONV_W:]
    v = a * jax.nn.sigmoid(gate)

    @pl.when(i == 0)
    def _():
        buf_ref[0:CONV_HALO, :] = jnp.zeros((CONV_HALO, CONV_W), F32)

    buf_ref[CONV_HALO:CONV_HALO + tm, :] = v

    base = CONV_HALO - (CONV_K - 1)
    rows = tm + CONV_HALO
    for cb in range(CONV_W // 128):
        cs = slice(cb * 128, (cb + 1) * 128)
        xb = buf_ref[:, cs]
        sh_ref[0] = xb
        for b in range(1, 8):
            sh_ref[b] = pltpu.roll(xb, rows - b, axis=0)

        def conv_rows(r, carry, cs=cs):
            r0 = pl.multiple_of(r * CONV_ROWS, CONV_ROWS)
            acc = jnp.zeros((CONV_ROWS, 128), F32) + dwb_ref[:, cs]
            for k in range(CONV_K):
                a, b = divmod(base + k, 8)
                acc = acc + dw_ref[k:k + 1, cs] * sh_ref[b, pl.ds(r0 + 8 * a, CONV_ROWS), :]
            cv_ref[pl.ds(r0, CONV_ROWS), cs] = acc
            return carry

        lax.fori_loop(0, tm // CONV_ROWS, conv_rows, 0)
    buf_ref[0:CONV_HALO, :] = buf_ref[tm:tm + CONV_HALO, :]

    cv = cv_ref[...]
    mu = jnp.mean(cv, axis=-1, keepdims=True)
    d = cv - mu
    var = jnp.mean(d * d, axis=-1, keepdims=True)
    y = d * lax.rsqrt(var + LN_EPS) * lng_ref[...] + lnb_ref[...]
    conv_ref[...] = (y * jax.nn.sigmoid(y)).astype(BF16)

    qm = u[:, 2 * CONV_W:]
    qn = qm * lax.rsqrt(_head_meansq(qm, bd_ref[...]) + EPS) * qg_ref[...]
    memo_ref[...] = _mem_attention(qn, mkT_ref[0], mv_ref[0], hm_ref).astype(BF16)


def _layer0(x, g, w_in, bglu, dw, dwb, lng, lnb, qg, mkT, mv, bd, hm):
    s = x.shape[0]
    tm = ROW_TILE
    const = lambda i: (0, 0)
    return pl.pallas_call(
        _layer0_kernel,
        grid=(s // tm,),
        in_specs=[
            pl.BlockSpec((tm, D_MODEL), lambda i: (i, 0)),
            pl.BlockSpec((1, D_MODEL), const),
            pl.BlockSpec(w_in.shape, const),
            pl.BlockSpec((1, 2 * CONV_W), const),
            pl.BlockSpec((CONV_K, CONV_W), const),
            pl.BlockSpec((1, CONV_W), const),
            pl.BlockSpec((1, CONV_W), const),
            pl.BlockSpec((1, CONV_W), const),
            pl.BlockSpec((1, MEM_W), const),
            pl.BlockSpec((1, MEM_W, MEM_LEN), lambda i: (0, 0, 0)),
            pl.BlockSpec((1, MEM_LEN, MEM_W), lambda i: (0, 0, 0)),
            pl.BlockSpec((MEM_W, MEM_W), const),
            pl.BlockSpec((MEM_HEADS, MEM_W), const),
        ],
        out_specs=[
            pl.BlockSpec((tm, CONV_W), lambda i: (i, 0)),
            pl.BlockSpec((tm, MEM_W), lambda i: (i, 0)),
        ],
        out_shape=[
            jax.ShapeDtypeStruct((s, CONV_W), BF16),
            jax.ShapeDtypeStruct((s, MEM_W), BF16),
        ],
        scratch_shapes=[
            pltpu.VMEM((tm + CONV_HALO, CONV_W), F32),
            pltpu.VMEM((tm, CONV_W), F32),
            pltpu.VMEM((8, tm + CONV_HALO, 128), F32),
        ],
        compiler_params=pltpu.CompilerParams(
            dimension_semantics=("arbitrary",), vmem_limit_bytes=VMEM_LIMIT),
        name="layer0_mixer",
    )(x, g, w_in, bglu, dw, dwb, lng, lnb, qg, mkT, mv, bd, hm)


def _post_kernel(x_ref, ma_ref, mb_ref, wa_ref, wb_ref, g_ref, win_ref, wout_ref, o_ref):
    o_ref[...] = x_ref[...] + _dot(ma_ref[...], wa_ref[...]) + _dot(mb_ref[...], wb_ref[...])
    x1 = o_ref[...]
    h = (x1 * _rms_scale(x1) * g_ref[...]).astype(BF16)
    for c in range(D_FF // FF_CHUNK):
        cs = slice(c * FF_CHUNK, (c + 1) * FF_CHUNK)
        t = jnp.maximum(_dot(h, win_ref[:, cs]), 0.0)
        o_ref[...] += _dot((t * t).astype(BF16), wout_ref[cs, :])


def _post(x, ma, mb, wa, wb, g, w_in, w_out):
    s = x.shape[0]
    tm = ROW_TILE
    const = lambda i: (0, 0)
    single = pl.Buffered(1)
    return pl.pallas_call(
        _post_kernel,
        grid=(s // tm,),
        in_specs=[
            pl.BlockSpec((tm, D_MODEL), lambda i: (i, 0)),
            pl.BlockSpec((tm, CONV_W), lambda i: (i, 0)),
            pl.BlockSpec((tm, MEM_W), lambda i: (i, 0)),
            pl.BlockSpec((CONV_W, D_MODEL), const, pipeline_mode=single),
            pl.BlockSpec((MEM_W, D_MODEL), const, pipeline_mode=single),
            pl.BlockSpec((1, D_MODEL), const),
            pl.BlockSpec((D_MODEL, D_FF), const, pipeline_mode=single),
            pl.BlockSpec((D_FF, D_MODEL), const, pipeline_mode=single),
        ],
        out_specs=pl.BlockSpec((tm, D_MODEL), lambda i: (i, 0)),
        out_shape=jax.ShapeDtypeStruct((s, D_MODEL), F32),
        compiler_params=pltpu.CompilerParams(
            dimension_semantics=("arbitrary",), vmem_limit_bytes=VMEM_LIMIT),
        name="outproj_mlp",
    )(x, ma, mb, wa, wb, g, w_in, w_out)


def _layer1_proj_kernel(x_ref, gkv_ref, gmix_ref, wkv_ref, win_ref, gb_ref, qg_ref, mqg_ref, kng_ref,
                        mkT_ref, mv_ref, bd_ref, hm_ref, pat_ref,
                        qT_ref, gT_ref, ksel_ref, kwin_ref, vselT_ref, vwinT_ref, kvc_ref, memo_ref):
    tm = x_ref.shape[0]
    x = x_ref[...]
    xn = x * _rms_scale(x)
    bd = bd_ref[...]

    kv = _dot((xn * gkv_ref[...]).astype(BF16), wkv_ref[...])
    kvc_ref[...] = kv[:, 0:256].astype(BF16)
    k2 = kv[:, 256:384]
    v2 = kv[:, 384:512]
    kw = kv[:, 512:640]
    vw = kv[:, 640:768]
    bd2 = bd[0:128, 0:128]
    k2n = k2 * lax.rsqrt(_head_meansq(k2, bd2) + EPS) * kng_ref[0:1, :]
    kwn = kw * lax.rsqrt(_head_meansq(kw, bd2) + EPS) * kng_ref[1:2, :]

    lane = lax.broadcasted_iota(jnp.int32, (tm, 128), 1)
    pat = pat_ref[...]
    ksel_ref[0] = jnp.where(lane < HEAD_DIM, k2n, pat).astype(BF16)
    ksel_ref[1] = jnp.where(lane < HEAD_DIM, pltpu.roll(k2n, HEAD_DIM, axis=1), pat).astype(BF16)
    kwin_ref[0] = kwn[:, :HEAD_DIM].astype(BF16)
    kwin_ref[1] = kwn[:, HEAD_DIM:].astype(BF16)

    ones_rows = (lax.broadcasted_iota(jnp.int32, (VAUG_ROWS - HEAD_DIM, tm), 0) == 0).astype(BF16)
    v2T = v2.T.astype(BF16)
    vwT = vw.T.astype(BF16)
    for g in range(GROUPS):
        vselT_ref[g, 0:HEAD_DIM, :] = v2T[g * HEAD_DIM:(g + 1) * HEAD_DIM, :]
        vselT_ref[g, HEAD_DIM:VAUG_ROWS, :] = ones_rows
        vwinT_ref[g, 0:HEAD_DIM, :] = vwT[g * HEAD_DIM:(g + 1) * HEAD_DIM, :]
        vwinT_ref[g, HEAD_DIM:VAUG_ROWS, :] = ones_rows

    u = _dot((xn * gmix_ref[...]).astype(BF16), win_ref[...])
    for cb in range(NSA_W // 256):
        cs = slice(cb * 256, (cb + 1) * 256)
        qc = u[:, cs]
        qn = qc * lax.rsqrt(_head_meansq(qc, bd) + EPS) * qg_ref[:, cs]
        qT_ref[4 * cb:4 * cb + 4] = qn.T.astype(BF16).reshape(4, HEAD_DIM, tm)

    gates = jax.nn.sigmoid(u[:, NSA_W + MEM_W:] + gb_ref[...])
    gT_ref[...] = gates.T

    qm = u[:, NSA_W:NSA_W + MEM_W]
    qmn = qm * lax.rsqrt(_head_meansq(qm, bd) + EPS) * mqg_ref[...]
    memo_ref[...] = _mem_attention(qmn, mkT_ref[0], mv_ref[0], hm_ref).astype(BF16)


def _layer1_proj(x, gkv, gmix, wkv, w_in, gb, qg, mqg, kng, mkT, mv, bd, hm, pat):
    s = x.shape[0]
    tm = ROW_TILE
    const = lambda i: (0, 0)
    pat_blocks = pat.shape[0] // tm
    return pl.pallas_call(
        _layer1_proj_kernel,
        grid=(s // tm,),
        in_specs=[
            pl.BlockSpec((tm, D_MODEL), lambda i: (i, 0)),
            pl.BlockSpec((1, D_MODEL), const),
            pl.BlockSpec((1, D_MODEL), const),
            pl.BlockSpec(wkv.shape, const),
            pl.BlockSpec(w_in.shape, const),
            pl.BlockSpec((1, GATE_PAD), const),
            pl.BlockSpec((1, NSA_W), const),
            pl.BlockSpec((1, MEM_W), const),
            pl.BlockSpec((2, 128), const),
            pl.BlockSpec((1, MEM_W, MEM_LEN), lambda i: (1, 0, 0)),
            pl.BlockSpec((1, MEM_LEN, MEM_W), lambda i: (1, 0, 0)),
            pl.BlockSpec((MEM_W, MEM_W), const),
            pl.BlockSpec((MEM_HEADS, MEM_W), const),
            pl.BlockSpec((tm, 128), lambda i: (i % pat_blocks, 0)),
        ],
        out_specs=[
            pl.BlockSpec((NSA_HEADS, HEAD_DIM, tm), lambda i: (0, 0, i)),
            pl.BlockSpec((GATE_PAD, tm), lambda i: (0, i)),
            pl.BlockSpec((GROUPS, tm, 128), lambda i: (0, i, 0)),
            pl.BlockSpec((GROUPS, tm, HEAD_DIM), lambda i: (0, i, 0)),
            pl.BlockSpec((GROUPS, VAUG_ROWS, tm), lambda i: (0, 0, i)),
            pl.BlockSpec((GROUPS, VAUG_ROWS, tm), lambda i: (0, 0, i)),
            pl.BlockSpec((tm, 256), lambda i: (i, 0)),
            pl.BlockSpec((tm, MEM_W), lambda i: (i, 0)),
        ],
        out_shape=[
            jax.ShapeDtypeStruct((NSA_HEADS, HEAD_DIM, s), BF16),
            jax.ShapeDtypeStruct((GATE_PAD, s), F32),
            jax.ShapeDtypeStruct((GROUPS, s, 128), BF16),
            jax.ShapeDtypeStruct((GROUPS, s, HEAD_DIM), BF16),
            jax.ShapeDtypeStruct((GROUPS, VAUG_ROWS, s), BF16),
            jax.ShapeDtypeStruct((GROUPS, VAUG_ROWS, s), BF16),
            jax.ShapeDtypeStruct((s, 256), BF16),
            jax.ShapeDtypeStruct((s, MEM_W), BF16),
        ],
        compiler_params=pltpu.CompilerParams(
            dimension_semantics=("arbitrary",), vmem_limit_bytes=VMEM_LIMIT),
        name="layer1_proj",
    )(x, gkv, gmix, wkv, w_in, gb, qg, mqg, kng, mkT, mv, bd, hm, pat)


def _compress_kernel(t_ref, w1_ref, w2_ref, w2T_ref, pe_ref, kg_ref, kgT_ref, o_ref, oT_ref, hb_ref):
    nc = t_ref.shape[1]
    half = CMP_STRIDE * HEAD_DIM
    t = t_ref[0]
    ha = _dot(t, w1_ref[0, 0:half, :])
    hb_ref[0:nc, :] = _dot(t, w1_ref[0, half:2 * half, :])
    hb_ref[nc:nc + 8, :] = jnp.zeros((8, CMP_HID), F32)
    pe = jnp.broadcast_to(pe_ref[0], (8, 2 * half)).astype(BF16)
    pe_term = _dot(pe, w1_ref[0])[0:1, :]
    h = ha + hb_ref[1:nc + 1, :] + pe_term
    hg = 0.5 * h * (1.0 + jnp.tanh(0.7978845608028654 * (h + 0.044715 * (h * h * h))))
    hg = hg.astype(BF16)
    o = _dot(hg, w2_ref[0])
    oT = lax.dot_general(w2T_ref[0], hg, (((1,), (1,)), ((), ())), preferred_element_type=F32)
    is_k = pl.program_id(0) < GROUPS
    on = o * lax.rsqrt(jnp.mean(o * o, axis=1, keepdims=True) + EPS) * kg_ref[...]
    oTn = oT * lax.rsqrt(jnp.mean(oT * oT, axis=0, keepdims=True) + EPS) * kgT_ref[...]
    o_ref[0] = jnp.where(is_k, on, o).astype(BF16)
    oT_ref[0, 0:HEAD_DIM, :] = jnp.where(is_k, oTn, oT).astype(BF16)
    oT_ref[0, HEAD_DIM:VAUG_ROWS, :] = (
        lax.broadcasted_iota(jnp.int32, (VAUG_ROWS - HEAD_DIM, nc), 0) == 0).astype(BF16)


def _compress(t, w1, w2, w2T, pe, kg, kgT):
    n, nc, width = t.shape
    return pl.pallas_call(
        _compress_kernel,
        grid=(n,),
        in_specs=[
            pl.BlockSpec((1, nc, width), lambda i: (i, 0, 0)),
            pl.BlockSpec((1, 2 * width, CMP_HID), lambda i: (i // GROUPS, 0, 0)),
            pl.BlockSpec((1, CMP_HID, HEAD_DIM), lambda i: (i // GROUPS, 0, 0)),
            pl.BlockSpec((1, HEAD_DIM, CMP_HID), lambda i: (i // GROUPS, 0, 0)),
            pl.BlockSpec((1, 1, 2 * width), lambda i: (i // GROUPS, 0, 0)),
            pl.BlockSpec((1, HEAD_DIM), lambda i: (0, 0)),
            pl.BlockSpec((HEAD_DIM, 1), lambda i: (0, 0)),
        ],
        out_specs=[
            pl.BlockSpec((1, nc, HEAD_DIM), lambda i: (i, 0, 0)),
            pl.BlockSpec((1, VAUG_ROWS, nc), lambda i: (i, 0, 0)),
        ],
        out_shape=[
            jax.ShapeDtypeStruct((n, nc, HEAD_DIM), BF16),
            jax.ShapeDtypeStruct((n, VAUG_ROWS, nc), BF16),
        ],
        scratch_shapes=[pltpu.VMEM((nc + 8, CMP_HID), F32)],
        compiler_params=pltpu.CompilerParams(
            dimension_semantics=("arbitrary",), vmem_limit_bytes=VMEM_LIMIT),
        name="compress_kv",
    )(t, w1, w2, w2T, pe, kg, kgT)


def _nsa_kernel(qT_ref, gT_ref, kc_ref, vcT_ref, ksel_ref, vselT_ref, kwin_ref, vwinT_ref, out_ref,
                qaug_st, psum_st, bias_st, acc_st, m_st, s_st, oc_st, sw_st, ow_st):
    cc = pl.program_id(1)
    nc = kc_ref.shape[1]
    nsb = bias_st.shape[1]
    hg = HEADS_PER_GROUP
    width = hg * Q_BLOCK
    n_chunks = KEY_TILE // KEY_CHUNK
    last = (cc * N_STREAMS) // (KEY_TILE // Q_BLOCK)

    def stream(st):
        c = cc * N_STREAMS + st
        qs = slice(st * Q_BLOCK, (st + 1) * Q_BLOCK)
        qaug_ref, psum_ref, bias_ref, acc_ref = qaug_st.at[st], psum_st.at[st], bias_st.at[st], acc_st.at[st]
        m_ref, s_ref, oc_ref, sw_ref, ow_ref = m_st.at[st], s_st.at[st], oc_st.at[st], sw_st.at[st], ow_st.at[st]

        for hh in range(hg):
            qaug_ref[0:HEAD_DIM, hh * Q_BLOCK:(hh + 1) * Q_BLOCK] = qT_ref[hh, :, qs]
        qaug_ref[HEAD_DIM:128, :] = jnp.zeros((128 - HEAD_DIM, width), BF16)
        qT = qaug_ref[0:HEAD_DIM, :]
        t_q = c * Q_BLOCK + (lax.broadcasted_iota(jnp.int32, (1, width), 1) & (Q_BLOCK - 1))

        def cmp_branch(rows):
            s = _dot_3tiles(kc_ref[0, 0:rows, :], qT)
            ci = lax.broadcasted_iota(jnp.int32, (rows, 1), 0)
            s = jnp.where(ci * CMP_STRIDE + (CMP_L - 1) <= t_q, s, -jnp.inf)
            m = jnp.max(s, axis=0, keepdims=True)
            m = jnp.where(m == -jnp.inf, 0.0, m)
            e = jnp.exp2(s - m)
            oa = _dot(vcT_ref[0, :, 0:rows], e.astype(BF16))
            rl = 1.0 / jnp.maximum(oa[HEAD_DIM:HEAD_DIM + 1, :], 1e-30)
            oc_ref[...] = oa[0:HEAD_DIM, :] * rl
            p = e * rl
            psum = p[:, 0:Q_BLOCK]
            for hh in range(1, hg):
                psum = psum + p[:, hh * Q_BLOCK:(hh + 1) * Q_BLOCK]
            psum_ref[8:8 + rows, :] = psum

        @pl.when(cc == 0)
        def _():
            psum_ref[...] = jnp.zeros(psum_ref.shape, F32)

        cuts = [nc * k // CMP_PATHS for k in range(1, CMP_PATHS + 1)] if nc % (128 * CMP_PATHS) == 0 else [nc]
        need = (c + 1) * (Q_BLOCK // CMP_STRIDE)
        for k, rows in enumerate(cuts):
            lo = cuts[k - 1] if k else 0
            pl.when((need > lo) & (need <= rows))(functools.partial(cmp_branch, rows))

        row0 = pl.multiple_of(jnp.maximum(c * Q_BLOCK - WIN, 0), Q_BLOCK)
        sw = _dot_3tiles(kwin_ref[0, pl.ds(row0, WIN_KEYS), :], qT)
        kp = row0 + lax.broadcasted_iota(jnp.int32, (WIN_KEYS, 1), 0)
        sw = jnp.where((kp <= t_q) & (kp > t_q - WIN), sw, -jnp.inf)
        sw_ref[...] = sw
        mw = jnp.max(jnp.max(sw.reshape(WIN_KEYS // 8, 8, width), axis=0), axis=0, keepdims=True)
        mw = jnp.where(mw == -jnp.inf, 0.0, mw)

        def select_blocks(rows):
            imp = (psum_ref[pl.ds(7, rows, stride=4), :] + psum_ref[pl.ds(11, rows, stride=4), :]
                   + 2.0 * (psum_ref[pl.ds(8, rows, stride=4), :] + psum_ref[pl.ds(9, rows, stride=4), :]
                            + psum_ref[pl.ds(10, rows, stride=4), :]))
            blk = lax.broadcasted_iota(jnp.int32, (rows, Q_BLOCK), 0)
            tb = (c * Q_BLOCK + lax.broadcasted_iota(jnp.int32, (1, Q_BLOCK), 1)) >> 6
            valid = blk <= tb
            forced = (blk == 0) | (blk == tb) | (blk == tb - 1)
            score = jnp.where(forced | ~valid, -jnp.inf, imp)
            blk_f = blk.astype(F32)

            def pick(_, sc):
                mx = jnp.max(sc, axis=0, keepdims=True)
                first = jnp.min(jnp.where(sc == mx, blk_f, float(rows)), axis=0, keepdims=True)
                return jnp.where(blk_f == first, -jnp.inf, sc)

            left = lax.fori_loop(0, N_SEL - 3, pick, score)
            sel = (left == -jnp.inf) & valid
            bias_ref[0:rows, :] = jnp.where(sel, 0.0, MASK_BIAS).astype(BF16)

        @pl.when(cc == 0)
        def _():
            bias_ref[...] = jnp.full(bias_ref.shape, MASK_BIAS, BF16)

        b_cuts = [nsb * k // SEL_PATHS for k in range(1, SEL_PATHS + 1)] if nsb % (16 * SEL_PATHS) == 0 else [nsb]
        need_b = 2 * c + 2
        for k, rows in enumerate(b_cuts):
            lo = b_cuts[k - 1] if k else 0
            pl.when((need_b > lo) & (need_b <= rows))(functools.partial(select_blocks, rows))

        acc_ref[...] = jnp.zeros(acc_ref.shape, F32)
        m_ref[...] = jnp.full(m_ref.shape, M_INIT, F32)

        def set_bias(j):
            span = j // (ONEHOT_SPAN // KEY_TILE)
            b16 = bias_ref[pl.ds(pl.multiple_of(span * SEL_PER_SPAN, SEL_PER_SPAN), SEL_PER_SPAN), :]
            for hh in range(hg):
                qaug_ref[HEAD_DIM:HEAD_DIM + SEL_PER_SPAN, hh * Q_BLOCK:(hh + 1) * Q_BLOCK] = b16

        def scores(j, r):
            k0 = pl.multiple_of(j * KEY_TILE + r * KEY_CHUNK, KEY_CHUNK)
            return _dot(ksel_ref[0, pl.ds(k0, KEY_CHUNK), :], qaug_ref[...])

        def chunk_max(cmax, sc):
            return jnp.maximum(cmax, jnp.max(sc.reshape(KEY_CHUNK // 8, 8, width), axis=0))

        def new_max(cmax):
            m_old = m_ref[...]
            m_new = jnp.maximum(m_old, jnp.max(cmax, axis=0, keepdims=True))
            m_ref[...] = m_new
            return m_new, jnp.exp2(m_old - m_new)

        def values(j, r):
            k0 = pl.multiple_of(j * KEY_TILE + r * KEY_CHUNK, KEY_CHUNK)
            return vselT_ref[0, :, pl.ds(k0, KEY_CHUNK)]

        cmax0 = jnp.full((8, width), MASK_BIAS, F32)
        set_bias(0)
        cmax = cmax0
        ow = jnp.zeros((VAUG_ROWS, width), F32)
        w_cuts = [WIN_KEYS * r // n_chunks // 128 * 128 for r in range(n_chunks)] + [WIN_KEYS]
        for r in range(n_chunks):
            sc = _dot_3tiles(ksel_ref[0, r * KEY_CHUNK:(r + 1) * KEY_CHUNK, :], qaug_ref[...])
            ws = slice(w_cuts[r], w_cuts[r + 1])
            pw = jnp.exp2(sw_ref[ws, :] - mw).astype(BF16)
            s_ref[r * KEY_CHUNK:(r + 1) * KEY_CHUNK, :] = sc
            cmax = chunk_max(cmax, sc)
            ow = ow + _dot(vwinT_ref[0, :, pl.ds(pl.multiple_of(row0 + w_cuts[r], 128), w_cuts[r + 1] - w_cuts[r])], pw)
        ow_ref[...] = ow[0:HEAD_DIM, :] / jnp.maximum(ow[HEAD_DIM:HEAD_DIM + 1, :], 1e-30)

        def pipe_step(j, cmax):
            m_new, alpha = new_max(cmax)
            set_bias(j + 1)
            acc = acc_ref[...] * alpha
            cnext = cmax0
            for r in range(n_chunks):
                rs = slice(r * KEY_CHUNK, (r + 1) * KEY_CHUNK)
                sc = scores(j + 1, r)
                pj = jnp.exp2(s_ref[rs, :] - m_new).astype(BF16)
                s_ref[rs, :] = sc
                cnext = chunk_max(cnext, sc)
                acc = acc + _dot(values(j, r), pj)
            acc_ref[...] = acc
            return cnext

        def finish():
            diag = pl.ds(pl.multiple_of((c % (KEY_TILE // Q_BLOCK)) * Q_BLOCK, Q_BLOCK), Q_BLOCK)
            ku = lax.broadcasted_iota(jnp.int32, (Q_BLOCK, 1), 0)
            s_ref[diag, :] = jnp.where(ku <= t_q - c * Q_BLOCK, s_ref[diag, :], MASK_BIAS)

            def finish_tile(rows):
                s_last = s_ref[0:rows, :]
                m_new, alpha = new_max(jnp.max(s_last.reshape(rows // 8, 8, width), axis=0))
                p_last = jnp.exp2(s_last - m_new).astype(BF16)
                acc_ref[...] = acc_ref[...] * alpha + _dot(
                    vselT_ref[0, :, pl.ds(pl.multiple_of(last * KEY_TILE, KEY_TILE), rows)], p_last)

            diag_chunk = (c % (KEY_TILE // Q_BLOCK)) // (KEY_CHUNK // Q_BLOCK)
            for k in range(n_chunks):
                pl.when(diag_chunk == k)(functools.partial(finish_tile, (k + 1) * KEY_CHUNK))
            o_cmp = oc_ref[...]
            o_win = ow_ref[...]
            o_slc = acc_ref[0:HEAD_DIM, :] / jnp.maximum(acc_ref[HEAD_DIM:HEAD_DIM + 1, :], 1e-30)

            heads = []
            for hh in range(hg):
                hs = slice(hh * Q_BLOCK, (hh + 1) * Q_BLOCK)
                heads.append(o_cmp[:, hs] * gT_ref[hh:hh + 1, qs] + o_slc[:, hs] * gT_ref[8 + hh:9 + hh, qs]
                             + o_win[:, hs] * gT_ref[16 + hh:17 + hh, qs])
            for pr in range(hg // 2):
                pair = jnp.concatenate([heads[2 * pr], heads[2 * pr + 1]], axis=0)
                out_ref[qs, pr * 128:(pr + 1) * 128] = pair.T.astype(BF16)

        return cmax, pipe_step, finish

    streams = [stream(st) for st in range(N_STREAMS)]

    def sweep_steps(j, cmaxes):
        return tuple(step(j, cm) for (_, step, _), cm in zip(streams, cmaxes))

    lax.fori_loop(0, last, sweep_steps, tuple(cm for cm, _, _ in streams))
    for _, _, finish in streams:
        finish()


def _nsa(qT, gT, kc, vcT, ksel, vselT, kwin, vwinT):
    s = ksel.shape[1]
    nc = kc.shape[1]
    nsb = s // SEL_L
    hg = HEADS_PER_GROUP
    ns = N_STREAMS
    qb = ns * Q_BLOCK
    width = hg * Q_BLOCK
    grp = lambda g, c: (g, 0, 0)
    once = pl.Buffered(1)
    return pl.pallas_call(
        _nsa_kernel,
        grid=(GROUPS, s // qb),
        in_specs=[
            pl.BlockSpec((hg, HEAD_DIM, qb), lambda g, c: (g, 0, c)),
            pl.BlockSpec((GATE_ROWS, qb), lambda g, c: (g, c)),
            pl.BlockSpec((1, nc, HEAD_DIM), grp, pipeline_mode=once),
            pl.BlockSpec((1, VAUG_ROWS, nc), lambda g, c: (GROUPS + g, 0, 0), pipeline_mode=once),
            pl.BlockSpec((1, s, 128), grp, pipeline_mode=once),
            pl.BlockSpec((1, VAUG_ROWS, s), grp, pipeline_mode=once),
            pl.BlockSpec((1, s, HEAD_DIM), grp, pipeline_mode=once),
            pl.BlockSpec((1, VAUG_ROWS, s), grp, pipeline_mode=once),
        ],
        out_specs=pl.BlockSpec((qb, hg * HEAD_DIM), lambda g, c: (c, g)),
        out_shape=jax.ShapeDtypeStruct((s, NSA_W), BF16),
        scratch_shapes=[
            pltpu.VMEM((ns, 128, width), BF16),
            pltpu.VMEM((ns, nc + 16, Q_BLOCK), F32),
            pltpu.VMEM((ns, nsb, Q_BLOCK), BF16),
            pltpu.VMEM((ns, VAUG_ROWS, width), F32),
            pltpu.VMEM((ns, 1, width), F32),
            pltpu.VMEM((ns, KEY_TILE, width), F32),
            pltpu.VMEM((ns, HEAD_DIM, width), F32),
            pltpu.VMEM((ns, WIN_KEYS, width), F32),
            pltpu.VMEM((ns, HEAD_DIM, width), F32),
        ],
        compiler_params=pltpu.CompilerParams(
            dimension_semantics=("arbitrary", "arbitrary"), vmem_limit_bytes=VMEM_LIMIT),
        name="nsa_attention",
    )(qT, gT, kc, vcT, ksel, vselT, kwin, vwinT)


def _block_diag_ones():
    idx = np.arange(MEM_W) // HEAD_DIM
    return jnp.asarray((idx[:, None] == idx[None, :]).astype(np.float32), BF16)


def _head_masks():
    idx = np.arange(MEM_W) // HEAD_DIM
    return jnp.asarray((idx[None, :] == np.arange(MEM_HEADS)[:, None]).astype(np.float32))


def _onehot_pattern():
    rows = np.arange(ONEHOT_SPAN)[:, None] // SEL_L
    lanes = np.arange(128)[None, :] - HEAD_DIM
    return jnp.asarray((rows == lanes).astype(np.float32))


def _gate_layout():
    src = np.full((GATE_PAD,), -1, np.int64)
    for g in range(GROUPS):
        for b in range(3):
            for hh in range(HEADS_PER_GROUP):
                src[g * GATE_ROWS + b * 8 + hh] = (g * HEADS_PER_GROUP + hh) * 3 + b
    return src


def kernel(x, mem, norm_mix_g, norm_mlp_g, mem_norm_g, w_mem_kv, mem_q_norm_g, mem_k_norm_g, w_out, w_mlp_in, w_mlp_out, a_w_in, a_b_glu, a_dw, a_dw_b, a_ln_g, a_ln_b, b_w_in, b_gate_b, b_q_norm_g, kv_norm_g, w_kv, k_norm_g, cmp_pe_k, cmp_pe_v, cmp_w1_k, cmp_w2_k, cmp_w1_v, cmp_w2_v):
    batch, s, _ = x.shape
    assert batch == 1 and s % ONEHOT_SPAN == 0 and s >= WIN_KEYS and (KEY_TILE // Q_BLOCK) % N_STREAMS == 0
    assert w_out.shape[0] == 2 and a_w_in.shape[0] == 1 and b_w_in.shape[0] == 1
    nc = s // CMP_STRIDE
    row = lambda v: v.reshape(1, -1)
    bd = _block_diag_ones()
    hm = _head_masks()

    mkT, mv = _memkv(mem[0], row(mem_norm_g), w_mem_kv.astype(BF16),
                     jnp.tile(mem_k_norm_g, (1, MEM_HEADS))[:, None, :], bd)

    conv0, memo0 = _layer0(
        x[0], row(norm_mix_g[0]), a_w_in[0].astype(BF16), row(a_b_glu[0]), a_dw[0], row(a_dw_b[0]),
        row(a_ln_g[0]), row(a_ln_b[0]), row(jnp.tile(mem_q_norm_g[0], MEM_HEADS)), mkT, mv, bd, hm)
    w_out_bf = w_out.astype(BF16)
    w_in_bf = w_mlp_in.astype(BF16)
    w_o_bf = w_mlp_out.astype(BF16)
    x1 = _post(x[0], conv0, memo0, w_out_bf[0, :CONV_W], w_out_bf[0, CONV_W:], row(norm_mlp_g[0]),
               w_in_bf[0], w_o_bf[0])

    src = _gate_layout()
    used = src >= 0
    w_gate = jnp.where(used[None, :], b_w_in[0][:, NSA_W + MEM_W + np.maximum(src, 0)], 0.0)
    b_gate = jnp.where(used, b_gate_b[0][np.maximum(src, 0)], 0.0)
    w_in1 = jnp.concatenate([b_w_in[0][:, :NSA_W + MEM_W], w_gate], axis=1).astype(BF16)
    q_gain = jnp.tile(b_q_norm_g[0], NSA_HEADS) * (QK_SCALE * LOG2E)
    kng = jnp.stack([jnp.tile(k_norm_g[1], GROUPS), jnp.tile(k_norm_g[2], GROUPS)])
    qT, gT, ksel, kwin, vselT, vwinT, kvc, memo1 = _layer1_proj(
        x1, row(kv_norm_g), row(norm_mix_g[1]), w_kv.astype(BF16), w_in1, row(b_gate), row(q_gain),
        row(jnp.tile(mem_q_norm_g[1], MEM_HEADS)), kng, mkT, mv, bd, hm, _onehot_pattern())

    t = kvc.reshape(s, 2 * GROUPS, HEAD_DIM).transpose(1, 0, 2).reshape(2 * GROUPS, nc, CMP_STRIDE * HEAD_DIM)
    w1 = jnp.stack([cmp_w1_k, cmp_w1_v]).astype(BF16)
    w2 = jnp.stack([cmp_w2_k, cmp_w2_v]).astype(BF16)
    pe = jnp.stack([cmp_pe_k.reshape(1, -1), cmp_pe_v.reshape(1, -1)])
    cmp_rows, cmp_cols = _compress(t, w1, w2, w2.transpose(0, 2, 1), pe, row(k_norm_g[0]),
                                   k_norm_g[0].reshape(-1, 1))

    nsa = _nsa(qT, gT, cmp_rows, cmp_cols, ksel, vselT, kwin, vwinT)
    x2 = _post(x1, nsa, memo1, w_out_bf[1, :NSA_W], w_out_bf[1, NSA_W:], row(norm_mlp_g[1]),
               w_in_bf[1], w_o_bf[1])
    return x2[None]
```

```python
import functools

import numpy as np
import jax
import jax.numpy as jnp
from jax import lax
from jax.experimental import pallas as pl
from jax.experimental.pallas import tpu as pltpu

F32 = jnp.float32
BF16 = jnp.bfloat16

D_MODEL = 1024
HEAD_DIM = 64
MEM_LEN = 256
MEM_HEADS = 4
MEM_W = MEM_HEADS * HEAD_DIM
CONV_W = D_MODEL - MEM_W
CONV_K = 31
NSA_HEADS = CONV_W // HEAD_DIM
NSA_W = NSA_HEADS * HEAD_DIM
GROUPS = 2
HEADS_PER_GROUP = NSA_HEADS // GROUPS
CMP_L = 32
CMP_STRIDE = 16
CMP_HID = 256
SEL_L = 64
N_SEL = 16
WIN = 512
Q_BLOCK = 128
D_FF = 4 * D_MODEL
GATE_PAD = 128
GATE_ROWS = GATE_PAD // GROUPS

ROW_TILE = 512
FF_CHUNK = 1024
KEY_TILE = 1024
ONEHOT_SPAN = 1024
CMP_PATHS = 4
SEL_PATHS = 4
N_STREAMS = 4
KEY_CHUNK = 256
SEL_PER_SPAN = ONEHOT_SPAN // SEL_L
WIN_KEYS = WIN + Q_BLOCK
CONV_HALO = 32
CONV_ROWS = 64
VAUG_ROWS = 80
VMEM_LIMIT = 56 * 1024 * 1024
MXU_WIDTH = 256

EPS = 1e-6
LN_EPS = 1e-5
LOG2E = 1.4426950408889634
QK_SCALE = HEAD_DIM ** -0.5
MASK_BIAS = -2.0 ** 126
M_INIT = -2.0 ** 100


def _dot(a, b):
    return jnp.dot(a, b, preferred_element_type=F32)


def _dot_3tiles(a, b):
    two = 2 * MXU_WIDTH
    half = a.shape[0] // 2
    return jnp.concatenate([
        _dot(a, b[:, 0:two]),
        jnp.concatenate([_dot(a[0:half], b[:, two:]), _dot(a[half:], b[:, two:])], axis=0),
    ], axis=1)


def _rms_scale(x):
    return lax.rsqrt(jnp.mean(x * x, axis=-1, keepdims=True) + EPS)


def _head_meansq(x, bd):
    x2 = x * x
    hi = x2.astype(BF16)
    lo = (x2 - hi.astype(F32)).astype(BF16)
    return (_dot(hi, bd) + _dot(lo, bd)) * (1.0 / HEAD_DIM)


def _mem_attention(qn, mkT, mv, hm_ref):
    out = jnp.zeros(qn.shape, F32)
    for h in range(MEM_HEADS):
        hm = hm_ref[h:h + 1, :]
        s = _dot((qn * hm).astype(BF16), mkT)
        e = jnp.exp2(s - jnp.max(s, axis=-1, keepdims=True))
        l = jnp.sum(e, axis=-1, keepdims=True)
        out = out + _dot(e.astype(BF16), mv) * (hm / l)
    return out


def _memkv_kernel(mem_ref, g_ref, w_ref, kg_ref, bd_ref, mkT_ref, mv_ref):
    m = mem_ref[...]
    mn = (m * _rms_scale(m) * g_ref[...]).astype(BF16)
    kv = _dot(mn, w_ref[0])
    k = kv[:, :MEM_W]
    kn = k * lax.rsqrt(_head_meansq(k, bd_ref[...]) + EPS) * kg_ref[0]
    mkT_ref[0] = (kn * (QK_SCALE * LOG2E)).T.astype(BF16)
    mv_ref[0] = kv[:, MEM_W:].astype(BF16)


def _memkv(mem, g, w, kg, bd):
    depth = w.shape[0]
    return pl.pallas_call(
        _memkv_kernel,
        grid=(depth,),
        in_specs=[
            pl.BlockSpec((MEM_LEN, D_MODEL), lambda l: (0, 0)),
            pl.BlockSpec((1, D_MODEL), lambda l: (0, 0)),
            pl.BlockSpec((1, D_MODEL, 2 * MEM_W), lambda l: (l, 0, 0)),
            pl.BlockSpec((1, 1, MEM_W), lambda l: (l, 0, 0)),
            pl.BlockSpec((MEM_W, MEM_W), lambda l: (0, 0)),
        ],
        out_specs=[
            pl.BlockSpec((1, MEM_W, MEM_LEN), lambda l: (l, 0, 0)),
            pl.BlockSpec((1, MEM_LEN, MEM_W), lambda l: (l, 0, 0)),
        ],
        out_shape=[
            jax.ShapeDtypeStruct((depth, MEM_W, MEM_LEN), BF16),
            jax.ShapeDtypeStruct((depth, MEM_LEN, MEM_W), BF16),
        ],
        name="mem_kv",
    )(mem, g, w, kg, bd)


def _layer0_kernel(x_ref, g_ref, win_ref, bglu_ref, dw_ref, dwb_ref, lng_ref, lnb_ref, qg_ref,
                   mkT_ref, mv_ref, bd_ref, hm_ref, conv_ref, memo_ref, buf_ref, cv_ref, sh_ref):
    i = pl.program_id(0)
    tm = x_ref.shape[0]
    x = x_ref[...]
    h = (x * _rms_scale(x) * g_ref[...]).astype(BF16)
    u = _dot(h, win_ref[...])
    a = u[:, :CONV_W] + bglu_ref[:, :CONV_W]
    gate = u[:, CONV_W:2 * CONV_W] + bglu_ref[:, CONV_W:]
    v = a * jax.nn.sigmoid(gate)

    @pl.when(i == 0)
    def _():
        buf_ref[0:CONV_HALO, :] = jnp.zeros((CONV_HALO, CONV_W), F32)

    buf_ref[CONV_HALO:CONV_HALO + tm, :] = v

    base = CONV_HALO - (CONV_K - 1)
    rows = tm + CONV_HALO
    for cb in range(CONV_W // 128):
        cs = slice(cb * 128, (cb + 1) * 128)
        xb = buf_ref[:, cs]
        sh_ref[0] = xb
        for b in range(1, 8):
            sh_ref[b] = pltpu.roll(xb, rows - b, axis=0)

        def conv_rows(r, carry, cs=cs):
            r0 = pl.multiple_of(r * CONV_ROWS, CONV_ROWS)
            acc = jnp.zeros((CONV_ROWS, 128), F32) + dwb_ref[:, cs]
            for k in range(CONV_K):
                a, b = divmod(base + k, 8)
                acc = acc + dw_ref[k:k + 1, cs] * sh_ref[b, pl.ds(r0 + 8 * a, CONV_ROWS), :]
            cv_ref[pl.ds(r0, CONV_ROWS), cs] = acc
            return carry

        lax.fori_loop(0, tm // CONV_ROWS, conv_rows, 0)
    buf_ref[0:CONV_HALO, :] = buf_ref[tm:tm + CONV_HALO, :]

    cv = cv_ref[...]
    mu = jnp.mean(cv, axis=-1, keepdims=True)
    d = cv - mu
    var = jnp.mean(d * d, axis=-1, keepdims=True)
    y = d * lax.rsqrt(var + LN_EPS) * lng_ref[...] + lnb_ref[...]
    conv_ref[...] = (y * jax.nn.sigmoid(y)).astype(BF16)

    qm = u[:, 2 * CONV_W:]
    qn = qm * lax.rsqrt(_head_meansq(qm, bd_ref[...]) + EPS) * qg_ref[...]
    memo_ref[...] = _mem_attention(qn, mkT_ref[0], mv_ref[0], hm_ref).astype(BF16)


def _layer0(x, g, w_in, bglu, dw, dwb, lng, lnb, qg, mkT, mv, bd, hm):
    s = x.shape[0]
    tm = ROW_TILE
    const = lambda i: (0, 0)
    return pl.pallas_call(
        _layer0_kernel,
        grid=(s // tm,),
        in_specs=[
            pl.BlockSpec((tm, D_MODEL), lambda i: (i, 0)),
            pl.BlockSpec((1, D_MODEL), const),
            pl.BlockSpec(w_in.shape, const),
            pl.BlockSpec((1, 2 * CONV_W), const),
            pl.BlockSpec((CONV_K, CONV_W), const),
            pl.BlockSpec((1, CONV_W), const),
            pl.BlockSpec((1, CONV_W), const),
            pl.BlockSpec((1, CONV_W), const),
            pl.BlockSpec((1, MEM_W), const),
            pl.BlockSpec((1, MEM_W, MEM_LEN), lambda i: (0, 0, 0)),
            pl.BlockSpec((1, MEM_LEN, MEM_W), lambda i: (0, 0, 0)),
            pl.BlockSpec((MEM_W, MEM_W), const),
            pl.BlockSpec((MEM_HEADS, MEM_W), const),
        ],
        out_specs=[
            pl.BlockSpec((tm, CONV_W), lambda i: (i, 0)),
            pl.BlockSpec((tm, MEM_W), lambda i: (i, 0)),
        ],
        out_shape=[
            jax.ShapeDtypeStruct((s, CONV_W), BF16),
            jax.ShapeDtypeStruct((s, MEM_W), BF16),
        ],
        scratch_shapes=[
            pltpu.VMEM((tm + CONV_HALO, CONV_W), F32),
            pltpu.VMEM((tm, CONV_W), F32),
            pltpu.VMEM((8, tm + CONV_HALO, 128), F32),
        ],
        compiler_params=pltpu.CompilerParams(
            dimension_semantics=("arbitrary",), vmem_limit_bytes=VMEM_LIMIT),
        name="layer0_mixer",
    )(x, g, w_in, bglu, dw, dwb, lng, lnb, qg, mkT, mv, bd, hm)


def _post_kernel(x_ref, ma_ref, mb_ref, wa_ref, wb_ref, g_ref, win_ref, wout_ref, o_ref):
    o_ref[...] = x_ref[...] + _dot(ma_ref[...], wa_ref[...]) + _dot(mb_ref[...], wb_ref[...])
    x1 = o_ref[...]
    h = (x1 * _rms_scale(x1) * g_ref[...]).astype(BF16)
    for c in range(D_FF // FF_CHUNK):
        cs = slice(c * FF_CHUNK, (c + 1) * FF_CHUNK)
        t = jnp.maximum(_dot(h, win_ref[:, cs]), 0.0)
        o_ref[...] += _dot((t * t).astype(BF16), wout_ref[cs, :])


def _post(x, ma, mb, wa, wb, g, w_in, w_out):
    s = x.shape[0]
    tm = ROW_TILE
    const = lambda i: (0, 0)
    single = pl.Buffered(1)
    return pl.pallas_call(
        _post_kernel,
        grid=(s // tm,),
        in_specs=[
            pl.BlockSpec((tm, D_MODEL), lambda i: (i, 0)),
            pl.BlockSpec((tm, CONV_W), lambda i: (i, 0)),
            pl.BlockSpec((tm, MEM_W), lambda i: (i, 0)),
            pl.BlockSpec((CONV_W, D_MODEL), const, pipeline_mode=single),
            pl.BlockSpec((MEM_W, D_MODEL), const, pipeline_mode=single),
            pl.BlockSpec((1, D_MODEL), const),
            pl.BlockSpec((D_MODEL, D_FF), const, pipeline_mode=single),
            pl.BlockSpec((D_FF, D_MODEL), const, pipeline_mode=single),
        ],
        out_specs=pl.BlockSpec((tm, D_MODEL), lambda i: (i, 0)),
        out_shape=jax.ShapeDtypeStruct((s, D_MODEL), F32),
        compiler_params=pltpu.CompilerParams(
            dimension_semantics=("arbitrary",), vmem_limit_bytes=VMEM_LIMIT),
        name="outproj_mlp",
    )(x, ma, mb, wa, wb, g, w_in, w_out)


def _layer1_proj_kernel(x_ref, gkv_ref, gmix_ref, wkv_ref, win_ref, gb_ref, qg_ref, mqg_ref, kng_ref,
                        mkT_ref, mv_ref, bd_ref, hm_ref, pat_ref,
                        qT_ref, gT_ref, ksel_ref, kwin_ref, vselT_ref, vwinT_ref, kvc_ref, memo_ref):
    tm = x_ref.shape[0]
    x = x_ref[...]
    xn = x * _rms_scale(x)
    bd = bd_ref[...]

    kv = _dot((xn * gkv_ref[...]).astype(BF16), wkv_ref[...])
    kvc_ref[...] = kv[:, 0:256].astype(BF16)
    k2 = kv[:, 256:384]
    v2 = kv[:, 384:512]
    kw = kv[:, 512:640]
    vw = kv[:, 640:768]
    bd2 = bd[0:128, 0:128]
    k2n = k2 * lax.rsqrt(_head_meansq(k2, bd2) + EPS) * kng_ref[0:1, :]
    kwn = kw * lax.rsqrt(_head_meansq(kw, bd2) + EPS) * kng_ref[1:2, :]

    lane = lax.broadcasted_iota(jnp.int32, (tm, 128), 1)
    pat = pat_ref[...]
    ksel_ref[0] = jnp.where(lane < HEAD_DIM, k2n, pat).astype(BF16)
    ksel_ref[1] = jnp.where(lane < HEAD_DIM, pltpu.roll(k2n, HEAD_DIM, axis=1), pat).astype(BF16)
    kwin_ref[0] = kwn[:, :HEAD_DIM].astype(BF16)
    kwin_ref[1] = kwn[:, HEAD_DIM:].astype(BF16)

    ones_rows = (lax.broadcasted_iota(jnp.int32, (VAUG_ROWS - HEAD_DIM, tm), 0) == 0).astype(BF16)
    v2T = v2.T.astype(BF16)
    vwT = vw.T.astype(BF16)
    for g in range(GROUPS):
        vselT_ref[g, 0:HEAD_DIM, :] = v2T[g * HEAD_DIM:(g + 1) * HEAD_DIM, :]
        vselT_ref[g, HEAD_DIM:VAUG_ROWS, :] = ones_rows
        vwinT_ref[g, 0:HEAD_DIM, :] = vwT[g * HEAD_DIM:(g + 1) * HEAD_DIM, :]
        vwinT_ref[g, HEAD_DIM:VAUG_ROWS, :] = ones_rows

    u = _dot((xn * gmix_ref[...]).astype(BF16), win_ref[...])
    for cb in range(NSA_W // 256):
        cs = slice(cb * 256, (cb + 1) * 256)
        qc = u[:, cs]
        qn = qc * lax.rsqrt(_head_meansq(qc, bd) + EPS) * qg_ref[:, cs]
        qT_ref[4 * cb:4 * cb + 4] = qn.T.astype(BF16).reshape(4, HEAD_DIM, tm)

    gates = jax.nn.sigmoid(u[:, NSA_W + MEM_W:] + gb_ref[...])
    gT_ref[...] = gates.T

    qm = u[:, NSA_W:NSA_W + MEM_W]
    qmn = qm * lax.rsqrt(_head_meansq(qm, bd) + EPS) * mqg_ref[...]
    memo_ref[...] = _mem_attention(qmn, mkT_ref[0], mv_ref[0], hm_ref).astype(BF16)


def _layer1_proj(x, gkv, gmix, wkv, w_in, gb, qg, mqg, kng, mkT, mv, bd, hm, pat):
    s = x.shape[0]
    tm = ROW_TILE
    const = lambda i: (0, 0)
    pat_blocks = pat.shape[0] // tm
    return pl.pallas_call(
        _layer1_proj_kernel,
        grid=(s // tm,),
        in_specs=[
            pl.BlockSpec((tm, D_MODEL), lambda i: (i, 0)),
            pl.BlockSpec((1, D_MODEL), const),
            pl.BlockSpec((1, D_MODEL), const),
            pl.BlockSpec(wkv.shape, const),
            pl.BlockSpec(w_in.shape, const),
            pl.BlockSpec((1, GATE_PAD), const),
            pl.BlockSpec((1, NSA_W), const),
            pl.BlockSpec((1, MEM_W), const),
            pl.BlockSpec((2, 128), const),
            pl.BlockSpec((1, MEM_W, MEM_LEN), lambda i: (1, 0, 0)),
            pl.BlockSpec((1, MEM_LEN, MEM_W), lambda i: (1, 0, 0)),
            pl.BlockSpec((MEM_W, MEM_W), const),
            pl.BlockSpec((MEM_HEADS, MEM_W), const),
            pl.BlockSpec((tm, 128), lambda i: (i % pat_blocks, 0)),
        ],
        out_specs=[
            pl.BlockSpec((NSA_HEADS, HEAD_DIM, tm), lambda i: (0, 0, i)),
            pl.BlockSpec((GATE_PAD, tm), lambda i: (0, i)),
            pl.BlockSpec((GROUPS, tm, 128), lambda i: (0, i, 0)),
            pl.BlockSpec((GROUPS, tm, HEAD_DIM), lambda i: (0, i, 0)),
            pl.BlockSpec((GROUPS, VAUG_ROWS, tm), lambda i: (0, 0, i)),
            pl.BlockSpec((GROUPS, VAUG_ROWS, tm), lambda i: (0, 0, i)),
            pl.BlockSpec((tm, 256), lambda i: (i, 0)),
            pl.BlockSpec((tm, MEM_W), lambda i: (i, 0)),
        ],
        out_shape=[
            jax.ShapeDtypeStruct((NSA_HEADS, HEAD_DIM, s), BF16),
            jax.ShapeDtypeStruct((GATE_PAD, s), F32),
            jax.ShapeDtypeStruct((GROUPS, s, 128), BF16),
            jax.ShapeDtypeStruct((GROUPS, s, HEAD_DIM), BF16),
            jax.ShapeDtypeStruct((GROUPS, VAUG_ROWS, s), BF16),
            jax.ShapeDtypeStruct((GROUPS, VAUG_ROWS, s), BF16),
            jax.ShapeDtypeStruct((s, 256), BF16),
            jax.ShapeDtypeStruct((s, MEM_W), BF16),
        ],
        compiler_params=pltpu.CompilerParams(
            dimension_semantics=("arbitrary",), vmem_limit_bytes=VMEM_LIMIT),
        name="layer1_proj",
    )(x, gkv, gmix, wkv, w_in, gb, qg, mqg, kng, mkT, mv, bd, hm, pat)


def _compress_kernel(t_ref, w1_ref, w2_ref, w2T_ref, pe_ref, kg_ref, kgT_ref, o_ref, oT_ref, hb_ref):
    nc = t_ref.shape[1]
    half = CMP_STRIDE * HEAD_DIM
    t = t_ref[0]
    ha = _dot(t, w1_ref[0, 0:half, :])
    hb_ref[0:nc, :] = _dot(t, w1_ref[0, half:2 * half, :])
    hb_ref[nc:nc + 8, :] = jnp.zeros((8, CMP_HID), F32)
    pe = jnp.broadcast_to(pe_ref[0], (8, 2 * half)).astype(BF16)
    pe_term = _dot(pe, w1_ref[0])[0:1, :]
    h = ha + hb_ref[1:nc + 1, :] + pe_term
    hg = 0.5 * h * (1.0 + jnp.tanh(0.7978845608028654 * (h + 0.044715 * (h * h * h))))
    hg = hg.astype(BF16)
    o = _dot(hg, w2_ref[0])
    oT = lax.dot_general(w2T_ref[0], hg, (((1,), (1,)), ((), ())), preferred_element_type=F32)
    is_k = pl.program_id(0) < GROUPS
    on = o * lax.rsqrt(jnp.mean(o * o, axis=1, keepdims=True) + EPS) * kg_ref[...]
    oTn = oT * lax.rsqrt(jnp.mean(oT * oT, axis=0, keepdims=True) + EPS) * kgT_ref[...]
    o_ref[0] = jnp.where(is_k, on, o).astype(BF16)
    oT_ref[0, 0:HEAD_DIM, :] = jnp.where(is_k, oTn, oT).astype(BF16)
    oT_ref[0, HEAD_DIM:VAUG_ROWS, :] = (
        lax.broadcasted_iota(jnp.int32, (VAUG_ROWS - HEAD_DIM, nc), 0) == 0).astype(BF16)


def _compress(t, w1, w2, w2T, pe, kg, kgT):
    n, nc, width = t.shape
    return pl.pallas_call(
        _compress_kernel,
        grid=(n,),
        in_specs=[
            pl.BlockSpec((1, nc, width), lambda i: (i, 0, 0)),
            pl.BlockSpec((1, 2 * width, CMP_HID), lambda i: (i // GROUPS, 0, 0)),
            pl.BlockSpec((1, CMP_HID, HEAD_DIM), lambda i: (i // GROUPS, 0, 0)),
            pl.BlockSpec((1, HEAD_DIM, CMP_HID), lambda i: (i // GROUPS, 0, 0)),
            pl.BlockSpec((1, 1, 2 * width), lambda i: (i // GROUPS, 0, 0)),
            pl.BlockSpec((1, HEAD_DIM), lambda i: (0, 0)),
            pl.BlockSpec((HEAD_DIM, 1), lambda i: (0, 0)),
        ],
        out_specs=[
            pl.BlockSpec((1, nc, HEAD_DIM), lambda i: (i, 0, 0)),
            pl.BlockSpec((1, VAUG_ROWS, nc), lambda i: (i, 0, 0)),
        ],
        out_shape=[
            jax.ShapeDtypeStruct((n, nc, HEAD_DIM), BF16),
            jax.ShapeDtypeStruct((n, VAUG_ROWS, nc), BF16),
        ],
        scratch_shapes=[pltpu.VMEM((nc + 8, CMP_HID), F32)],
        compiler_params=pltpu.CompilerParams(
            dimension_semantics=("arbitrary",), vmem_limit_bytes=VMEM_LIMIT),
        name="compress_kv",
    )(t, w1, w2, w2T, pe, kg, kgT)


def _nsa_kernel(qT_ref, gT_ref, kc_ref, vcT_ref, ksel_ref, vselT_ref, kwin_ref, vwinT_ref, out_ref,
                qaug_st, psum_st, bias_st, acc_st, m_st, s_st, oc_st, sw_st, ow_st, pick_st):
    cc = pl.program_id(1)
    nc = kc_ref.shape[1]
    nsb = bias_st.shape[1]
    hg = HEADS_PER_GROUP
    width = hg * Q_BLOCK
    n_chunks = KEY_TILE // KEY_CHUNK
    last = (cc * N_STREAMS) // (KEY_TILE // Q_BLOCK)

    def load_queries(st):
        qaug_ref = qaug_st.at[st]
        for hh in range(hg):
            qaug_ref[0:HEAD_DIM, hh * Q_BLOCK:(hh + 1) * Q_BLOCK] = qT_ref[hh, :, st * Q_BLOCK:(st + 1) * Q_BLOCK]
        qaug_ref[HEAD_DIM:128, :] = jnp.zeros((128 - HEAD_DIM, width), BF16)

    def query_pos(st):
        return (cc * N_STREAMS + st) * Q_BLOCK + (lax.broadcasted_iota(jnp.int32, (1, width), 1) & (Q_BLOCK - 1))

    for st in range(N_STREAMS):
        load_queries(st)

    def cmp_branch(rows):
        for st in range(N_STREAMS):
            qT = qaug_st[st, 0:HEAD_DIM, :]
            t_q = query_pos(st)
            oc_ref, psum_ref = oc_st.at[st], psum_st.at[st]
            s = _dot_3tiles(kc_ref[0, 0:rows, :], qT)
            ci = lax.broadcasted_iota(jnp.int32, (rows, 1), 0)
            s = jnp.where(ci * CMP_STRIDE + (CMP_L - 1) <= t_q, s, -jnp.inf)
            m = jnp.max(s, axis=0, keepdims=True)
            m = jnp.where(m == -jnp.inf, 0.0, m)
            e = jnp.exp2(s - m)
            oa = _dot(vcT_ref[0, :, 0:rows], e.astype(BF16))
            rl = 1.0 / jnp.maximum(oa[HEAD_DIM:HEAD_DIM + 1, :], 1e-30)
            oc_ref[...] = oa[0:HEAD_DIM, :] * rl
            p = e * rl
            psum = p[:, 0:Q_BLOCK]
            for hh in range(1, hg):
                psum = psum + p[:, hh * Q_BLOCK:(hh + 1) * Q_BLOCK]
            psum_ref[8:8 + rows, :] = psum

    @pl.when(cc == 0)
    def _():
        psum_st[...] = jnp.zeros(psum_st.shape, F32)

    cuts = [nc * k // CMP_PATHS for k in range(1, CMP_PATHS + 1)] if nc % (128 * CMP_PATHS) == 0 else [nc]
    need = (cc + 1) * N_STREAMS * (Q_BLOCK // CMP_STRIDE)
    for k, rows in enumerate(cuts):
        lo = cuts[k - 1] if k else 0
        pl.when((need > lo) & (need <= rows))(functools.partial(cmp_branch, rows))

    def block_validity(st, rows):
        blk = lax.broadcasted_iota(jnp.int32, (rows, Q_BLOCK), 0)
        tb = ((cc * N_STREAMS + st) * Q_BLOCK + lax.broadcasted_iota(jnp.int32, (1, Q_BLOCK), 1)) >> 6
        return blk, tb, blk <= tb

    def select_blocks(rows):
        for st in range(N_STREAMS):
            psum_ref = psum_st.at[st]
            imp = (psum_ref[pl.ds(7, rows, stride=4), :] + psum_ref[pl.ds(11, rows, stride=4), :]
                   + 2.0 * (psum_ref[pl.ds(8, rows, stride=4), :] + psum_ref[pl.ds(9, rows, stride=4), :]
                            + psum_ref[pl.ds(10, rows, stride=4), :]))
            blk, tb, valid = block_validity(st, rows)
            forced = (blk == 0) | (blk == tb) | (blk == tb - 1)
            pick_st[st, 0:rows, :] = jnp.where(forced | ~valid, -jnp.inf, imp)
        blk_f = lax.broadcasted_iota(jnp.int32, (rows, Q_BLOCK), 0).astype(F32)

        def pick(_, carry):
            for st in range(N_STREAMS):
                sc = pick_st[st, 0:rows, :]
                mx = jnp.max(sc, axis=0, keepdims=True)
                first = jnp.min(jnp.where(sc == mx, blk_f, float(rows)), axis=0, keepdims=True)
                pick_st[st, 0:rows, :] = jnp.where(blk_f == first, -jnp.inf, sc)
            return carry

        lax.fori_loop(0, N_SEL - 3, pick, 0)
        for st in range(N_STREAMS):
            sel = (pick_st[st, 0:rows, :] == -jnp.inf) & block_validity(st, rows)[2]
            bias_st[st, 0:rows, :] = jnp.where(sel, 0.0, MASK_BIAS).astype(BF16)

    @pl.when(cc == 0)
    def _():
        bias_st[...] = jnp.full(bias_st.shape, MASK_BIAS, BF16)

    b_cuts = [nsb * k // SEL_PATHS for k in range(1, SEL_PATHS + 1)] if nsb % (16 * SEL_PATHS) == 0 else [nsb]
    need_b = 2 * (cc + 1) * N_STREAMS
    for k, rows in enumerate(b_cuts):
        lo = b_cuts[k - 1] if k else 0
        pl.when((need_b > lo) & (need_b <= rows))(functools.partial(select_blocks, rows))

    def stream(st):
        c = cc * N_STREAMS + st
        qs = slice(st * Q_BLOCK, (st + 1) * Q_BLOCK)
        qaug_ref, psum_ref, bias_ref, acc_ref = qaug_st.at[st], psum_st.at[st], bias_st.at[st], acc_st.at[st]
        m_ref, s_ref, oc_ref, sw_ref, ow_ref = m_st.at[st], s_st.at[st], oc_st.at[st], sw_st.at[st], ow_st.at[st]
        qT = qaug_ref[0:HEAD_DIM, :]
        t_q = query_pos(st)

        row0 = pl.multiple_of(jnp.maximum(c * Q_BLOCK - WIN, 0), Q_BLOCK)
        sw = _dot_3tiles(kwin_ref[0, pl.ds(row0, WIN_KEYS), :], qT)
        kp = row0 + lax.broadcasted_iota(jnp.int32, (WIN_KEYS, 1), 0)
        sw = jnp.where((kp <= t_q) & (kp > t_q - WIN), sw, -jnp.inf)
        sw_ref[...] = sw
        mw = jnp.max(jnp.max(sw.reshape(WIN_KEYS // 8, 8, width), axis=0), axis=0, keepdims=True)
        mw = jnp.where(mw == -jnp.inf, 0.0, mw)

        acc_ref[...] = jnp.zeros(acc_ref.shape, F32)
        m_ref[...] = jnp.full(m_ref.shape, M_INIT, F32)

        def set_bias(j):
            span = j // (ONEHOT_SPAN // KEY_TILE)
            b16 = bias_ref[pl.ds(pl.multiple_of(span * SEL_PER_SPAN, SEL_PER_SPAN), SEL_PER_SPAN), :]
            for hh in range(hg):
                qaug_ref[HEAD_DIM:HEAD_DIM + SEL_PER_SPAN, hh * Q_BLOCK:(hh + 1) * Q_BLOCK] = b16

        def scores(j, r):
            k0 = pl.multiple_of(j * KEY_TILE + r * KEY_CHUNK, KEY_CHUNK)
            return _dot(ksel_ref[0, pl.ds(k0, KEY_CHUNK), :], qaug_ref[...])

        def chunk_max(cmax, sc):
            return jnp.maximum(cmax, jnp.max(sc.reshape(KEY_CHUNK // 8, 8, width), axis=0))

        def new_max(cmax):
            m_old = m_ref[...]
            m_new = jnp.maximum(m_old, jnp.max(cmax, axis=0, keepdims=True))
            m_ref[...] = m_new
            return m_new, jnp.exp2(m_old - m_new)

        def values(j, r):
            k0 = pl.multiple_of(j * KEY_TILE + r * KEY_CHUNK, KEY_CHUNK)
            return vselT_ref[0, :, pl.ds(k0, KEY_CHUNK)]

        cmax0 = jnp.full((8, width), MASK_BIAS, F32)
        set_bias(0)
        cmax = cmax0
        ow = jnp.zeros((VAUG_ROWS, width), F32)
        w_cuts = [WIN_KEYS * r // n_chunks // 128 * 128 for r in range(n_chunks)] + [WIN_KEYS]
        for r in range(n_chunks):
            sc = _dot_3tiles(ksel_ref[0, r * KEY_CHUNK:(r + 1) * KEY_CHUNK, :], qaug_ref[...])
            ws = slice(w_cuts[r], w_cuts[r + 1])
            pw = jnp.exp2(sw_ref[ws, :] - mw).astype(BF16)
            s_ref[r * KEY_CHUNK:(r + 1) * KEY_CHUNK, :] = sc
            cmax = chunk_max(cmax, sc)
            ow = ow + _dot(vwinT_ref[0, :, pl.ds(pl.multiple_of(row0 + w_cuts[r], 128), w_cuts[r + 1] - w_cuts[r])], pw)
        ow_ref[...] = ow[0:HEAD_DIM, :] / jnp.maximum(ow[HEAD_DIM:HEAD_DIM + 1, :], 1e-30)

        def pipe_step(j, cmax):
            m_new, alpha = new_max(cmax)
            set_bias(j + 1)
            acc = acc_ref[...] * alpha
            cnext = cmax0
            for r in range(n_chunks):
                rs = slice(r * KEY_CHUNK, (r + 1) * KEY_CHUNK)
                sc = scores(j + 1, r)
                pj = jnp.exp2(s_ref[rs, :] - m_new).astype(BF16)
                s_ref[rs, :] = sc
                cnext = chunk_max(cnext, sc)
                acc = acc + _dot(values(j, r), pj)
            acc_ref[...] = acc
            return cnext

        def finish():
            diag = pl.ds(pl.multiple_of((c % (KEY_TILE // Q_BLOCK)) * Q_BLOCK, Q_BLOCK), Q_BLOCK)
            ku = lax.broadcasted_iota(jnp.int32, (Q_BLOCK, 1), 0)
            s_ref[diag, :] = jnp.where(ku <= t_q - c * Q_BLOCK, s_ref[diag, :], MASK_BIAS)

            def finish_tile(rows):
                s_last = s_ref[0:rows, :]
                m_new, alpha = new_max(jnp.max(s_last.reshape(rows // 8, 8, width), axis=0))
                p_last = jnp.exp2(s_last - m_new).astype(BF16)
                acc_ref[...] = acc_ref[...] * alpha + _dot(
                    vselT_ref[0, :, pl.ds(pl.multiple_of(last * KEY_TILE, KEY_TILE), rows)], p_last)

            diag_chunk = (c % (KEY_TILE // Q_BLOCK)) // (KEY_CHUNK // Q_BLOCK)
            for k in range(n_chunks):
                pl.when(diag_chunk == k)(functools.partial(finish_tile, (k + 1) * KEY_CHUNK))
            o_cmp = oc_ref[...]
            o_win = ow_ref[...]
            o_slc = acc_ref[0:HEAD_DIM, :] / jnp.maximum(acc_ref[HEAD_DIM:HEAD_DIM + 1, :], 1e-30)

            heads = []
            for hh in range(hg):
                hs = slice(hh * Q_BLOCK, (hh + 1) * Q_BLOCK)
                heads.append(o_cmp[:, hs] * gT_ref[hh:hh + 1, qs] + o_slc[:, hs] * gT_ref[8 + hh:9 + hh, qs]
                             + o_win[:, hs] * gT_ref[16 + hh:17 + hh, qs])
            for pr in range(hg // 2):
                pair = jnp.concatenate([heads[2 * pr], heads[2 * pr + 1]], axis=0)
                out_ref[qs, pr * 128:(pr + 1) * 128] = pair.T.astype(BF16)

        return cmax, pipe_step, finish

    streams = [stream(st) for st in range(N_STREAMS)]

    def sweep_steps(j, cmaxes):
        return tuple(step(j, cm) for (_, step, _), cm in zip(streams, cmaxes))

    lax.fori_loop(0, last, sweep_steps, tuple(cm for cm, _, _ in streams))
    for _, _, finish in streams:
        finish()


def _nsa(qT, gT, kc, vcT, ksel, vselT, kwin, vwinT):
    s = ksel.shape[1]
    nc = kc.shape[1]
    nsb = s // SEL_L
    hg = HEADS_PER_GROUP
    ns = N_STREAMS
    qb = ns * Q_BLOCK
    width = hg * Q_BLOCK
    grp = lambda g, c: (g, 0, 0)
    once = pl.Buffered(1)
    return pl.pallas_call(
        _nsa_kernel,
        grid=(GROUPS, s // qb),
        in_specs=[
            pl.BlockSpec((hg, HEAD_DIM, qb), lambda g, c: (g, 0, c)),
            pl.BlockSpec((GATE_ROWS, qb), lambda g, c: (g, c)),
            pl.BlockSpec((1, nc, HEAD_DIM), grp, pipeline_mode=once),
            pl.BlockSpec((1, VAUG_ROWS, nc), lambda g, c: (GROUPS + g, 0, 0), pipeline_mode=once),
            pl.BlockSpec((1, s, 128), grp, pipeline_mode=once),
            pl.BlockSpec((1, VAUG_ROWS, s), grp, pipeline_mode=once),
            pl.BlockSpec((1, s, HEAD_DIM), grp, pipeline_mode=once),
            pl.BlockSpec((1, VAUG_ROWS, s), grp, pipeline_mode=once),
        ],
        out_specs=pl.BlockSpec((qb, hg * HEAD_DIM), lambda g, c: (c, g)),
        out_shape=jax.ShapeDtypeStruct((s, NSA_W), BF16),
        scratch_shapes=[
            pltpu.VMEM((ns, 128, width), BF16),
            pltpu.VMEM((ns, nc + 16, Q_BLOCK), F32),
            pltpu.VMEM((ns, nsb, Q_BLOCK), BF16),
            pltpu.VMEM((ns, VAUG_ROWS, width), F32),
            pltpu.VMEM((ns, 1, width), F32),
            pltpu.VMEM((ns, KEY_TILE, width), F32),
            pltpu.VMEM((ns, HEAD_DIM, width), F32),
            pltpu.VMEM((ns, WIN_KEYS, width), F32),
            pltpu.VMEM((ns, HEAD_DIM, width), F32),
            pltpu.VMEM((ns, nsb, Q_BLOCK), F32),
        ],
        compiler_params=pltpu.CompilerParams(
            dimension_semantics=("arbitrary", "arbitrary"), vmem_limit_bytes=VMEM_LIMIT),
        name="nsa_attention",
    )(qT, gT, kc, vcT, ksel, vselT, kwin, vwinT)


def _block_diag_ones():
    idx = np.arange(MEM_W) // HEAD_DIM
    return jnp.asarray((idx[:, None] == idx[None, :]).astype(np.float32), BF16)


def _head_masks():
    idx = np.arange(MEM_W) // HEAD_DIM
    return jnp.asarray((idx[None, :] == np.arange(MEM_HEADS)[:, None]).astype(np.float32))


def _onehot_pattern():
    rows = np.arange(ONEHOT_SPAN)[:, None] // SEL_L
    lanes = np.arange(128)[None, :] - HEAD_DIM
    return jnp.asarray((rows == lanes).astype(np.float32))


def _gate_layout():
    src = np.full((GATE_PAD,), -1, np.int64)
    for g in range(GROUPS):
        for b in range(3):
            for hh in range(HEADS_PER_GROUP):
                src[g * GATE_ROWS + b * 8 + hh] = (g * HEADS_PER_GROUP + hh) * 3 + b
    return src


def kernel(x, mem, norm_mix_g, norm_mlp_g, mem_norm_g, w_mem_kv, mem_q_norm_g, mem_k_norm_g, w_out, w_mlp_in, w_mlp_out, a_w_in, a_b_glu, a_dw, a_dw_b, a_ln_g, a_ln_b, b_w_in, b_gate_b, b_q_norm_g, kv_norm_g, w_kv, k_norm_g, cmp_pe_k, cmp_pe_v, cmp_w1_k, cmp_w2_k, cmp_w1_v, cmp_w2_v):
    batch, s, _ = x.shape
    assert batch == 1 and s % ONEHOT_SPAN == 0 and s >= WIN_KEYS and (KEY_TILE // Q_BLOCK) % N_STREAMS == 0
    assert w_out.shape[0] == 2 and a_w_in.shape[0] == 1 and b_w_in.shape[0] == 1
    nc = s // CMP_STRIDE
    row = lambda v: v.reshape(1, -1)
    bd = _block_diag_ones()
    hm = _head_masks()

    mkT, mv = _memkv(mem[0], row(mem_norm_g), w_mem_kv.astype(BF16),
                     jnp.tile(mem_k_norm_g, (1, MEM_HEADS))[:, None, :], bd)

    conv0, memo0 = _layer0(
        x[0], row(norm_mix_g[0]), a_w_in[0].astype(BF16), row(a_b_glu[0]), a_dw[0], row(a_dw_b[0]),
        row(a_ln_g[0]), row(a_ln_b[0]), row(jnp.tile(mem_q_norm_g[0], MEM_HEADS)), mkT, mv, bd, hm)
    w_out_bf = w_out.astype(BF16)
    w_in_bf = w_mlp_in.astype(BF16)
    w_o_bf = w_mlp_out.astype(BF16)
    x1 = _post(x[0], conv0, memo0, w_out_bf[0, :CONV_W], w_out_bf[0, CONV_W:], row(norm_mlp_g[0]),
               w_in_bf[0], w_o_bf[0])

    src = _gate_layout()
    used = src >= 0
    w_gate = jnp.where(used[None, :], b_w_in[0][:, NSA_W + MEM_W + np.maximum(src, 0)], 0.0)
    b_gate = jnp.where(used, b_gate_b[0][np.maximum(src, 0)], 0.0)
    w_in1 = jnp.concatenate([b_w_in[0][:, :NSA_W + MEM_W], w_gate], axis=1).astype(BF16)
    q_gain = jnp.tile(b_q_norm_g[0], NSA_HEADS) * (QK_SCALE * LOG2E)
    kng = jnp.stack([jnp.tile(k_norm_g[1], GROUPS), jnp.tile(k_norm_g[2], GROUPS)])
    qT, gT, ksel, kwin, vselT, vwinT, kvc, memo1 = _layer1_proj(
        x1, row(kv_norm_g), row(norm_mix_g[1]), w_kv.astype(BF16), w_in1, row(b_gate), row(q_gain),
        row(jnp.tile(mem_q_norm_g[1], MEM_HEADS)), kng, mkT, mv, bd, hm, _onehot_pattern())

    t = kvc.reshape(s, 2 * GROUPS, HEAD_DIM).transpose(1, 0, 2).reshape(2 * GROUPS, nc, CMP_STRIDE * HEAD_DIM)
    w1 = jnp.stack([cmp_w1_k, cmp_w1_v]).astype(BF16)
    w2 = jnp.stack([cmp_w2_k, cmp_w2_v]).astype(BF16)
    pe = jnp.stack([cmp_pe_k.reshape(1, -1), cmp_pe_v.reshape(1, -1)])
    cmp_rows, cmp_cols = _compress(t, w1, w2, w2.transpose(0, 2, 1), pe, row(k_norm_g[0]),
                                   k_norm_g[0].reshape(-1, 1))

    nsa = _nsa(qT, gT, cmp_rows, cmp_cols, ksel, vselT, kwin, vwinT)
    x2 = _post(x1, nsa, memo1, w_out_bf[1, :NSA_W], w_out_bf[1, NSA_W:], row(norm_mlp_g[1]),
               w_in_bf[1], w_o_bf[1])
    return x2[None]
```

```python
import functools

import numpy as np
import jax
import jax.numpy as jnp
from jax import lax
from jax.experimental import pallas as pl
from jax.experimental.pallas import tpu as pltpu

F32 = jnp.float32
BF16 = jnp.bfloat16

D_MODEL = 1024
HEAD_DIM = 64
MEM_LEN = 256
MEM_HEADS = 4
MEM_W = MEM_HEADS * HEAD_DIM
CONV_W = D_MODEL - MEM_W
CONV_K = 31
NSA_HEADS = CONV_W // HEAD_DIM
NSA_W = NSA_HEADS * HEAD_DIM
GROUPS = 2
HEADS_PER_GROUP = NSA_HEADS // GROUPS
CMP_L = 32
CMP_STRIDE = 16
CMP_HID = 256
SEL_L = 64
N_SEL = 16
WIN = 512
Q_BLOCK = 128
D_FF = 4 * D_MODEL
GATE_PAD = 128
GATE_ROWS = GATE_PAD // GROUPS

ROW_TILE = 512
FF_CHUNK = 1024
KEY_TILE = 1024
ONEHOT_SPAN = 1024
CMP_PATHS = 4
SEL_PATHS = 4
N_STREAMS = 4
KEY_CHUNK = 256
SEL_PER_SPAN = ONEHOT_SPAN // SEL_L
WIN_KEYS = WIN + Q_BLOCK
CONV_HALO = 32
CONV_ROWS = 64
VAUG_ROWS = 80
VMEM_LIMIT = 56 * 1024 * 1024
MXU_WIDTH = 256

EPS = 1e-6
LN_EPS = 1e-5
LOG2E = 1.4426950408889634
QK_SCALE = HEAD_DIM ** -0.5
MASK_BIAS = -2.0 ** 126
M_INIT = -2.0 ** 100


def _dot(a, b):
    return jnp.dot(a, b, preferred_element_type=F32)


def _dot_3tiles(a, b):
    two = 2 * MXU_WIDTH
    half = a.shape[0] // 2
    return jnp.concatenate([
        _dot(a, b[:, 0:two]),
        jnp.concatenate([_dot(a[0:half], b[:, two:]), _dot(a[half:], b[:, two:])], axis=0),
    ], axis=1)


def _rms_scale(x):
    return lax.rsqrt(jnp.mean(x * x, axis=-1, keepdims=True) + EPS)


def _head_meansq(x, bd):
    x2 = x * x
    hi = x2.astype(BF16)
    lo = (x2 - hi.astype(F32)).astype(BF16)
    return (_dot(hi, bd) + _dot(lo, bd)) * (1.0 / HEAD_DIM)


def _mem_attention(qn, mkT, mv, hm_ref):
    out = jnp.zeros(qn.shape, F32)
    for h in range(MEM_HEADS):
        hm = hm_ref[h:h + 1, :]
        s = _dot((qn * hm).astype(BF16), mkT)
        e = jnp.exp2(s - jnp.max(s, axis=-1, keepdims=True))
        l = jnp.sum(e, axis=-1, keepdims=True)
        out = out + _dot(e.astype(BF16), mv) * (hm / l)
    return out


def _memkv_kernel(mem_ref, g_ref, w_ref, kg_ref, bd_ref, mkT_ref, mv_ref):
    m = mem_ref[...]
    mn = (m * _rms_scale(m) * g_ref[...]).astype(BF16)
    kv = _dot(mn, w_ref[0])
    k = kv[:, :MEM_W]
    kn = k * lax.rsqrt(_head_meansq(k, bd_ref[...]) + EPS) * kg_ref[0]
    mkT_ref[0] = (kn * (QK_SCALE * LOG2E)).T.astype(BF16)
    mv_ref[0] = kv[:, MEM_W:].astype(BF16)


def _memkv(mem, g, w, kg, bd):
    depth = w.shape[0]
    return pl.pallas_call(
        _memkv_kernel,
        grid=(depth,),
        in_specs=[
            pl.BlockSpec((MEM_LEN, D_MODEL), lambda l: (0, 0)),
            pl.BlockSpec((1, D_MODEL), lambda l: (0, 0)),
            pl.BlockSpec((1, D_MODEL, 2 * MEM_W), lambda l: (l, 0, 0)),
            pl.BlockSpec((1, 1, MEM_W), lambda l: (l, 0, 0)),
            pl.BlockSpec((MEM_W, MEM_W), lambda l: (0, 0)),
        ],
        out_specs=[
            pl.BlockSpec((1, MEM_W, MEM_LEN), lambda l: (l, 0, 0)),
            pl.BlockSpec((1, MEM_LEN, MEM_W), lambda l: (l, 0, 0)),
        ],
        out_shape=[
            jax.ShapeDtypeStruct((depth, MEM_W, MEM_LEN), BF16),
            jax.ShapeDtypeStruct((depth, MEM_LEN, MEM_W), BF16),
        ],
        name="mem_kv",
    )(mem, g, w, kg, bd)


def _layer0_kernel(x_ref, g_ref, win_ref, bglu_ref, dw_ref, dwb_ref, lng_ref, lnb_ref, qg_ref,
                   mkT_ref, mv_ref, bd_ref, hm_ref, conv_ref, memo_ref, buf_ref, cv_ref, sh_ref):
    i = pl.program_id(0)
    tm = x_ref.shape[0]
    x = x_ref[...]
    h = (x * _rms_scale(x) * g_ref[...]).astype(BF16)
    u = _dot(h, win_ref[...])
    a = u[:, :CONV_W] + bglu_ref[:, :CONV_W]
    gate = u[:, CONV_W:2 * CONV_W] + bglu_ref[:, CONV_W:]
    v = a * jax.nn.sigmoid(gate)

    @pl.when(i == 0)
    def _():
        buf_ref[0:CONV_HALO, :] = jnp.zeros((CONV_HALO, CONV_W), F32)

    buf_ref[CONV_HALO:CONV_HALO + tm, :] = v

    base = CONV_HALO - (CONV_K - 1)
    rows = tm + CONV_HALO
    for cb in range(CONV_W // 128):
        cs = slice(cb * 128, (cb + 1) * 128)
        xb = buf_ref[:, cs]
        sh_ref[0] = xb
        for b in range(1, 8):
            sh_ref[b] = pltpu.roll(xb, rows - b, axis=0)

        def conv_rows(r, carry, cs=cs):
            r0 = pl.multiple_of(r * CONV_ROWS, CONV_ROWS)
            acc = jnp.zeros((CONV_ROWS, 128), F32) + dwb_ref[:, cs]
            for k in range(CONV_K):
                a, b = divmod(base + k, 8)
                acc = acc + dw_ref[k:k + 1, cs] * sh_ref[b, pl.ds(r0 + 8 * a, CONV_ROWS), :]
            cv_ref[pl.ds(r0, CONV_ROWS), cs] = acc
            return carry

        lax.fori_loop(0, tm // CONV_ROWS, conv_rows, 0)
    buf_ref[0:CONV_HALO, :] = buf_ref[tm:tm + CONV_HALO, :]

    cv = cv_ref[...]
    mu = jnp.mean(cv, axis=-1, keepdims=True)
    d = cv - mu
    var = jnp.mean(d * d, axis=-1, keepdims=True)
    y = d * lax.rsqrt(var + LN_EPS) * lng_ref[...] + lnb_ref[...]
    conv_ref[...] = (y * jax.nn.sigmoid(y)).astype(BF16)

    qm = u[:, 2 * CONV_W:]
    qn = qm * lax.rsqrt(_head_meansq(qm, bd_ref[...]) + EPS) * qg_ref[...]
    memo_ref[...] = _mem_attention(qn, mkT_ref[0], mv_ref[0], hm_ref).astype(BF16)


def _layer0(x, g, w_in, bglu, dw, dwb, lng, lnb, qg, mkT, mv, bd, hm):
    s = x.shape[0]
    tm = ROW_TILE
    const = lambda i: (0, 0)
    return pl.pallas_call(
        _layer0_kernel,
        grid=(s // tm,),
        in_specs=[
            pl.BlockSpec((tm, D_MODEL), lambda i: (i, 0)),
            pl.BlockSpec((1, D_MODEL), const),
            pl.BlockSpec(w_in.shape, const),
            pl.BlockSpec((1, 2 * CONV_W), const),
            pl.BlockSpec((CONV_K, CONV_W), const),
            pl.BlockSpec((1, CONV_W), const),
            pl.BlockSpec((1, CONV_W), const),
            pl.BlockSpec((1, CONV_W), const),
            pl.BlockSpec((1, MEM_W), const),
            pl.BlockSpec((1, MEM_W, MEM_LEN), lambda i: (0, 0, 0)),
            pl.BlockSpec((1, MEM_LEN, MEM_W), lambda i: (0, 0, 0)),
            pl.BlockSpec((MEM_W, MEM_W), const),
            pl.BlockSpec((MEM_HEADS, MEM_W), const),
        ],
        out_specs=[
            pl.BlockSpec((tm, CONV_W), lambda i: (i, 0)),
            pl.BlockSpec((tm, MEM_W), lambda i: (i, 0)),
        ],
        out_shape=[
            jax.ShapeDtypeStruct((s, CONV_W), BF16),
            jax.ShapeDtypeStruct((s, MEM_W), BF16),
        ],
        scratch_shapes=[
            pltpu.VMEM((tm + CONV_HALO, CONV_W), F32),
            pltpu.VMEM((tm, CONV_W), F32),
            pltpu.VMEM((8, tm + CONV_HALO, 128), F32),
        ],
        compiler_params=pltpu.CompilerParams(
            dimension_semantics=("arbitrary",), vmem_limit_bytes=VMEM_LIMIT),
        name="layer0_mixer",
    )(x, g, w_in, bglu, dw, dwb, lng, lnb, qg, mkT, mv, bd, hm)


def _post_kernel(x_ref, ma_ref, mb_ref, wa_ref, wb_ref, g_ref, win_ref, wout_ref, o_ref):
    o_ref[...] = x_ref[...] + _dot(ma_ref[...], wa_ref[...]) + _dot(mb_ref[...], wb_ref[...])
    x1 = o_ref[...]
    h = (x1 * _rms_scale(x1) * g_ref[...]).astype(BF16)
    for c in range(D_FF // FF_CHUNK):
        cs = slice(c * FF_CHUNK, (c + 1) * FF_CHUNK)
        t = jnp.maximum(_dot(h, win_ref[:, cs]), 0.0)
        o_ref[...] += _dot((t * t).astype(BF16), wout_ref[cs, :])


def _post(x, ma, mb, wa, wb, g, w_in, w_out):
    s = x.shape[0]
    tm = ROW_TILE
    const = lambda i: (0, 0)
    single = pl.Buffered(1)
    return pl.pallas_call(
        _post_kernel,
        grid=(s // tm,),
        in_specs=[
            pl.BlockSpec((tm, D_MODEL), lambda i: (i, 0)),
            pl.BlockSpec((tm, CONV_W), lambda i: (i, 0)),
            pl.BlockSpec((tm, MEM_W), lambda i: (i, 0)),
            pl.BlockSpec((CONV_W, D_MODEL), const, pipeline_mode=single),
            pl.BlockSpec((MEM_W, D_MODEL), const, pipeline_mode=single),
            pl.BlockSpec((1, D_MODEL), const),
            pl.BlockSpec((D_MODEL, D_FF), const, pipeline_mode=single),
            pl.BlockSpec((D_FF, D_MODEL), const, pipeline_mode=single),
        ],
        out_specs=pl.BlockSpec((tm, D_MODEL), lambda i: (i, 0)),
        out_shape=jax.ShapeDtypeStruct((s, D_MODEL), F32),
        compiler_params=pltpu.CompilerParams(
            dimension_semantics=("arbitrary",), vmem_limit_bytes=VMEM_LIMIT),
        name="outproj_mlp",
    )(x, ma, mb, wa, wb, g, w_in, w_out)


def _layer1_proj_kernel(x_ref, gkv_ref, gmix_ref, wkv_ref, win_ref, gb_ref, qg_ref, mqg_ref, kng_ref,
                        mkT_ref, mv_ref, bd_ref, hm_ref, pat_ref,
                        qT_ref, gT_ref, ksel_ref, kwin_ref, vselT_ref, vwinT_ref, kvc_ref, memo_ref):
    tm = x_ref.shape[0]
    x = x_ref[...]
    xn = x * _rms_scale(x)
    bd = bd_ref[...]

    kv = _dot((xn * gkv_ref[...]).astype(BF16), wkv_ref[...])
    for i in range(2 * GROUPS):
        kvc_ref[i] = kv[:, i * HEAD_DIM:(i + 1) * HEAD_DIM].astype(BF16)
    k2 = kv[:, 256:384]
    v2 = kv[:, 384:512]
    kw = kv[:, 512:640]
    vw = kv[:, 640:768]
    bd2 = bd[0:128, 0:128]
    k2n = k2 * lax.rsqrt(_head_meansq(k2, bd2) + EPS) * kng_ref[0:1, :]
    kwn = kw * lax.rsqrt(_head_meansq(kw, bd2) + EPS) * kng_ref[1:2, :]

    lane = lax.broadcasted_iota(jnp.int32, (tm, 128), 1)
    pat = pat_ref[...]
    ksel_ref[0] = jnp.where(lane < HEAD_DIM, k2n, pat).astype(BF16)
    ksel_ref[1] = jnp.where(lane < HEAD_DIM, pltpu.roll(k2n, HEAD_DIM, axis=1), pat).astype(BF16)
    kwin_ref[0] = kwn[:, :HEAD_DIM].astype(BF16)
    kwin_ref[1] = kwn[:, HEAD_DIM:].astype(BF16)

    ones_rows = (lax.broadcasted_iota(jnp.int32, (VAUG_ROWS - HEAD_DIM, tm), 0) == 0).astype(BF16)
    v2T = v2.T.astype(BF16)
    vwT = vw.T.astype(BF16)
    for g in range(GROUPS):
        vselT_ref[g, 0:HEAD_DIM, :] = v2T[g * HEAD_DIM:(g + 1) * HEAD_DIM, :]
        vselT_ref[g, HEAD_DIM:VAUG_ROWS, :] = ones_rows
        vwinT_ref[g, 0:HEAD_DIM, :] = vwT[g * HEAD_DIM:(g + 1) * HEAD_DIM, :]
        vwinT_ref[g, HEAD_DIM:VAUG_ROWS, :] = ones_rows

    u = _dot((xn * gmix_ref[...]).astype(BF16), win_ref[...])
    for cb in range(NSA_W // 256):
        cs = slice(cb * 256, (cb + 1) * 256)
        qc = u[:, cs]
        qn = qc * lax.rsqrt(_head_meansq(qc, bd) + EPS) * qg_ref[:, cs]
        qT_ref[4 * cb:4 * cb + 4] = qn.T.astype(BF16).reshape(4, HEAD_DIM, tm)

    gates = jax.nn.sigmoid(u[:, NSA_W + MEM_W:] + gb_ref[...])
    gT_ref[...] = gates.T

    qm = u[:, NSA_W:NSA_W + MEM_W]
    qmn = qm * lax.rsqrt(_head_meansq(qm, bd) + EPS) * mqg_ref[...]
    memo_ref[...] = _mem_attention(qmn, mkT_ref[0], mv_ref[0], hm_ref).astype(BF16)


def _layer1_proj(x, gkv, gmix, wkv, w_in, gb, qg, mqg, kng, mkT, mv, bd, hm, pat):
    s = x.shape[0]
    tm = ROW_TILE
    const = lambda i: (0, 0)
    pat_blocks = pat.shape[0] // tm
    return pl.pallas_call(
        _layer1_proj_kernel,
        grid=(s // tm,),
        in_specs=[
            pl.BlockSpec((tm, D_MODEL), lambda i: (i, 0)),
            pl.BlockSpec((1, D_MODEL), const),
            pl.BlockSpec((1, D_MODEL), const),
            pl.BlockSpec(wkv.shape, const),
            pl.BlockSpec(w_in.shape, const),
            pl.BlockSpec((1, GATE_PAD), const),
            pl.BlockSpec((1, NSA_W), const),
            pl.BlockSpec((1, MEM_W), const),
            pl.BlockSpec((2, 128), const),
            pl.BlockSpec((1, MEM_W, MEM_LEN), lambda i: (1, 0, 0)),
            pl.BlockSpec((1, MEM_LEN, MEM_W), lambda i: (1, 0, 0)),
            pl.BlockSpec((MEM_W, MEM_W), const),
            pl.BlockSpec((MEM_HEADS, MEM_W), const),
            pl.BlockSpec((tm, 128), lambda i: (i % pat_blocks, 0)),
        ],
        out_specs=[
            pl.BlockSpec((NSA_HEADS, HEAD_DIM, tm), lambda i: (0, 0, i)),
            pl.BlockSpec((GATE_PAD, tm), lambda i: (0, i)),
            pl.BlockSpec((GROUPS, tm, 128), lambda i: (0, i, 0)),
            pl.BlockSpec((GROUPS, tm, HEAD_DIM), lambda i: (0, i, 0)),
            pl.BlockSpec((GROUPS, VAUG_ROWS, tm), lambda i: (0, 0, i)),
            pl.BlockSpec((GROUPS, VAUG_ROWS, tm), lambda i: (0, 0, i)),
            pl.BlockSpec((2 * GROUPS, tm, HEAD_DIM), lambda i: (0, i, 0)),
            pl.BlockSpec((tm, MEM_W), lambda i: (i, 0)),
        ],
        out_shape=[
            jax.ShapeDtypeStruct((NSA_HEADS, HEAD_DIM, s), BF16),
            jax.ShapeDtypeStruct((GATE_PAD, s), F32),
            jax.ShapeDtypeStruct((GROUPS, s, 128), BF16),
            jax.ShapeDtypeStruct((GROUPS, s, HEAD_DIM), BF16),
            jax.ShapeDtypeStruct((GROUPS, VAUG_ROWS, s), BF16),
            jax.ShapeDtypeStruct((GROUPS, VAUG_ROWS, s), BF16),
            jax.ShapeDtypeStruct((2 * GROUPS, s, HEAD_DIM), BF16),
            jax.ShapeDtypeStruct((s, MEM_W), BF16),
        ],
        compiler_params=pltpu.CompilerParams(
            dimension_semantics=("arbitrary",), vmem_limit_bytes=VMEM_LIMIT),
        name="layer1_proj",
    )(x, gkv, gmix, wkv, w_in, gb, qg, mqg, kng, mkT, mv, bd, hm, pat)


def _compress_kernel(t_ref, w1_ref, w2_ref, w2T_ref, pe_ref, kg_ref, kgT_ref, o_ref, oT_ref, hb_ref):
    nc = t_ref.shape[1]
    half = CMP_STRIDE * HEAD_DIM
    t = t_ref[0]
    ha = _dot(t, w1_ref[0, 0:half, :])
    hb_ref[0:nc, :] = _dot(t, w1_ref[0, half:2 * half, :])
    hb_ref[nc:nc + 8, :] = jnp.zeros((8, CMP_HID), F32)
    pe = jnp.broadcast_to(pe_ref[0], (8, 2 * half)).astype(BF16)
    pe_term = _dot(pe, w1_ref[0])[0:1, :]
    h = ha + hb_ref[1:nc + 1, :] + pe_term
    hg = 0.5 * h * (1.0 + jnp.tanh(0.7978845608028654 * (h + 0.044715 * (h * h * h))))
    hg = hg.astype(BF16)
    o = _dot(hg, w2_ref[0])
    oT = lax.dot_general(w2T_ref[0], hg, (((1,), (1,)), ((), ())), preferred_element_type=F32)
    is_k = pl.program_id(0) < GROUPS
    on = o * lax.rsqrt(jnp.mean(o * o, axis=1, keepdims=True) + EPS) * kg_ref[...]
    oTn = oT * lax.rsqrt(jnp.mean(oT * oT, axis=0, keepdims=True) + EPS) * kgT_ref[...]
    o_ref[0] = jnp.where(is_k, on, o).astype(BF16)
    oT_ref[0, 0:HEAD_DIM, :] = jnp.where(is_k, oTn, oT).astype(BF16)
    oT_ref[0, HEAD_DIM:VAUG_ROWS, :] = (
        lax.broadcasted_iota(jnp.int32, (VAUG_ROWS - HEAD_DIM, nc), 0) == 0).astype(BF16)


def _compress(t, w1, w2, w2T, pe, kg, kgT):
    n, nc, width = t.shape
    return pl.pallas_call(
        _compress_kernel,
        grid=(n,),
        in_specs=[
            pl.BlockSpec((1, nc, width), lambda i: (i, 0, 0)),
            pl.BlockSpec((1, 2 * width, CMP_HID), lambda i: (i // GROUPS, 0, 0)),
            pl.BlockSpec((1, CMP_HID, HEAD_DIM), lambda i: (i // GROUPS, 0, 0)),
            pl.BlockSpec((1, HEAD_DIM, CMP_HID), lambda i: (i // GROUPS, 0, 0)),
            pl.BlockSpec((1, 1, 2 * width), lambda i: (i // GROUPS, 0, 0)),
            pl.BlockSpec((1, HEAD_DIM), lambda i: (0, 0)),
            pl.BlockSpec((HEAD_DIM, 1), lambda i: (0, 0)),
        ],
        out_specs=[
            pl.BlockSpec((1, nc, HEAD_DIM), lambda i: (i, 0, 0)),
            pl.BlockSpec((1, VAUG_ROWS, nc), lambda i: (i, 0, 0)),
        ],
        out_shape=[
            jax.ShapeDtypeStruct((n, nc, HEAD_DIM), BF16),
            jax.ShapeDtypeStruct((n, VAUG_ROWS, nc), BF16),
        ],
        scratch_shapes=[pltpu.VMEM((nc + 8, CMP_HID), F32)],
        compiler_params=pltpu.CompilerParams(
            dimension_semantics=("arbitrary",), vmem_limit_bytes=VMEM_LIMIT),
        name="compress_kv",
    )(t, w1, w2, w2T, pe, kg, kgT)


def _nsa_kernel(qT_ref, gT_ref, kc_ref, vcT_ref, ksel_ref, vselT_ref, kwin_ref, vwinT_ref, out_ref,
                qaug_st, psum_st, bias_st, acc_st, m_st, s_st, oc_st, sw_st, ow_st, pick_st):
    cc = pl.program_id(1)
    nc = kc_ref.shape[1]
    nsb = bias_st.shape[1]
    hg = HEADS_PER_GROUP
    width = hg * Q_BLOCK
    n_chunks = KEY_TILE // KEY_CHUNK
    last = (cc * N_STREAMS) // (KEY_TILE // Q_BLOCK)

    def load_queries(st):
        qaug_ref = qaug_st.at[st]
        for hh in range(hg):
            qaug_ref[0:HEAD_DIM, hh * Q_BLOCK:(hh + 1) * Q_BLOCK] = qT_ref[hh, :, st * Q_BLOCK:(st + 1) * Q_BLOCK]
        qaug_ref[HEAD_DIM:128, :] = jnp.zeros((128 - HEAD_DIM, width), BF16)

    def query_pos(st):
        return (cc * N_STREAMS + st) * Q_BLOCK + (lax.broadcasted_iota(jnp.int32, (1, width), 1) & (Q_BLOCK - 1))

    for st in range(N_STREAMS):
        load_queries(st)

    def cmp_branch(rows):
        for st in range(N_STREAMS):
            qT = qaug_st[st, 0:HEAD_DIM, :]
            t_q = query_pos(st)
            oc_ref, psum_ref = oc_st.at[st], psum_st.at[st]
            s = _dot_3tiles(kc_ref[0, 0:rows, :], qT)
            seen = max(rows - 2 * cuts[0], 0)
            ci = seen + lax.broadcasted_iota(jnp.int32, (rows - seen, 1), 0)
            tail = jnp.where(ci * CMP_STRIDE + (CMP_L - 1) <= t_q, s[seen:, :], -jnp.inf)
            s = jnp.concatenate([s[0:seen, :], tail], axis=0) if seen else tail
            m = jnp.max(s, axis=0, keepdims=True)
            m = jnp.where(m == -jnp.inf, 0.0, m)
            e = jnp.exp2(s - m)
            oa = _dot(vcT_ref[0, :, 0:rows], e.astype(BF16))
            rl = 1.0 / jnp.maximum(oa[HEAD_DIM:HEAD_DIM + 1, :], 1e-30)
            oc_ref[...] = oa[0:HEAD_DIM, :] * rl
            p = e * rl
            psum = p[:, 0:Q_BLOCK]
            for hh in range(1, hg):
                psum = psum + p[:, hh * Q_BLOCK:(hh + 1) * Q_BLOCK]
            psum_ref[8:8 + rows, :] = psum

    @pl.when(cc == 0)
    def _():
        psum_st[...] = jnp.zeros(psum_st.shape, F32)

    cuts = [nc * k // CMP_PATHS for k in range(1, CMP_PATHS + 1)] if nc % (128 * CMP_PATHS) == 0 else [nc]
    need = (cc + 1) * N_STREAMS * (Q_BLOCK // CMP_STRIDE)
    for k, rows in enumerate(cuts):
        lo = cuts[k - 1] if k else 0
        pl.when((need > lo) & (need <= rows))(functools.partial(cmp_branch, rows))

    def block_validity(st, rows):
        blk = lax.broadcasted_iota(jnp.int32, (rows, Q_BLOCK), 0)
        tb = ((cc * N_STREAMS + st) * Q_BLOCK + lax.broadcasted_iota(jnp.int32, (1, Q_BLOCK), 1)) >> 6
        return blk, tb, blk <= tb

    def select_blocks(rows):
        for st in range(N_STREAMS):
            psum_ref = psum_st.at[st]
            imp = (psum_ref[pl.ds(7, rows, stride=4), :] + psum_ref[pl.ds(11, rows, stride=4), :]
                   + 2.0 * (psum_ref[pl.ds(8, rows, stride=4), :] + psum_ref[pl.ds(9, rows, stride=4), :]
                            + psum_ref[pl.ds(10, rows, stride=4), :]))
            blk, tb, valid = block_validity(st, rows)
            forced = (blk == 0) | (blk == tb) | (blk == tb - 1)
            pick_st[st, 0:rows, :] = jnp.where(forced | ~valid, -jnp.inf, imp)
        blk_f = lax.broadcasted_iota(jnp.int32, (rows, Q_BLOCK), 0).astype(F32)

        def pick(_, carry):
            for st in range(N_STREAMS):
                sc = pick_st[st, 0:rows, :]
                mx = jnp.max(sc, axis=0, keepdims=True)
                first = jnp.min(jnp.where(sc == mx, blk_f, float(rows)), axis=0, keepdims=True)
                pick_st[st, 0:rows, :] = jnp.where(blk_f == first, -jnp.inf, sc)
            return carry

        lax.fori_loop(0, N_SEL - 3, pick, 0)
        for st in range(N_STREAMS):
            sel = (pick_st[st, 0:rows, :] == -jnp.inf) & block_validity(st, rows)[2]
            bias_st[st, 0:rows, :] = jnp.where(sel, 0.0, MASK_BIAS).astype(BF16)

    @pl.when(cc == 0)
    def _():
        bias_st[...] = jnp.full(bias_st.shape, MASK_BIAS, BF16)

    b_cuts = [nsb * k // SEL_PATHS for k in range(1, SEL_PATHS + 1)] if nsb % (16 * SEL_PATHS) == 0 else [nsb]
    need_b = 2 * (cc + 1) * N_STREAMS
    for k, rows in enumerate(b_cuts):
        lo = b_cuts[k - 1] if k else 0
        pl.when((need_b > lo) & (need_b <= rows))(functools.partial(select_blocks, rows))

    def stream(st):
        c = cc * N_STREAMS + st
        qs = slice(st * Q_BLOCK, (st + 1) * Q_BLOCK)
        qaug_ref, psum_ref, bias_ref, acc_ref = qaug_st.at[st], psum_st.at[st], bias_st.at[st], acc_st.at[st]
        m_ref, s_ref, oc_ref, sw_ref, ow_ref = m_st.at[st], s_st.at[st], oc_st.at[st], sw_st.at[st], ow_st.at[st]
        qT = qaug_ref[0:HEAD_DIM, :]
        t_q = query_pos(st)

        row0 = pl.multiple_of(jnp.maximum(c * Q_BLOCK - WIN, 0), Q_BLOCK)
        sw = _dot_3tiles(kwin_ref[0, pl.ds(row0, WIN_KEYS), :], qT)
        kp = row0 + lax.broadcasted_iota(jnp.int32, (WIN_KEYS, 1), 0)
        sw = jnp.where((kp <= t_q) & (kp > t_q - WIN), sw, -jnp.inf)
        sw_ref[...] = sw
        mw = jnp.max(jnp.max(sw.reshape(WIN_KEYS // 8, 8, width), axis=0), axis=0, keepdims=True)
        mw = jnp.where(mw == -jnp.inf, 0.0, mw)

        acc_ref[...] = jnp.zeros(acc_ref.shape, F32)
        m_ref[...] = jnp.full(m_ref.shape, M_INIT, F32)

        def set_bias(j):
            span = j // (ONEHOT_SPAN // KEY_TILE)
            b16 = bias_ref[pl.ds(pl.multiple_of(span * SEL_PER_SPAN, SEL_PER_SPAN), SEL_PER_SPAN), :]
            for hh in range(hg):
                qaug_ref[HEAD_DIM:HEAD_DIM + SEL_PER_SPAN, hh * Q_BLOCK:(hh + 1) * Q_BLOCK] = b16

        def scores(j, r):
            k0 = pl.multiple_of(j * KEY_TILE + r * KEY_CHUNK, KEY_CHUNK)
            return _dot(ksel_ref[0, pl.ds(k0, KEY_CHUNK), :], qaug_ref[...])

        def chunk_max(cmax, sc):
            return jnp.maximum(cmax, jnp.max(sc.reshape(KEY_CHUNK // 8, 8, width), axis=0))

        def new_max(cmax):
            m_old = m_ref[...]
            m_new = jnp.maximum(m_old, jnp.max(cmax, axis=0, keepdims=True))
            m_ref[...] = m_new
            return m_new, jnp.exp2(m_old - m_new)

        def values(j, r):
            k0 = pl.multiple_of(j * KEY_TILE + r * KEY_CHUNK, KEY_CHUNK)
            return vselT_ref[0, :, pl.ds(k0, KEY_CHUNK)]

        cmax0 = jnp.full((8, width), MASK_BIAS, F32)
        set_bias(0)
        cmax = cmax0
        ow = jnp.zeros((VAUG_ROWS, width), F32)
        w_cuts = [WIN_KEYS * r // n_chunks // 128 * 128 for r in range(n_chunks)] + [WIN_KEYS]
        for r in range(n_chunks):
            sc = _dot_3tiles(ksel_ref[0, r * KEY_CHUNK:(r + 1) * KEY_CHUNK, :], qaug_ref[...])
            ws = slice(w_cuts[r], w_cuts[r + 1])
            pw = jnp.exp2(sw_ref[ws, :] - mw).astype(BF16)
            s_ref[r * KEY_CHUNK:(r + 1) * KEY_CHUNK, :] = sc
            cmax = chunk_max(cmax, sc)
            ow = ow + _dot(vwinT_ref[0, :, pl.ds(pl.multiple_of(row0 + w_cuts[r], 128), w_cuts[r + 1] - w_cuts[r])], pw)
        ow_ref[...] = ow[0:HEAD_DIM, :] / jnp.maximum(ow[HEAD_DIM:HEAD_DIM + 1, :], 1e-30)

        def pipe_step(j, cmax):
            m_new, alpha = new_max(cmax)
            set_bias(j + 1)
            acc = acc_ref[...] * alpha
            cnext = cmax0
            for r in range(n_chunks):
                rs = slice(r * KEY_CHUNK, (r + 1) * KEY_CHUNK)
                sc = scores(j + 1, r)
                pj = jnp.exp2(s_ref[rs, :] - m_new).astype(BF16)
                s_ref[rs, :] = sc
                cnext = chunk_max(cnext, sc)
                acc = acc + _dot(values(j, r), pj)
            acc_ref[...] = acc
            return cnext

        def finish():
            diag = pl.ds(pl.multiple_of((c % (KEY_TILE // Q_BLOCK)) * Q_BLOCK, Q_BLOCK), Q_BLOCK)
            ku = lax.broadcasted_iota(jnp.int32, (Q_BLOCK, 1), 0)
            s_ref[diag, :] = jnp.where(ku <= t_q - c * Q_BLOCK, s_ref[diag, :], MASK_BIAS)

            def finish_tile(rows):
                s_last = s_ref[0:rows, :]
                m_new, alpha = new_max(jnp.max(s_last.reshape(rows // 8, 8, width), axis=0))
                p_last = jnp.exp2(s_last - m_new).astype(BF16)
                acc_ref[...] = acc_ref[...] * alpha + _dot(
                    vselT_ref[0, :, pl.ds(pl.multiple_of(last * KEY_TILE, KEY_TILE), rows)], p_last)

            diag_chunk = (c % (KEY_TILE // Q_BLOCK)) // (KEY_CHUNK // Q_BLOCK)
            for k in range(n_chunks):
                pl.when(diag_chunk == k)(functools.partial(finish_tile, (k + 1) * KEY_CHUNK))
            o_cmp = oc_ref[...]
            o_win = ow_ref[...]
            o_slc = acc_ref[0:HEAD_DIM, :] / jnp.maximum(acc_ref[HEAD_DIM:HEAD_DIM + 1, :], 1e-30)

            heads = []
            for hh in range(hg):
                hs = slice(hh * Q_BLOCK, (hh + 1) * Q_BLOCK)
                heads.append(o_cmp[:, hs] * gT_ref[hh:hh + 1, qs] + o_slc[:, hs] * gT_ref[8 + hh:9 + hh, qs]
                             + o_win[:, hs] * gT_ref[16 + hh:17 + hh, qs])
            for pr in range(hg // 2):
                pair = jnp.concatenate([heads[2 * pr], heads[2 * pr + 1]], axis=0)
                out_ref[qs, pr * 128:(pr + 1) * 128] = pair.T.astype(BF16)

        return cmax, pipe_step, finish

    streams = [stream(st) for st in range(N_STREAMS)]

    def sweep_steps(j, cmaxes):
        return tuple(step(j, cm) for (_, step, _), cm in zip(streams, cmaxes))

    lax.fori_loop(0, last, sweep_steps, tuple(cm for cm, _, _ in streams))
    for _, _, finish in streams:
        finish()


def _nsa(qT, gT, kc, vcT, ksel, vselT, kwin, vwinT):
    s = ksel.shape[1]
    nc = kc.shape[1]
    nsb = s // SEL_L
    hg = HEADS_PER_GROUP
    ns = N_STREAMS
    qb = ns * Q_BLOCK
    width = hg * Q_BLOCK
    grp = lambda g, c: (g, 0, 0)
    once = pl.Buffered(1)
    return pl.pallas_call(
        _nsa_kernel,
        grid=(GROUPS, s // qb),
        in_specs=[
            pl.BlockSpec((hg, HEAD_DIM, qb), lambda g, c: (g, 0, c)),
            pl.BlockSpec((GATE_ROWS, qb), lambda g, c: (g, c)),
            pl.BlockSpec((1, nc, HEAD_DIM), grp, pipeline_mode=once),
            pl.BlockSpec((1, VAUG_ROWS, nc), lambda g, c: (GROUPS + g, 0, 0), pipeline_mode=once),
            pl.BlockSpec((1, s, 128), grp, pipeline_mode=once),
            pl.BlockSpec((1, VAUG_ROWS, s), grp, pipeline_mode=once),
            pl.BlockSpec((1, s, HEAD_DIM), grp, pipeline_mode=once),
            pl.BlockSpec((1, VAUG_ROWS, s), grp, pipeline_mode=once),
        ],
        out_specs=pl.BlockSpec((qb, hg * HEAD_DIM), lambda g, c: (c, g)),
        out_shape=jax.ShapeDtypeStruct((s, NSA_W), BF16),
        scratch_shapes=[
            pltpu.VMEM((ns, 128, width), BF16),
            pltpu.VMEM((ns, nc + 16, Q_BLOCK), F32),
            pltpu.VMEM((ns, nsb, Q_BLOCK), BF16),
            pltpu.VMEM((ns, VAUG_ROWS, width), F32),
            pltpu.VMEM((ns, 1, width), F32),
            pltpu.VMEM((ns, KEY_TILE, width), F32),
            pltpu.VMEM((ns, HEAD_DIM, width), F32),
            pltpu.VMEM((ns, WIN_KEYS, width), F32),
            pltpu.VMEM((ns, HEAD_DIM, width), F32),
            pltpu.VMEM((ns, nsb, Q_BLOCK), F32),
        ],
        compiler_params=pltpu.CompilerParams(
            dimension_semantics=("arbitrary", "arbitrary"), vmem_limit_bytes=VMEM_LIMIT),
        name="nsa_attention",
    )(qT, gT, kc, vcT, ksel, vselT, kwin, vwinT)


def _block_diag_ones():
    idx = np.arange(MEM_W) // HEAD_DIM
    return jnp.asarray((idx[:, None] == idx[None, :]).astype(np.float32), BF16)


def _head_masks():
    idx = np.arange(MEM_W) // HEAD_DIM
    return jnp.asarray((idx[None, :] == np.arange(MEM_HEADS)[:, None]).astype(np.float32))


def _onehot_pattern():
    rows = np.arange(ONEHOT_SPAN)[:, None] // SEL_L
    lanes = np.arange(128)[None, :] - HEAD_DIM
    return jnp.asarray((rows == lanes).astype(np.float32))


def _gate_layout():
    src = np.full((GATE_PAD,), -1, np.int64)
    for g in range(GROUPS):
        for b in range(3):
            for hh in range(HEADS_PER_GROUP):
                src[g * GATE_ROWS + b * 8 + hh] = (g * HEADS_PER_GROUP + hh) * 3 + b
    return src


def kernel(x, mem, norm_mix_g, norm_mlp_g, mem_norm_g, w_mem_kv, mem_q_norm_g, mem_k_norm_g, w_out, w_mlp_in, w_mlp_out, a_w_in, a_b_glu, a_dw, a_dw_b, a_ln_g, a_ln_b, b_w_in, b_gate_b, b_q_norm_g, kv_norm_g, w_kv, k_norm_g, cmp_pe_k, cmp_pe_v, cmp_w1_k, cmp_w2_k, cmp_w1_v, cmp_w2_v):
    batch, s, _ = x.shape
    assert batch == 1 and s % ONEHOT_SPAN == 0 and s >= WIN_KEYS and (KEY_TILE // Q_BLOCK) % N_STREAMS == 0
    assert w_out.shape[0] == 2 and a_w_in.shape[0] == 1 and b_w_in.shape[0] == 1
    nc = s // CMP_STRIDE
    row = lambda v: v.reshape(1, -1)
    bd = _block_diag_ones()
    hm = _head_masks()

    mkT, mv = _memkv(mem[0], row(mem_norm_g), w_mem_kv.astype(BF16),
                     jnp.tile(mem_k_norm_g, (1, MEM_HEADS))[:, None, :], bd)

    conv0, memo0 = _layer0(
        x[0], row(norm_mix_g[0]), a_w_in[0].astype(BF16), row(a_b_glu[0]), a_dw[0], row(a_dw_b[0]),
        row(a_ln_g[0]), row(a_ln_b[0]), row(jnp.tile(mem_q_norm_g[0], MEM_HEADS)), mkT, mv, bd, hm)
    w_out_bf = w_out.astype(BF16)
    w_in_bf = w_mlp_in.astype(BF16)
    w_o_bf = w_mlp_out.astype(BF16)
    x1 = _post(x[0], conv0, memo0, w_out_bf[0, :CONV_W], w_out_bf[0, CONV_W:], row(norm_mlp_g[0]),
               w_in_bf[0], w_o_bf[0])

    src = _gate_layout()
    used = src >= 0
    w_gate = jnp.where(used[None, :], b_w_in[0][:, NSA_W + MEM_W + np.maximum(src, 0)], 0.0)
    b_gate = jnp.where(used, b_gate_b[0][np.maximum(src, 0)], 0.0)
    w_in1 = jnp.concatenate([b_w_in[0][:, :NSA_W + MEM_W], w_gate], axis=1).astype(BF16)
    q_gain = jnp.tile(b_q_norm_g[0], NSA_HEADS) * (QK_SCALE * LOG2E)
    kng = jnp.stack([jnp.tile(k_norm_g[1], GROUPS), jnp.tile(k_norm_g[2], GROUPS)])
    qT, gT, ksel, kwin, vselT, vwinT, kvc, memo1 = _layer1_proj(
        x1, row(kv_norm_g), row(norm_mix_g[1]), w_kv.astype(BF16), w_in1, row(b_gate), row(q_gain),
        row(jnp.tile(mem_q_norm_g[1], MEM_HEADS)), kng, mkT, mv, bd, hm, _onehot_pattern())

    t = kvc.reshape(2 * GROUPS, nc, CMP_STRIDE * HEAD_DIM)
    w1 = jnp.stack([cmp_w1_k, cmp_w1_v]).astype(BF16)
    w2 = jnp.stack([cmp_w2_k, cmp_w2_v]).astype(BF16)
    pe = jnp.stack([cmp_pe_k.reshape(1, -1), cmp_pe_v.reshape(1, -1)])
    cmp_rows, cmp_cols = _compress(t, w1, w2, w2.transpose(0, 2, 1), pe, row(k_norm_g[0]),
                                   k_norm_g[0].reshape(-1, 1))

    nsa = _nsa(qT, gT, cmp_rows, cmp_cols, ksel, vselT, kwin, vwinT)
    x2 = _post(x1, nsa, memo1, w_out_bf[1, :NSA_W], w_out_bf[1, NSA_W:], row(norm_mlp_g[1]),
               w_in_bf[1], w_o_bf[1])
    return x2[None]
```

```python
import functools

import numpy as np
import jax
import jax.numpy as jnp
from jax import lax
from jax.experimental import pallas as pl
from jax.experimental.pallas import tpu as pltpu

F32 = jnp.float32
BF16 = jnp.bfloat16

D_MODEL = 1024
HEAD_DIM = 64
MEM_LEN = 256
MEM_HEADS = 4
MEM_W = MEM_HEADS * HEAD_DIM
CONV_W = D_MODEL - MEM_W
CONV_K = 31
NSA_HEADS = CONV_W // HEAD_DIM
NSA_W = NSA_HEADS * HEAD_DIM
GROUPS = 2
HEADS_PER_GROUP = NSA_HEADS // GROUPS
CMP_L = 32
CMP_STRIDE = 16
CMP_HID = 256
SEL_L = 64
N_SEL = 16
WIN = 512
Q_BLOCK = 128
D_FF = 4 * D_MODEL
GATE_PAD = 128
GATE_ROWS = GATE_PAD // GROUPS

ROW_TILE = 512
FF_CHUNK = 1024
KEY_TILE = 1024
ONEHOT_SPAN = 1024
CMP_PATHS = 4
SEL_PATHS = 4
N_STREAMS = 4
KEY_CHUNK = 256
SEL_PER_SPAN = ONEHOT_SPAN // SEL_L
WIN_KEYS = WIN + Q_BLOCK
CONV_HALO = 32
CONV_ROWS = 64
VAUG_ROWS = 80
VMEM_LIMIT = 56 * 1024 * 1024
MXU_WIDTH = 256

EPS = 1e-6
LN_EPS = 1e-5
LOG2E = 1.4426950408889634
QK_SCALE = HEAD_DIM ** -0.5
MASK_BIAS = -2.0 ** 126
M_INIT = -2.0 ** 100


def _dot(a, b):
    return jnp.dot(a, b, preferred_element_type=F32)


def _dot_3tiles(a, b):
    two = 2 * MXU_WIDTH
    half = a.shape[0] // 2
    return jnp.concatenate([
        _dot(a, b[:, 0:two]),
        jnp.concatenate([_dot(a[0:half], b[:, two:]), _dot(a[half:], b[:, two:])], axis=0),
    ], axis=1)


def _sigmoid(x):
    return 0.5 * jnp.tanh(0.5 * x) + 0.5


def _rms_scale(x):
    return lax.rsqrt(jnp.mean(x * x, axis=-1, keepdims=True) + EPS)


def _head_meansq(x, bd):
    x2 = x * x
    hi = x2.astype(BF16)
    lo = (x2 - hi.astype(F32)).astype(BF16)
    return (_dot(hi, bd) + _dot(lo, bd)) * (1.0 / HEAD_DIM)


def _mem_attention(qn, mkT, mv, hm_ref):
    out = jnp.zeros(qn.shape, F32)
    for h in range(MEM_HEADS):
        hm = hm_ref[h:h + 1, :]
        s = _dot((qn * hm).astype(BF16), mkT)
        e = jnp.exp2(s - jnp.max(s, axis=-1, keepdims=True))
        l = jnp.sum(e, axis=-1, keepdims=True)
        out = out + _dot(e.astype(BF16), mv) * (hm / l)
    return out


def _memkv_kernel(mem_ref, g_ref, w_ref, kg_ref, bd_ref, mkT_ref, mv_ref):
    m = mem_ref[...]
    mn = (m * _rms_scale(m) * g_ref[...]).astype(BF16)
    kv = _dot(mn, w_ref[0])
    k = kv[:, :MEM_W]
    kn = k * lax.rsqrt(_head_meansq(k, bd_ref[...]) + EPS) * kg_ref[0]
    mkT_ref[0] = (kn * (QK_SCALE * LOG2E)).T.astype(BF16)
    mv_ref[0] = kv[:, MEM_W:].astype(BF16)


def _memkv(mem, g, w, kg, bd):
    depth = w.shape[0]
    return pl.pallas_call(
        _memkv_kernel,
        grid=(depth,),
        in_specs=[
            pl.BlockSpec((MEM_LEN, D_MODEL), lambda l: (0, 0)),
            pl.BlockSpec((1, D_MODEL), lambda l: (0, 0)),
            pl.BlockSpec((1, D_MODEL, 2 * MEM_W), lambda l: (l, 0, 0)),
            pl.BlockSpec((1, 1, MEM_W), lambda l: (l, 0, 0)),
            pl.BlockSpec((MEM_W, MEM_W), lambda l: (0, 0)),
        ],
        out_specs=[
            pl.BlockSpec((1, MEM_W, MEM_LEN), lambda l: (l, 0, 0)),
            pl.BlockSpec((1, MEM_LEN, MEM_W), lambda l: (l, 0, 0)),
        ],
        out_shape=[
            jax.ShapeDtypeStruct((depth, MEM_W, MEM_LEN), BF16),
            jax.ShapeDtypeStruct((depth, MEM_LEN, MEM_W), BF16),
        ],
        name="mem_kv",
    )(mem, g, w, kg, bd)


def _layer0_kernel(x_ref, g_ref, win_ref, bglu_ref, dw_ref, dwb_ref, lng_ref, lnb_ref, qg_ref,
                   mkT_ref, mv_ref, bd_ref, hm_ref, conv_ref, memo_ref, buf_ref, cv_ref, sh_ref):
    i = pl.program_id(0)
    tm = x_ref.shape[0]
    x = x_ref[...]
    h = (x * _rms_scale(x) * g_ref[...]).astype(BF16)
    u = _dot(h, win_ref[...])
    a = u[:, :CONV_W] + bglu_ref[:, :CONV_W]
    gate = u[:, CONV_W:2 * CONV_W] + bglu_ref[:, CONV_W:]
    v = a * _sigmoid(gate)

    @pl.when(i == 0)
    def _():
        buf_ref[0:CONV_HALO, :] = jnp.zeros((CONV_HALO, CONV_W), F32)

    buf_ref[CONV_HALO:CONV_HALO + tm, :] = v

    base = CONV_HALO - (CONV_K - 1)
    rows = tm + CONV_HALO
    for cb in range(CONV_W // 128):
        cs = slice(cb * 128, (cb + 1) * 128)
        xb = buf_ref[:, cs]
        sh_ref[0] = xb
        for b in range(1, 8):
            sh_ref[b] = pltpu.roll(xb, rows - b, axis=0)

        def conv_rows(r, carry, cs=cs):
            r0 = pl.multiple_of(r * CONV_ROWS, CONV_ROWS)
            acc = jnp.zeros((CONV_ROWS, 128), F32) + dwb_ref[:, cs]
            for k in range(CONV_K):
                a, b = divmod(base + k, 8)
                acc = acc + dw_ref[k:k + 1, cs] * sh_ref[b, pl.ds(r0 + 8 * a, CONV_ROWS), :]
            cv_ref[pl.ds(r0, CONV_ROWS), cs] = acc
            return carry

        lax.fori_loop(0, tm // CONV_ROWS, conv_rows, 0)
    buf_ref[0:CONV_HALO, :] = buf_ref[tm:tm + CONV_HALO, :]

    cv = cv_ref[...]
    mu = jnp.mean(cv, axis=-1, keepdims=True)
    d = cv - mu
    var = jnp.mean(d * d, axis=-1, keepdims=True)
    y = d * lax.rsqrt(var + LN_EPS) * lng_ref[...] + lnb_ref[...]
    conv_ref[...] = (y * _sigmoid(y)).astype(BF16)

    qm = u[:, 2 * CONV_W:]
    qn = qm * lax.rsqrt(_head_meansq(qm, bd_ref[...]) + EPS) * qg_ref[...]
    memo_ref[...] = _mem_attention(qn, mkT_ref[0], mv_ref[0], hm_ref).astype(BF16)


def _layer0(x, g, w_in, bglu, dw, dwb, lng, lnb, qg, mkT, mv, bd, hm):
    s = x.shape[0]
    tm = ROW_TILE
    const = lambda i: (0, 0)
    return pl.pallas_call(
        _layer0_kernel,
        grid=(s // tm,),
        in_specs=[
            pl.BlockSpec((tm, D_MODEL), lambda i: (i, 0)),
            pl.BlockSpec((1, D_MODEL), const),
            pl.BlockSpec(w_in.shape, const),
            pl.BlockSpec((1, 2 * CONV_W), const),
            pl.BlockSpec((CONV_K, CONV_W), const),
            pl.BlockSpec((1, CONV_W), const),
            pl.BlockSpec((1, CONV_W), const),
            pl.BlockSpec((1, CONV_W), const),
            pl.BlockSpec((1, MEM_W), const),
            pl.BlockSpec((1, MEM_W, MEM_LEN), lambda i: (0, 0, 0)),
            pl.BlockSpec((1, MEM_LEN, MEM_W), lambda i: (0, 0, 0)),
            pl.BlockSpec((MEM_W, MEM_W), const),
            pl.BlockSpec((MEM_HEADS, MEM_W), const),
        ],
        out_specs=[
            pl.BlockSpec((tm, CONV_W), lambda i: (i, 0)),
            pl.BlockSpec((tm, MEM_W), lambda i: (i, 0)),
        ],
        out_shape=[
            jax.ShapeDtypeStruct((s, CONV_W), BF16),
            jax.ShapeDtypeStruct((s, MEM_W), BF16),
        ],
        scratch_shapes=[
            pltpu.VMEM((tm + CONV_HALO, CONV_W), F32),
            pltpu.VMEM((tm, CONV_W), F32),
            pltpu.VMEM((8, tm + CONV_HALO, 128), F32),
        ],
        compiler_params=pltpu.CompilerParams(
            dimension_semantics=("arbitrary",), vmem_limit_bytes=VMEM_LIMIT),
        name="layer0_mixer",
    )(x, g, w_in, bglu, dw, dwb, lng, lnb, qg, mkT, mv, bd, hm)


def _post_kernel(x_ref, ma_ref, mb_ref, wa_ref, wb_ref, g_ref, win_ref, wout_ref, o_ref):
    o_ref[...] = x_ref[...] + _dot(ma_ref[...], wa_ref[0]) + _dot(mb_ref[...], wb_ref[0])
    x1 = o_ref[...]
    h = (x1 * _rms_scale(x1) * g_ref[...]).astype(BF16)
    for c in range(D_FF // FF_CHUNK):
        cs = slice(c * FF_CHUNK, (c + 1) * FF_CHUNK)
        t = jnp.maximum(_dot(h, win_ref[0, :, cs]), 0.0)
        o_ref[...] += _dot((t * t).astype(BF16), wout_ref[0, cs, :])


def _post(x, ma, mb, w_o, g, w_in, w_out, layer):
    s = x.shape[0]
    tm = ROW_TILE
    const = lambda i: (0, 0)
    this = lambda i: (layer, 0, 0)
    single = pl.Buffered(1)
    return pl.pallas_call(
        _post_kernel,
        grid=(s // tm,),
        in_specs=[
            pl.BlockSpec((tm, D_MODEL), lambda i: (i, 0)),
            pl.BlockSpec((tm, CONV_W), lambda i: (i, 0)),
            pl.BlockSpec((tm, MEM_W), lambda i: (i, 0)),
            pl.BlockSpec((1, CONV_W, D_MODEL), this, pipeline_mode=single),
            pl.BlockSpec((1, MEM_W, D_MODEL), lambda i: (layer, CONV_W // MEM_W, 0), pipeline_mode=single),
            pl.BlockSpec((1, D_MODEL), const),
            pl.BlockSpec((1, D_MODEL, D_FF), this, pipeline_mode=single),
            pl.BlockSpec((1, D_FF, D_MODEL), this, pipeline_mode=single),
        ],
        out_specs=pl.BlockSpec((tm, D_MODEL), lambda i: (i, 0)),
        out_shape=jax.ShapeDtypeStruct((s, D_MODEL), F32),
        compiler_params=pltpu.CompilerParams(
            dimension_semantics=("arbitrary",), vmem_limit_bytes=VMEM_LIMIT),
        name="outproj_mlp",
    )(x, ma, mb, w_o, w_o, g, w_in, w_out)


def _layer1_proj_kernel(x_ref, gkv_ref, gmix_ref, wkv_ref, win_ref, gb_ref, qg_ref, mqg_ref, kng_ref,
                        mkT_ref, mv_ref, bd_ref, hm_ref, pat_ref,
                        qT_ref, gT_ref, ksel_ref, kwin_ref, vselT_ref, vwinT_ref, kvc_ref, memo_ref):
    tm = x_ref.shape[0]
    x = x_ref[...]
    xn = x * _rms_scale(x)
    bd = bd_ref[...]

    kv = _dot((xn * gkv_ref[...]).astype(BF16), wkv_ref[...])
    for i in range(2 * GROUPS):
        kvc_ref[i] = kv[:, i * HEAD_DIM:(i + 1) * HEAD_DIM].astype(BF16)
    k2 = kv[:, 256:384]
    v2 = kv[:, 384:512]
    kw = kv[:, 512:640]
    vw = kv[:, 640:768]
    bd2 = bd[0:128, 0:128]
    k2n = k2 * lax.rsqrt(_head_meansq(k2, bd2) + EPS) * kng_ref[0:1, :]
    kwn = kw * lax.rsqrt(_head_meansq(kw, bd2) + EPS) * kng_ref[1:2, :]

    lane = lax.broadcasted_iota(jnp.int32, (tm, 128), 1)
    pat = pat_ref[...]
    ksel_ref[0] = jnp.where(lane < HEAD_DIM, k2n, pat).astype(BF16)
    ksel_ref[1] = jnp.where(lane < HEAD_DIM, pltpu.roll(k2n, HEAD_DIM, axis=1), pat).astype(BF16)
    kwin_ref[0] = kwn[:, :HEAD_DIM].astype(BF16)
    kwin_ref[1] = kwn[:, HEAD_DIM:].astype(BF16)

    ones_rows = (lax.broadcasted_iota(jnp.int32, (VAUG_ROWS - HEAD_DIM, tm), 0) == 0).astype(BF16)
    v2T = v2.T.astype(BF16)
    vwT = vw.T.astype(BF16)
    for g in range(GROUPS):
        vselT_ref[g, 0:HEAD_DIM, :] = v2T[g * HEAD_DIM:(g + 1) * HEAD_DIM, :]
        vselT_ref[g, HEAD_DIM:VAUG_ROWS, :] = ones_rows
        vwinT_ref[g, 0:HEAD_DIM, :] = vwT[g * HEAD_DIM:(g + 1) * HEAD_DIM, :]
        vwinT_ref[g, HEAD_DIM:VAUG_ROWS, :] = ones_rows

    u = _dot((xn * gmix_ref[...]).astype(BF16), win_ref[...])
    for cb in range(NSA_W // 256):
        cs = slice(cb * 256, (cb + 1) * 256)
        qc = u[:, cs]
        qn = qc * lax.rsqrt(_head_meansq(qc, bd) + EPS) * qg_ref[:, cs]
        qT_ref[4 * cb:4 * cb + 4] = qn.T.astype(BF16).reshape(4, HEAD_DIM, tm)

    gates = _sigmoid(u[:, NSA_W + MEM_W:] + gb_ref[...])
    gT_ref[...] = gates.T

    qm = u[:, NSA_W:NSA_W + MEM_W]
    qmn = qm * lax.rsqrt(_head_meansq(qm, bd) + EPS) * mqg_ref[...]
    memo_ref[...] = _mem_attention(qmn, mkT_ref[0], mv_ref[0], hm_ref).astype(BF16)


def _layer1_proj(x, gkv, gmix, wkv, w_in, gb, qg, mqg, kng, mkT, mv, bd, hm, pat):
    s = x.shape[0]
    tm = ROW_TILE
    const = lambda i: (0, 0)
    pat_blocks = pat.shape[0] // tm
    return pl.pallas_call(
        _layer1_proj_kernel,
        grid=(s // tm,),
        in_specs=[
            pl.BlockSpec((tm, D_MODEL), lambda i: (i, 0)),
            pl.BlockSpec((1, D_MODEL), const),
            pl.BlockSpec((1, D_MODEL), const),
            pl.BlockSpec(wkv.shape, const),
            pl.BlockSpec(w_in.shape, const),
            pl.BlockSpec((1, GATE_PAD), const),
            pl.BlockSpec((1, NSA_W), const),
            pl.BlockSpec((1, MEM_W), const),
            pl.BlockSpec((2, 128), const),
            pl.BlockSpec((1, MEM_W, MEM_LEN), lambda i: (1, 0, 0)),
            pl.BlockSpec((1, MEM_LEN, MEM_W), lambda i: (1, 0, 0)),
            pl.BlockSpec((MEM_W, MEM_W), const),
            pl.BlockSpec((MEM_HEADS, MEM_W), const),
            pl.BlockSpec((tm, 128), lambda i: (i % pat_blocks, 0)),
        ],
        out_specs=[
            pl.BlockSpec((NSA_HEADS, HEAD_DIM, tm), lambda i: (0, 0, i)),
            pl.BlockSpec((GATE_PAD, tm), lambda i: (0, i)),
            pl.BlockSpec((GROUPS, tm, 128), lambda i: (0, i, 0)),
            pl.BlockSpec((GROUPS, tm, HEAD_DIM), lambda i: (0, i, 0)),
            pl.BlockSpec((GROUPS, VAUG_ROWS, tm), lambda i: (0, 0, i)),
            pl.BlockSpec((GROUPS, VAUG_ROWS, tm), lambda i: (0, 0, i)),
            pl.BlockSpec((2 * GROUPS, tm, HEAD_DIM), lambda i: (0, i, 0)),
            pl.BlockSpec((tm, MEM_W), lambda i: (i, 0)),
        ],
        out_shape=[
            jax.ShapeDtypeStruct((NSA_HEADS, HEAD_DIM, s), BF16),
            jax.ShapeDtypeStruct((GATE_PAD, s), F32),
            jax.ShapeDtypeStruct((GROUPS, s, 128), BF16),
            jax.ShapeDtypeStruct((GROUPS, s, HEAD_DIM), BF16),
            jax.ShapeDtypeStruct((GROUPS, VAUG_ROWS, s), BF16),
            jax.ShapeDtypeStruct((GROUPS, VAUG_ROWS, s), BF16),
            jax.ShapeDtypeStruct((2 * GROUPS, s, HEAD_DIM), BF16),
            jax.ShapeDtypeStruct((s, MEM_W), BF16),
        ],
        compiler_params=pltpu.CompilerParams(
            dimension_semantics=("arbitrary",), vmem_limit_bytes=VMEM_LIMIT),
        name="layer1_proj",
    )(x, gkv, gmix, wkv, w_in, gb, qg, mqg, kng, mkT, mv, bd, hm, pat)


def _compress_kernel(t_ref, w1_ref, w2_ref, w2T_ref, pe_ref, kg_ref, kgT_ref, o_ref, oT_ref, hb_ref):
    nc = t_ref.shape[1]
    half = CMP_STRIDE * HEAD_DIM
    t = t_ref[0]
    ha = _dot(t, w1_ref[0, 0:half, :])
    hb_ref[0:nc, :] = _dot(t, w1_ref[0, half:2 * half, :])
    hb_ref[nc:nc + 8, :] = jnp.zeros((8, CMP_HID), F32)
    pe = jnp.broadcast_to(pe_ref[0], (8, 2 * half)).astype(BF16)
    pe_term = _dot(pe, w1_ref[0])[0:1, :]
    h = ha + hb_ref[1:nc + 1, :] + pe_term
    hg = 0.5 * h * (1.0 + jnp.tanh(0.7978845608028654 * (h + 0.044715 * (h * h * h))))
    hg = hg.astype(BF16)
    o = _dot(hg, w2_ref[0])
    oT = lax.dot_general(w2T_ref[0], hg, (((1,), (1,)), ((), ())), preferred_element_type=F32)
    is_k = pl.program_id(0) < GROUPS
    on = o * lax.rsqrt(jnp.mean(o * o, axis=1, keepdims=True) + EPS) * kg_ref[...]
    oTn = oT * lax.rsqrt(jnp.mean(oT * oT, axis=0, keepdims=True) + EPS) * kgT_ref[...]
    o_ref[0] = jnp.where(is_k, on, o).astype(BF16)
    oT_ref[0, 0:HEAD_DIM, :] = jnp.where(is_k, oTn, oT).astype(BF16)
    oT_ref[0, HEAD_DIM:VAUG_ROWS, :] = (
        lax.broadcasted_iota(jnp.int32, (VAUG_ROWS - HEAD_DIM, nc), 0) == 0).astype(BF16)


def _compress(t, w1, w2, w2T, pe, kg, kgT):
    n, nc, width = t.shape
    return pl.pallas_call(
        _compress_kernel,
        grid=(n,),
        in_specs=[
            pl.BlockSpec((1, nc, width), lambda i: (i, 0, 0)),
            pl.BlockSpec((1, 2 * width, CMP_HID), lambda i: (i // GROUPS, 0, 0)),
            pl.BlockSpec((1, CMP_HID, HEAD_DIM), lambda i: (i // GROUPS, 0, 0)),
            pl.BlockSpec((1, HEAD_DIM, CMP_HID), lambda i: (i // GROUPS, 0, 0)),
            pl.BlockSpec((1, 1, 2 * width), lambda i: (i // GROUPS, 0, 0)),
            pl.BlockSpec((1, HEAD_DIM), lambda i: (0, 0)),
            pl.BlockSpec((HEAD_DIM, 1), lambda i: (0, 0)),
        ],
        out_specs=[
            pl.BlockSpec((1, nc, HEAD_DIM), lambda i: (i, 0, 0)),
            pl.BlockSpec((1, VAUG_ROWS, nc), lambda i: (i, 0, 0)),
        ],
        out_shape=[
            jax.ShapeDtypeStruct((n, nc, HEAD_DIM), BF16),
            jax.ShapeDtypeStruct((n, VAUG_ROWS, nc), BF16),
        ],
        scratch_shapes=[pltpu.VMEM((nc + 8, CMP_HID), F32)],
        compiler_params=pltpu.CompilerParams(
            dimension_semantics=("arbitrary",), vmem_limit_bytes=VMEM_LIMIT),
        name="compress_kv",
    )(t, w1, w2, w2T, pe, kg, kgT)


def _nsa_kernel(qT_ref, gT_ref, kc_ref, vcT_ref, ksel_ref, vselT_ref, kwin_ref, vwinT_ref, out_ref,
                qaug_st, psum_st, bias_st, acc_st, m_st, s_st, oc_st, sw_st, ow_st, pick_st):
    cc = pl.program_id(1)
    nc = kc_ref.shape[1]
    nsb = bias_st.shape[1]
    hg = HEADS_PER_GROUP
    width = hg * Q_BLOCK
    n_chunks = KEY_TILE // KEY_CHUNK
    last = (cc * N_STREAMS) // (KEY_TILE // Q_BLOCK)

    def load_queries(st):
        qaug_ref = qaug_st.at[st]
        for hh in range(hg):
            qaug_ref[0:HEAD_DIM, hh * Q_BLOCK:(hh + 1) * Q_BLOCK] = qT_ref[hh, :, st * Q_BLOCK:(st + 1) * Q_BLOCK]
        qaug_ref[HEAD_DIM:128, :] = jnp.zeros((128 - HEAD_DIM, width), BF16)

    def query_pos(st):
        return (cc * N_STREAMS + st) * Q_BLOCK + (lax.broadcasted_iota(jnp.int32, (1, width), 1) & (Q_BLOCK - 1))

    for st in range(N_STREAMS):
        load_queries(st)

    def cmp_branch(rows):
        for st in range(N_STREAMS):
            qT = qaug_st[st, 0:HEAD_DIM, :]
            t_q = query_pos(st)
            oc_ref, psum_ref = oc_st.at[st], psum_st.at[st]
            s = _dot_3tiles(kc_ref[0, 0:rows, :], qT)
            seen = max(rows - 2 * cuts[0], 0)
            ci = seen + lax.broadcasted_iota(jnp.int32, (rows - seen, 1), 0)
            tail = jnp.where(ci * CMP_STRIDE + (CMP_L - 1) <= t_q, s[seen:, :], -jnp.inf)
            s = jnp.concatenate([s[0:seen, :], tail], axis=0) if seen else tail
            m = jnp.max(s, axis=0, keepdims=True)
            m = jnp.where(m == -jnp.inf, 0.0, m)
            e = jnp.exp2(s - m)
            oa = _dot(vcT_ref[0, :, 0:rows], e.astype(BF16))
            rl = 1.0 / jnp.maximum(oa[HEAD_DIM:HEAD_DIM + 1, :], 1e-30)
            oc_ref[...] = oa[0:HEAD_DIM, :] * rl
            p = e * rl
            psum = p[:, 0:Q_BLOCK]
            for hh in range(1, hg):
                psum = psum + p[:, hh * Q_BLOCK:(hh + 1) * Q_BLOCK]
            psum_ref[8:8 + rows, :] = psum

    @pl.when(cc == 0)
    def _():
        psum_st[...] = jnp.zeros(psum_st.shape, F32)

    cuts = [nc * k // CMP_PATHS for k in range(1, CMP_PATHS + 1)] if nc % (128 * CMP_PATHS) == 0 else [nc]
    need = (cc + 1) * N_STREAMS * (Q_BLOCK // CMP_STRIDE)
    for k, rows in enumerate(cuts):
        lo = cuts[k - 1] if k else 0
        pl.when((need > lo) & (need <= rows))(functools.partial(cmp_branch, rows))

    def block_validity(st, rows):
        blk = lax.broadcasted_iota(jnp.int32, (rows, Q_BLOCK), 0)
        tb = ((cc * N_STREAMS + st) * Q_BLOCK + lax.broadcasted_iota(jnp.int32, (1, Q_BLOCK), 1)) >> 6
        return blk, tb, blk <= tb

    def select_blocks(rows):
        for st in range(N_STREAMS):
            psum_ref = psum_st.at[st]
            imp = (psum_ref[pl.ds(7, rows, stride=4), :] + psum_ref[pl.ds(11, rows, stride=4), :]
                   + 2.0 * (psum_ref[pl.ds(8, rows, stride=4), :] + psum_ref[pl.ds(9, rows, stride=4), :]
                            + psum_ref[pl.ds(10, rows, stride=4), :]))
            blk, tb, valid = block_validity(st, rows)
            forced = (blk == 0) | (blk == tb) | (blk == tb - 1)
            pick_st[st, 0:rows, :] = jnp.where(forced | ~valid, -jnp.inf, imp)
        blk_f = lax.broadcasted_iota(jnp.int32, (rows, Q_BLOCK), 0).astype(F32)

        def pick(_, carry):
            for st in range(N_STREAMS):
                sc = pick_st[st, 0:rows, :]
                mx = jnp.max(sc, axis=0, keepdims=True)
                first = jnp.min(jnp.where(sc == mx, blk_f, float(rows)), axis=0, keepdims=True)
                pick_st[st, 0:rows, :] = jnp.where(blk_f == first, -jnp.inf, sc)
            return carry

        lax.fori_loop(0, N_SEL - 3, pick, 0)
        for st in range(N_STREAMS):
            sel = (pick_st[st, 0:rows, :] == -jnp.inf) & block_validity(st, rows)[2]
            bias_st[st, 0:rows, :] = jnp.where(sel, 0.0, MASK_BIAS).astype(BF16)

    @pl.when(cc == 0)
    def _():
        bias_st[...] = jnp.full(bias_st.shape, MASK_BIAS, BF16)

    b_cuts = [nsb * k // SEL_PATHS for k in range(1, SEL_PATHS + 1)] if nsb % (16 * SEL_PATHS) == 0 else [nsb]
    need_b = 2 * (cc + 1) * N_STREAMS
    for k, rows in enumerate(b_cuts):
        lo = b_cuts[k - 1] if k else 0
        pl.when((need_b > lo) & (need_b <= rows))(functools.partial(select_blocks, rows))

    def stream(st):
        c = cc * N_STREAMS + st
        qs = slice(st * Q_BLOCK, (st + 1) * Q_BLOCK)
        qaug_ref, psum_ref, bias_ref, acc_ref = qaug_st.at[st], psum_st.at[st], bias_st.at[st], acc_st.at[st]
        m_ref, s_ref, oc_ref, sw_ref, ow_ref = m_st.at[st], s_st.at[st], oc_st.at[st], sw_st.at[st], ow_st.at[st]
        qT = qaug_ref[0:HEAD_DIM, :]
        t_q = query_pos(st)

        row0 = pl.multiple_of(jnp.maximum(c * Q_BLOCK - WIN, 0), Q_BLOCK)
        sw = _dot_3tiles(kwin_ref[0, pl.ds(row0, WIN_KEYS), :], qT)
        kp = row0 + lax.broadcasted_iota(jnp.int32, (WIN_KEYS, 1), 0)
        sw = jnp.where((kp <= t_q) & (kp > t_q - WIN), sw, -jnp.inf)
        sw_ref[...] = sw
        mw = jnp.max(jnp.max(sw.reshape(WIN_KEYS // 8, 8, width), axis=0), axis=0, keepdims=True)
        mw = jnp.where(mw == -jnp.inf, 0.0, mw)

        acc_ref[...] = jnp.zeros(acc_ref.shape, F32)
        m_ref[...] = jnp.full(m_ref.shape, M_INIT, F32)

        def set_bias(j):
            span = j // (ONEHOT_SPAN // KEY_TILE)
            b16 = bias_ref[pl.ds(pl.multiple_of(span * SEL_PER_SPAN, SEL_PER_SPAN), SEL_PER_SPAN), :]
            for hh in range(hg):
                qaug_ref[HEAD_DIM:HEAD_DIM + SEL_PER_SPAN, hh * Q_BLOCK:(hh + 1) * Q_BLOCK] = b16

        def scores(j, r):
            k0 = pl.multiple_of(j * KEY_TILE + r * KEY_CHUNK, KEY_CHUNK)
            return _dot(ksel_ref[0, pl.ds(k0, KEY_CHUNK), :], qaug_ref[...])

        def chunk_max(cmax, sc):
            return jnp.maximum(cmax, jnp.max(sc.reshape(KEY_CHUNK // 8, 8, width), axis=0))

        def new_max(cmax):
            m_old = m_ref[...]
            m_new = jnp.maximum(m_old, jnp.max(cmax, axis=0, keepdims=True))
            m_ref[...] = m_new
            return m_new, jnp.exp2(m_old - m_new)

        def values(j, r):
            k0 = pl.multiple_of(j * KEY_TILE + r * KEY_CHUNK, KEY_CHUNK)
            return vselT_ref[0, :, pl.ds(k0, KEY_CHUNK)]

        cmax0 = jnp.full((8, width), MASK_BIAS, F32)
        set_bias(0)
        cmax = cmax0
        ow = jnp.zeros((VAUG_ROWS, width), F32)
        w_cuts = [WIN_KEYS * r // n_chunks // 128 * 128 for r in range(n_chunks)] + [WIN_KEYS]
        for r in range(n_chunks):
            sc = _dot_3tiles(ksel_ref[0, r * KEY_CHUNK:(r + 1) * KEY_CHUNK, :], qaug_ref[...])
            ws = slice(w_cuts[r], w_cuts[r + 1])
            pw = jnp.exp2(sw_ref[ws, :] - mw).astype(BF16)
            s_ref[r * KEY_CHUNK:(r + 1) * KEY_CHUNK, :] = sc
            cmax = chunk_max(cmax, sc)
            ow = ow + _dot(vwinT_ref[0, :, pl.ds(pl.multiple_of(row0 + w_cuts[r], 128), w_cuts[r + 1] - w_cuts[r])], pw)
        ow_ref[...] = ow[0:HEAD_DIM, :] / jnp.maximum(ow[HEAD_DIM:HEAD_DIM + 1, :], 1e-30)

        def pipe_step(j, cmax):
            m_new, alpha = new_max(cmax)
            set_bias(j + 1)
            acc = acc_ref[...] * alpha
            cnext = cmax0
            for r in range(n_chunks):
                rs = slice(r * KEY_CHUNK, (r + 1) * KEY_CHUNK)
                sc = scores(j + 1, r)
                pj = jnp.exp2(s_ref[rs, :] - m_new).astype(BF16)
                s_ref[rs, :] = sc
                cnext = chunk_max(cnext, sc)
                acc = acc + _dot(values(j, r), pj)
            acc_ref[...] = acc
            return cnext

        def finish():
            diag = pl.ds(pl.multiple_of((c % (KEY_TILE // Q_BLOCK)) * Q_BLOCK, Q_BLOCK), Q_BLOCK)
            ku = lax.broadcasted_iota(jnp.int32, (Q_BLOCK, 1), 0)
            s_ref[diag, :] = jnp.where(ku <= t_q - c * Q_BLOCK, s_ref[diag, :], MASK_BIAS)

            def finish_tile(rows):
                s_last = s_ref[0:rows, :]
                m_new, alpha = new_max(jnp.max(s_last.reshape(rows // 8, 8, width), axis=0))
                p_last = jnp.exp2(s_last - m_new).astype(BF16)
                acc_ref[...] = acc_ref[...] * alpha + _dot(
                    vselT_ref[0, :, pl.ds(pl.multiple_of(last * KEY_TILE, KEY_TILE), rows)], p_last)

            diag_chunk = (c % (KEY_TILE // Q_BLOCK)) // (KEY_CHUNK // Q_BLOCK)
            for k in range(n_chunks):
                pl.when(diag_chunk == k)(functools.partial(finish_tile, (k + 1) * KEY_CHUNK))
            o_cmp = oc_ref[...]
            o_win = ow_ref[...]
            o_slc = acc_ref[0:HEAD_DIM, :] / jnp.maximum(acc_ref[HEAD_DIM:HEAD_DIM + 1, :], 1e-30)

            heads = []
            for hh in range(hg):
                hs = slice(hh * Q_BLOCK, (hh + 1) * Q_BLOCK)
                heads.append(o_cmp[:, hs] * gT_ref[hh:hh + 1, qs] + o_slc[:, hs] * gT_ref[8 + hh:9 + hh, qs]
                             + o_win[:, hs] * gT_ref[16 + hh:17 + hh, qs])
            for pr in range(hg // 2):
                pair = jnp.concatenate([heads[2 * pr], heads[2 * pr + 1]], axis=0)
                out_ref[qs, pr * 128:(pr + 1) * 128] = pair.T.astype(BF16)

        return cmax, pipe_step, finish

    streams = [stream(st) for st in range(N_STREAMS)]

    def sweep_steps(j, cmaxes):
        return tuple(step(j, cm) for (_, step, _), cm in zip(streams, cmaxes))

    lax.fori_loop(0, last, sweep_steps, tuple(cm for cm, _, _ in streams))
    for _, _, finish in streams:
        finish()


def _nsa(qT, gT, kc, vcT, ksel, vselT, kwin, vwinT):
    s = ksel.shape[1]
    nc = kc.shape[1]
    nsb = s // SEL_L
    hg = HEADS_PER_GROUP
    ns = N_STREAMS
    qb = ns * Q_BLOCK
    width = hg * Q_BLOCK
    grp = lambda g, c: (g, 0, 0)
    once = pl.Buffered(1)
    return pl.pallas_call(
        _nsa_kernel,
        grid=(GROUPS, s // qb),
        in_specs=[
            pl.BlockSpec((hg, HEAD_DIM, qb), lambda g, c: (g, 0, c)),
            pl.BlockSpec((GATE_ROWS, qb), lambda g, c: (g, c)),
            pl.BlockSpec((1, nc, HEAD_DIM), grp, pipeline_mode=once),
            pl.BlockSpec((1, VAUG_ROWS, nc), lambda g, c: (GROUPS + g, 0, 0), pipeline_mode=once),
            pl.BlockSpec((1, s, 128), grp, pipeline_mode=once),
            pl.BlockSpec((1, VAUG_ROWS, s), grp, pipeline_mode=once),
            pl.BlockSpec((1, s, HEAD_DIM), grp, pipeline_mode=once),
            pl.BlockSpec((1, VAUG_ROWS, s), grp, pipeline_mode=once),
        ],
        out_specs=pl.BlockSpec((qb, hg * HEAD_DIM), lambda g, c: (c, g)),
        out_shape=jax.ShapeDtypeStruct((s, NSA_W), BF16),
        scratch_shapes=[
            pltpu.VMEM((ns, 128, width), BF16),
            pltpu.VMEM((ns, nc + 16, Q_BLOCK), F32),
            pltpu.VMEM((ns, nsb, Q_BLOCK), BF16),
            pltpu.VMEM((ns, VAUG_ROWS, width), F32),
            pltpu.VMEM((ns, 1, width), F32),
            pltpu.VMEM((ns, KEY_TILE, width), F32),
            pltpu.VMEM((ns, HEAD_DIM, width), F32),
            pltpu.VMEM((ns, WIN_KEYS, width), F32),
            pltpu.VMEM((ns, HEAD_DIM, width), F32),
            pltpu.VMEM((ns, nsb, Q_BLOCK), F32),
        ],
        compiler_params=pltpu.CompilerParams(
            dimension_semantics=("arbitrary", "arbitrary"), vmem_limit_bytes=VMEM_LIMIT),
        name="nsa_attention",
    )(qT, gT, kc, vcT, ksel, vselT, kwin, vwinT)


def _block_diag_ones():
    idx = np.arange(MEM_W) // HEAD_DIM
    return jnp.asarray((idx[:, None] == idx[None, :]).astype(np.float32), BF16)


def _head_masks():
    idx = np.arange(MEM_W) // HEAD_DIM
    return jnp.asarray((idx[None, :] == np.arange(MEM_HEADS)[:, None]).astype(np.float32))


def _onehot_pattern():
    rows = np.arange(ONEHOT_SPAN)[:, None] // SEL_L
    lanes = np.arange(128)[None, :] - HEAD_DIM
    return jnp.asarray((rows == lanes).astype(np.float32))


def _gate_layout():
    src = np.full((GATE_PAD,), -1, np.int64)
    for g in range(GROUPS):
        for b in range(3):
            for hh in range(HEADS_PER_GROUP):
                src[g * GATE_ROWS + b * 8 + hh] = (g * HEADS_PER_GROUP + hh) * 3 + b
    return src


def kernel(x, mem, norm_mix_g, norm_mlp_g, mem_norm_g, w_mem_kv, mem_q_norm_g, mem_k_norm_g, w_out, w_mlp_in, w_mlp_out, a_w_in, a_b_glu, a_dw, a_dw_b, a_ln_g, a_ln_b, b_w_in, b_gate_b, b_q_norm_g, kv_norm_g, w_kv, k_norm_g, cmp_pe_k, cmp_pe_v, cmp_w1_k, cmp_w2_k, cmp_w1_v, cmp_w2_v):
    batch, s, _ = x.shape
    assert batch == 1 and s % ONEHOT_SPAN == 0 and s >= WIN_KEYS and (KEY_TILE // Q_BLOCK) % N_STREAMS == 0
    assert w_out.shape[0] == 2 and a_w_in.shape[0] == 1 and b_w_in.shape[0] == 1
    nc = s // CMP_STRIDE
    row = lambda v: v.reshape(1, -1)
    bd = _block_diag_ones()
    hm = _head_masks()

    mkT, mv = _memkv(mem[0], row(mem_norm_g), w_mem_kv.astype(BF16),
                     jnp.tile(mem_k_norm_g, (1, MEM_HEADS))[:, None, :], bd)

    conv0, memo0 = _layer0(
        x[0], row(norm_mix_g[0]), a_w_in[0].astype(BF16), row(a_b_glu[0]), a_dw[0], row(a_dw_b[0]),
        row(a_ln_g[0]), row(a_ln_b[0]), row(jnp.tile(mem_q_norm_g[0], MEM_HEADS)), mkT, mv, bd, hm)
    w_out_bf = w_out.astype(BF16)
    w_in_bf = w_mlp_in.astype(BF16)
    w_o_bf = w_mlp_out.astype(BF16)
    x1 = _post(x[0], conv0, memo0, w_out_bf, row(norm_mlp_g[0]), w_in_bf, w_o_bf, 0)

    src = _gate_layout()
    used = src >= 0
    w_gate = jnp.where(used[None, :], b_w_in[0][:, NSA_W + MEM_W + np.maximum(src, 0)], 0.0)
    b_gate = jnp.where(used, b_gate_b[0][np.maximum(src, 0)], 0.0)
    w_in1 = jnp.concatenate([b_w_in[0][:, :NSA_W + MEM_W], w_gate], axis=1).astype(BF16)
    q_gain = jnp.tile(b_q_norm_g[0], NSA_HEADS) * (QK_SCALE * LOG2E)
    kng = jnp.stack([jnp.tile(k_norm_g[1], GROUPS), jnp.tile(k_norm_g[2], GROUPS)])
    qT, gT, ksel, kwin, vselT, vwinT, kvc, memo1 = _layer1_proj(
        x1, row(kv_norm_g), row(norm_mix_g[1]), w_kv.astype(BF16), w_in1, row(b_gate), row(q_gain),
        row(jnp.tile(mem_q_norm_g[1], MEM_HEADS)), kng, mkT, mv, bd, hm, _onehot_pattern())

    t = kvc.reshape(2 * GROUPS, nc, CMP_STRIDE * HEAD_DIM)
    w1 = jnp.stack([cmp_w1_k, cmp_w1_v]).astype(BF16)
    w2 = jnp.stack([cmp_w2_k, cmp_w2_v]).astype(BF16)
    pe = jnp.stack([cmp_pe_k.reshape(1, -1), cmp_pe_v.reshape(1, -1)])
    cmp_rows, cmp_cols = _compress(t, w1, w2, w2.transpose(0, 2, 1), pe, row(k_norm_g[0]),
                                   k_norm_g[0].reshape(-1, 1))

    nsa = _nsa(qT, gT, cmp_rows, cmp_cols, ksel, vselT, kwin, vwinT)
    x2 = _post(x1, nsa, memo1, w_out_bf, row(norm_mlp_g[1]), w_in_bf, w_o_bf, 1)
    return x2[None]
```

```python
import functools

import numpy as np
import jax
import jax.numpy as jnp
from jax import lax
from jax.experimental import pallas as pl
from jax.experimental.pallas import tpu as pltpu

F32 = jnp.float32
BF16 = jnp.bfloat16

D_MODEL = 1024
HEAD_DIM = 64
MEM_LEN = 256
MEM_HEADS = 4
MEM_W = MEM_HEADS * HEAD_DIM
CONV_W = D_MODEL - MEM_W
CONV_K = 31
NSA_HEADS = CONV_W // HEAD_DIM
NSA_W = NSA_HEADS * HEAD_DIM
GROUPS = 2
HEADS_PER_GROUP = NSA_HEADS // GROUPS
CMP_L = 32
CMP_STRIDE = 16
CMP_HID = 256
SEL_L = 64
N_SEL = 16
WIN = 512
Q_BLOCK = 128
D_FF = 4 * D_MODEL
GATE_PAD = 128
GATE_ROWS = GATE_PAD // GROUPS

ROW_TILE = 512
FF_CHUNK = 1024
KEY_TILE = 1024
ONEHOT_SPAN = 1024
CMP_PATHS = 4
SEL_PATHS = 4
N_STREAMS = 4
KEY_CHUNK = 256
SEL_PER_SPAN = ONEHOT_SPAN // SEL_L
WIN_KEYS = WIN + Q_BLOCK
CONV_HALO = 32
CONV_ROWS = 64
VAUG_ROWS = 80
VMEM_LIMIT = 56 * 1024 * 1024
MXU_WIDTH = 256

EPS = 1e-6
LN_EPS = 1e-5
LOG2E = 1.4426950408889634
QK_SCALE = HEAD_DIM ** -0.5
MASK_BIAS = -2.0 ** 126
M_INIT = -2.0 ** 100


def _dot(a, b):
    return jnp.dot(a, b, preferred_element_type=F32)


def _dot_3tiles(a, b):
    two = 2 * MXU_WIDTH
    half = a.shape[0] // 2
    return jnp.concatenate([
        _dot(a, b[:, 0:two]),
        jnp.concatenate([_dot(a[0:half], b[:, two:]), _dot(a[half:], b[:, two:])], axis=0),
    ], axis=1)


def _sigmoid(x):
    return 0.5 * jnp.tanh(0.5 * x) + 0.5


def _rms_scale(x):
    return lax.rsqrt(jnp.mean(x * x, axis=-1, keepdims=True) + EPS)


def _head_meansq(x, bd):
    x2 = x * x
    hi = x2.astype(BF16)
    lo = (x2 - hi.astype(F32)).astype(BF16)
    return (_dot(hi, bd) + _dot(lo, bd)) * (1.0 / HEAD_DIM)


def _mem_attention(qn, mkT, mv, hm_ref):
    out = jnp.zeros(qn.shape, F32)
    for h in range(MEM_HEADS):
        hm = hm_ref[h:h + 1, :]
        s = _dot((qn * hm).astype(BF16), mkT)
        e = jnp.exp2(s - jnp.max(s, axis=-1, keepdims=True))
        l = jnp.sum(e, axis=-1, keepdims=True)
        out = out + _dot(e.astype(BF16), mv) * (hm / l)
    return out


def _memkv_kernel(mem_ref, g_ref, w_ref, kg_ref, bd_ref, mkT_ref, mv_ref):
    m = mem_ref[...]
    mn = (m * _rms_scale(m) * g_ref[...]).astype(BF16)
    kv = _dot(mn, w_ref[0])
    k = kv[:, :MEM_W]
    kn = k * lax.rsqrt(_head_meansq(k, bd_ref[...]) + EPS) * kg_ref[0]
    mkT_ref[0] = (kn * (QK_SCALE * LOG2E)).T.astype(BF16)
    mv_ref[0] = kv[:, MEM_W:].astype(BF16)


def _memkv(mem, g, w, kg, bd):
    depth = w.shape[0]
    return pl.pallas_call(
        _memkv_kernel,
        grid=(depth,),
        in_specs=[
            pl.BlockSpec((MEM_LEN, D_MODEL), lambda l: (0, 0)),
            pl.BlockSpec((1, D_MODEL), lambda l: (0, 0)),
            pl.BlockSpec((1, D_MODEL, 2 * MEM_W), lambda l: (l, 0, 0)),
            pl.BlockSpec((1, 1, MEM_W), lambda l: (l, 0, 0)),
            pl.BlockSpec((MEM_W, MEM_W), lambda l: (0, 0)),
        ],
        out_specs=[
            pl.BlockSpec((1, MEM_W, MEM_LEN), lambda l: (l, 0, 0)),
            pl.BlockSpec((1, MEM_LEN, MEM_W), lambda l: (l, 0, 0)),
        ],
        out_shape=[
            jax.ShapeDtypeStruct((depth, MEM_W, MEM_LEN), BF16),
            jax.ShapeDtypeStruct((depth, MEM_LEN, MEM_W), BF16),
        ],
        name="mem_kv",
    )(mem, g, w, kg, bd)


def _layer0_kernel(x_ref, g_ref, win_ref, bglu_ref, dw_ref, dwb_ref, lng_ref, lnb_ref, qg_ref,
                   mkT_ref, mv_ref, bd_ref, hm_ref, conv_ref, memo_ref, buf_ref, cv_ref, sh_ref):
    i = pl.program_id(0)
    tm = x_ref.shape[0]
    x = x_ref[...]
    h = (x * _rms_scale(x) * g_ref[...]).astype(BF16)
    u = _dot(h, win_ref[...])
    a = u[:, :CONV_W] + bglu_ref[:, :CONV_W]
    gate = u[:, CONV_W:2 * CONV_W] + bglu_ref[:, CONV_W:]
    v = a * _sigmoid(gate)

    @pl.when(i == 0)
    def _():
        buf_ref[0:CONV_HALO, :] = jnp.zeros((CONV_HALO, CONV_W), F32)

    buf_ref[CONV_HALO:CONV_HALO + tm, :] = v

    base = CONV_HALO - (CONV_K - 1)
    rows = tm + CONV_HALO
    for cb in range(CONV_W // 128):
        cs = slice(cb * 128, (cb + 1) * 128)
        xb = buf_ref[:, cs]
        sh_ref[0] = xb
        for b in range(1, 8):
            sh_ref[b] = pltpu.roll(xb, rows - b, axis=0)

        def conv_rows(r, carry, cs=cs):
            r0 = pl.multiple_of(r * CONV_ROWS, CONV_ROWS)
            acc = jnp.zeros((CONV_ROWS, 128), F32) + dwb_ref[:, cs]
            for k in range(CONV_K):
                a, b = divmod(base + k, 8)
                acc = acc + dw_ref[k:k + 1, cs] * sh_ref[b, pl.ds(r0 + 8 * a, CONV_ROWS), :]
            cv_ref[pl.ds(r0, CONV_ROWS), cs] = acc
            return carry

        lax.fori_loop(0, tm // CONV_ROWS, conv_rows, 0)
    buf_ref[0:CONV_HALO, :] = buf_ref[tm:tm + CONV_HALO, :]

    cv = cv_ref[...]
    mu = jnp.mean(cv, axis=-1, keepdims=True)
    d = cv - mu
    var = jnp.mean(d * d, axis=-1, keepdims=True)
    y = d * lax.rsqrt(var + LN_EPS) * lng_ref[...] + lnb_ref[...]
    conv_ref[...] = (y * _sigmoid(y)).astype(BF16)

    qm = u[:, 2 * CONV_W:]
    qn = qm * lax.rsqrt(_head_meansq(qm, bd_ref[...]) + EPS) * qg_ref[...]
    memo_ref[...] = _mem_attention(qn, mkT_ref[0], mv_ref[0], hm_ref).astype(BF16)


def _layer0(x, g, w_in, bglu, dw, dwb, lng, lnb, qg, mkT, mv, bd, hm):
    s = x.shape[0]
    tm = ROW_TILE
    const = lambda i: (0, 0)
    return pl.pallas_call(
        _layer0_kernel,
        grid=(s // tm,),
        in_specs=[
            pl.BlockSpec((tm, D_MODEL), lambda i: (i, 0)),
            pl.BlockSpec((1, D_MODEL), const),
            pl.BlockSpec(w_in.shape, const),
            pl.BlockSpec((1, 2 * CONV_W), const),
            pl.BlockSpec((CONV_K, CONV_W), const),
            pl.BlockSpec((1, CONV_W), const),
            pl.BlockSpec((1, CONV_W), const),
            pl.BlockSpec((1, CONV_W), const),
            pl.BlockSpec((1, MEM_W), const),
            pl.BlockSpec((1, MEM_W, MEM_LEN), lambda i: (0, 0, 0)),
            pl.BlockSpec((1, MEM_LEN, MEM_W), lambda i: (0, 0, 0)),
            pl.BlockSpec((MEM_W, MEM_W), const),
            pl.BlockSpec((MEM_HEADS, MEM_W), const),
        ],
        out_specs=[
            pl.BlockSpec((tm, CONV_W), lambda i: (i, 0)),
            pl.BlockSpec((tm, MEM_W), lambda i: (i, 0)),
        ],
        out_shape=[
            jax.ShapeDtypeStruct((s, CONV_W), BF16),
            jax.ShapeDtypeStruct((s, MEM_W), BF16),
        ],
        scratch_shapes=[
            pltpu.VMEM((tm + CONV_HALO, CONV_W), F32),
            pltpu.VMEM((tm, CONV_W), F32),
            pltpu.VMEM((8, tm + CONV_HALO, 128), F32),
        ],
        compiler_params=pltpu.CompilerParams(
            dimension_semantics=("arbitrary",), vmem_limit_bytes=VMEM_LIMIT),
        name="layer0_mixer",
    )(x, g, w_in, bglu, dw, dwb, lng, lnb, qg, mkT, mv, bd, hm)


def _post_kernel(x_ref, ma_ref, mb_ref, wa_ref, wb_ref, g_ref, win_ref, wout_ref, o_ref):
    o_ref[...] = x_ref[...] + _dot(ma_ref[...], wa_ref[0]) + _dot(mb_ref[...], wb_ref[0])
    x1 = o_ref[...]
    h = (x1 * _rms_scale(x1) * g_ref[...]).astype(BF16)
    for c in range(D_FF // FF_CHUNK):
        cs = slice(c * FF_CHUNK, (c + 1) * FF_CHUNK)
        t = jnp.maximum(_dot(h, win_ref[0, :, cs]), 0.0)
        o_ref[...] += _dot((t * t).astype(BF16), wout_ref[0, cs, :])


def _post(x, ma, mb, w_o, g, w_in, w_out, layer):
    s = x.shape[0]
    tm = ROW_TILE
    const = lambda i: (0, 0)
    this = lambda i: (layer, 0, 0)
    single = pl.Buffered(1)
    return pl.pallas_call(
        _post_kernel,
        grid=(s // tm,),
        in_specs=[
            pl.BlockSpec((tm, D_MODEL), lambda i: (i, 0)),
            pl.BlockSpec((tm, CONV_W), lambda i: (i, 0)),
            pl.BlockSpec((tm, MEM_W), lambda i: (i, 0)),
            pl.BlockSpec((1, CONV_W, D_MODEL), this, pipeline_mode=single),
            pl.BlockSpec((1, MEM_W, D_MODEL), lambda i: (layer, CONV_W // MEM_W, 0), pipeline_mode=single),
            pl.BlockSpec((1, D_MODEL), const),
            pl.BlockSpec((1, D_MODEL, D_FF), this, pipeline_mode=single),
            pl.BlockSpec((1, D_FF, D_MODEL), this, pipeline_mode=single),
        ],
        out_specs=pl.BlockSpec((tm, D_MODEL), lambda i: (i, 0)),
        out_shape=jax.ShapeDtypeStruct((s, D_MODEL), F32),
        compiler_params=pltpu.CompilerParams(
            dimension_semantics=("arbitrary",), vmem_limit_bytes=VMEM_LIMIT),
        name="outproj_mlp",
    )(x, ma, mb, w_o, w_o, g, w_in, w_out)


def _layer1_proj_kernel(x_ref, gkv_ref, gmix_ref, wkv_ref, win_ref, gb_ref, qg_ref, mqg_ref, kng_ref,
                        mkT_ref, mv_ref, bd_ref, hm_ref, pat_ref,
                        qT_ref, gT_ref, ksel_ref, kwin_ref, vselT_ref, vwinT_ref, kvc_ref, memo_ref):
    tm = x_ref.shape[0]
    x = x_ref[...]
    xn = x * _rms_scale(x)
    bd = bd_ref[...]

    kv = _dot((xn * gkv_ref[...]).astype(BF16), wkv_ref[...])
    for i in range(2 * GROUPS):
        kvc_ref[i] = kv[:, i * HEAD_DIM:(i + 1) * HEAD_DIM].astype(BF16)
    k2 = kv[:, 256:384]
    v2 = kv[:, 384:512]
    kw = kv[:, 512:640]
    vw = kv[:, 640:768]
    bd2 = bd[0:128, 0:128]
    k2n = k2 * lax.rsqrt(_head_meansq(k2, bd2) + EPS) * kng_ref[0:1, :]
    kwn = kw * lax.rsqrt(_head_meansq(kw, bd2) + EPS) * kng_ref[1:2, :]

    lane = lax.broadcasted_iota(jnp.int32, (tm, 128), 1)
    pat = pat_ref[...]
    ksel_ref[0] = jnp.where(lane < HEAD_DIM, k2n, pat).astype(BF16)
    ksel_ref[1] = jnp.where(lane < HEAD_DIM, pltpu.roll(k2n, HEAD_DIM, axis=1), pat).astype(BF16)
    kwin_ref[0] = kwn[:, :HEAD_DIM].astype(BF16)
    kwin_ref[1] = kwn[:, HEAD_DIM:].astype(BF16)

    ones_rows = (lax.broadcasted_iota(jnp.int32, (VAUG_ROWS - HEAD_DIM, tm), 0) == 0).astype(BF16)
    v2T = v2.T.astype(BF16)
    vwT = vw.T.astype(BF16)
    for g in range(GROUPS):
        vselT_ref[g, 0:HEAD_DIM, :] = v2T[g * HEAD_DIM:(g + 1) * HEAD_DIM, :]
        vselT_ref[g, HEAD_DIM:VAUG_ROWS, :] = ones_rows
        vwinT_ref[g, 0:HEAD_DIM, :] = vwT[g * HEAD_DIM:(g + 1) * HEAD_DIM, :]
        vwinT_ref[g, HEAD_DIM:VAUG_ROWS, :] = ones_rows

    u = _dot((xn * gmix_ref[...]).astype(BF16), win_ref[...])
    for cb in range(NSA_W // 256):
        cs = slice(cb * 256, (cb + 1) * 256)
        qc = u[:, cs]
        qn = qc * lax.rsqrt(_head_meansq(qc, bd) + EPS) * qg_ref[:, cs]
        qT_ref[4 * cb:4 * cb + 4] = qn.T.astype(BF16).reshape(4, HEAD_DIM, tm)

    gates = _sigmoid(u[:, NSA_W + MEM_W:] + gb_ref[...])
    gT_ref[...] = gates.T

    qm = u[:, NSA_W:NSA_W + MEM_W]
    qmn = qm * lax.rsqrt(_head_meansq(qm, bd) + EPS) * mqg_ref[...]
    memo_ref[...] = _mem_attention(qmn, mkT_ref[0], mv_ref[0], hm_ref).astype(BF16)


def _layer1_proj(x, gkv, gmix, wkv, w_in, gb, qg, mqg, kng, mkT, mv, bd, hm, pat):
    s = x.shape[0]
    tm = ROW_TILE
    const = lambda i: (0, 0)
    pat_blocks = pat.shape[0] // tm
    return pl.pallas_call(
        _layer1_proj_kernel,
        grid=(s // tm,),
        in_specs=[
            pl.BlockSpec((tm, D_MODEL), lambda i: (i, 0)),
            pl.BlockSpec((1, D_MODEL), const),
            pl.BlockSpec((1, D_MODEL), const),
            pl.BlockSpec(wkv.shape, const),
            pl.BlockSpec(w_in.shape, const),
            pl.BlockSpec((1, GATE_PAD), const),
            pl.BlockSpec((1, NSA_W), const),
            pl.BlockSpec((1, MEM_W), const),
            pl.BlockSpec((2, 128), const),
            pl.BlockSpec((1, MEM_W, MEM_LEN), lambda i: (1, 0, 0)),
            pl.BlockSpec((1, MEM_LEN, MEM_W), lambda i: (1, 0, 0)),
            pl.BlockSpec((MEM_W, MEM_W), const),
            pl.BlockSpec((MEM_HEADS, MEM_W), const),
            pl.BlockSpec((tm, 128), lambda i: (i % pat_blocks, 0)),
        ],
        out_specs=[
            pl.BlockSpec((NSA_HEADS, HEAD_DIM, tm), lambda i: (0, 0, i)),
            pl.BlockSpec((GATE_PAD, tm), lambda i: (0, i)),
            pl.BlockSpec((GROUPS, tm, 128), lambda i: (0, i, 0)),
            pl.BlockSpec((GROUPS, tm, HEAD_DIM), lambda i: (0, i, 0)),
            pl.BlockSpec((GROUPS, VAUG_ROWS, tm), lambda i: (0, 0, i)),
            pl.BlockSpec((GROUPS, VAUG_ROWS, tm), lambda i: (0, 0, i)),
            pl.BlockSpec((2 * GROUPS, tm, HEAD_DIM), lambda i: (0, i, 0)),
            pl.BlockSpec((tm, MEM_W), lambda i: (i, 0)),
        ],
        out_shape=[
            jax.ShapeDtypeStruct((NSA_HEADS, HEAD_DIM, s), BF16),
            jax.ShapeDtypeStruct((GATE_PAD, s), F32),
            jax.ShapeDtypeStruct((GROUPS, s, 128), BF16),
            jax.ShapeDtypeStruct((GROUPS, s, HEAD_DIM), BF16),
            jax.ShapeDtypeStruct((GROUPS, VAUG_ROWS, s), BF16),
            jax.ShapeDtypeStruct((GROUPS, VAUG_ROWS, s), BF16),
            jax.ShapeDtypeStruct((2 * GROUPS, s, HEAD_DIM), BF16),
            jax.ShapeDtypeStruct((s, MEM_W), BF16),
        ],
        compiler_params=pltpu.CompilerParams(
            dimension_semantics=("arbitrary",), vmem_limit_bytes=VMEM_LIMIT),
        name="layer1_proj",
    )(x, gkv, gmix, wkv, w_in, gb, qg, mqg, kng, mkT, mv, bd, hm, pat)


def _compress_kernel(t_ref, w1_ref, w2_ref, w2T_ref, pe_ref, kg_ref, kgT_ref, o_ref, oT_ref, hb_ref):
    nc = t_ref.shape[1]
    half = CMP_STRIDE * HEAD_DIM
    t = t_ref[0]
    ha = _dot(t, w1_ref[0, 0:half, :])
    hb_ref[0:nc, :] = _dot(t, w1_ref[0, half:2 * half, :])
    hb_ref[nc:nc + 8, :] = jnp.zeros((8, CMP_HID), F32)
    pe = jnp.broadcast_to(pe_ref[0], (8, 2 * half)).astype(BF16)
    pe_term = _dot(pe, w1_ref[0])[0:1, :]
    h = ha + hb_ref[1:nc + 1, :] + pe_term
    hg = 0.5 * h * (1.0 + jnp.tanh(0.7978845608028654 * (h + 0.044715 * (h * h * h))))
    hg = hg.astype(BF16)
    o = _dot(hg, w2_ref[0])
    oT = lax.dot_general(w2T_ref[0], hg, (((1,), (1,)), ((), ())), preferred_element_type=F32)
    is_k = pl.program_id(0) < GROUPS
    on = o * lax.rsqrt(jnp.mean(o * o, axis=1, keepdims=True) + EPS) * kg_ref[...]
    oTn = oT * lax.rsqrt(jnp.mean(oT * oT, axis=0, keepdims=True) + EPS) * kgT_ref[...]
    o_ref[0] = jnp.where(is_k, on, o).astype(BF16)
    oT_ref[0, 0:HEAD_DIM, :] = jnp.where(is_k, oTn, oT).astype(BF16)
    oT_ref[0, HEAD_DIM:VAUG_ROWS, :] = (
        lax.broadcasted_iota(jnp.int32, (VAUG_ROWS - HEAD_DIM, nc), 0) == 0).astype(BF16)


def _compress(t, w1, w2, w2T, pe, kg, kgT):
    n, nc, width = t.shape
    return pl.pallas_call(
        _compress_kernel,
        grid=(n,),
        in_specs=[
            pl.BlockSpec((1, nc, width), lambda i: (i, 0, 0)),
            pl.BlockSpec((1, 2 * width, CMP_HID), lambda i: (i // GROUPS, 0, 0)),
            pl.BlockSpec((1, CMP_HID, HEAD_DIM), lambda i: (i // GROUPS, 0, 0)),
            pl.BlockSpec((1, HEAD_DIM, CMP_HID), lambda i: (i // GROUPS, 0, 0)),
            pl.BlockSpec((1, 1, 2 * width), lambda i: (i // GROUPS, 0, 0)),
            pl.BlockSpec((1, HEAD_DIM), lambda i: (0, 0)),
            pl.BlockSpec((HEAD_DIM, 1), lambda i: (0, 0)),
        ],
        out_specs=[
            pl.BlockSpec((1, nc, HEAD_DIM), lambda i: (i, 0, 0)),
            pl.BlockSpec((1, VAUG_ROWS, nc), lambda i: (i, 0, 0)),
        ],
        out_shape=[
            jax.ShapeDtypeStruct((n, nc, HEAD_DIM), BF16),
            jax.ShapeDtypeStruct((n, VAUG_ROWS, nc), BF16),
        ],
        scratch_shapes=[pltpu.VMEM((nc + 8, CMP_HID), F32)],
        compiler_params=pltpu.CompilerParams(
            dimension_semantics=("arbitrary",), vmem_limit_bytes=VMEM_LIMIT),
        name="compress_kv",
    )(t, w1, w2, w2T, pe, kg, kgT)


def _nsa_kernel(qT_ref, gT_ref, kc_ref, vcT_ref, ksel_ref, vselT_ref, kwin_ref, vwinT_ref, out_ref,
                qaug_st, psum_st, bias_st, acc_st, m_st, s_st, oc_st, sw_st, ow_st, pick_st):
    cc = pl.program_id(1)
    nc = kc_ref.shape[1]
    nsb = bias_st.shape[1]
    hg = HEADS_PER_GROUP
    width = hg * Q_BLOCK
    n_chunks = KEY_TILE // KEY_CHUNK
    last = (cc * N_STREAMS) // (KEY_TILE // Q_BLOCK)

    def load_queries(st):
        qaug_ref = qaug_st.at[st]
        for hh in range(hg):
            qaug_ref[0:HEAD_DIM, hh * Q_BLOCK:(hh + 1) * Q_BLOCK] = qT_ref[hh, :, st * Q_BLOCK:(st + 1) * Q_BLOCK]
        qaug_ref[HEAD_DIM:128, :] = jnp.zeros((128 - HEAD_DIM, width), BF16)

    def query_pos(st):
        return (cc * N_STREAMS + st) * Q_BLOCK + (lax.broadcasted_iota(jnp.int32, (1, width), 1) & (Q_BLOCK - 1))

    for st in range(N_STREAMS):
        load_queries(st)

    def cmp_branch(rows):
        for st in range(N_STREAMS):
            qT = qaug_st[st, 0:HEAD_DIM, :]
            t_q = query_pos(st)
            oc_ref, psum_ref = oc_st.at[st], psum_st.at[st]
            s = _dot_3tiles(kc_ref[0, 0:rows, :], qT)
            seen = max(rows - 2 * cuts[0], 0)
            ci = seen + lax.broadcasted_iota(jnp.int32, (rows - seen, 1), 0)
            tail = jnp.where(ci * CMP_STRIDE + (CMP_L - 1) <= t_q, s[seen:, :], -jnp.inf)
            s = jnp.concatenate([s[0:seen, :], tail], axis=0) if seen else tail
            m = jnp.max(s, axis=0, keepdims=True)
            m = jnp.where(m == -jnp.inf, 0.0, m)
            e = jnp.exp2(s - m)
            oa = _dot(vcT_ref[0, :, 0:rows], e.astype(BF16))
            rl = 1.0 / jnp.maximum(oa[HEAD_DIM:HEAD_DIM + 1, :], 1e-30)
            oc_ref[...] = oa[0:HEAD_DIM, :] * rl
            p = e * rl
            psum = p[:, 0:Q_BLOCK]
            for hh in range(1, hg):
                psum = psum + p[:, hh * Q_BLOCK:(hh + 1) * Q_BLOCK]
            psum_ref[8:8 + rows, :] = psum

    @pl.when(cc == 0)
    def _():
        psum_st[...] = jnp.zeros(psum_st.shape, F32)

    cuts = [nc * k // CMP_PATHS for k in range(1, CMP_PATHS + 1)] if nc % (128 * CMP_PATHS) == 0 else [nc]
    need = (cc + 1) * N_STREAMS * (Q_BLOCK // CMP_STRIDE)
    for k, rows in enumerate(cuts):
        lo = cuts[k - 1] if k else 0
        pl.when((need > lo) & (need <= rows))(functools.partial(cmp_branch, rows))

    def block_validity(st, rows):
        blk = lax.broadcasted_iota(jnp.int32, (rows, Q_BLOCK), 0)
        tb = ((cc * N_STREAMS + st) * Q_BLOCK + lax.broadcasted_iota(jnp.int32, (1, Q_BLOCK), 1)) >> 6
        return blk, tb, blk <= tb

    def select_blocks(rows):
        for st in range(N_STREAMS):
            psum_ref = psum_st.at[st]
            imp = (psum_ref[pl.ds(7, rows, stride=4), :] + psum_ref[pl.ds(11, rows, stride=4), :]
                   + 2.0 * (psum_ref[pl.ds(8, rows, stride=4), :] + psum_ref[pl.ds(9, rows, stride=4), :]
                            + psum_ref[pl.ds(10, rows, stride=4), :]))
            blk, tb, valid = block_validity(st, rows)
            forced = (blk == 0) | (blk == tb) | (blk == tb - 1)
            pick_st[st, 0:rows, :] = jnp.where(forced | ~valid, -jnp.inf, imp)
        blk_f = lax.broadcasted_iota(jnp.int32, (rows, Q_BLOCK), 0).astype(F32)

        def pick(_, carry):
            for st in range(N_STREAMS):
                sc = pick_st[st, 0:rows, :]
                mx = jnp.max(sc, axis=0, keepdims=True)
                first = jnp.min(jnp.where(sc == mx, blk_f, float(rows)), axis=0, keepdims=True)
                pick_st[st, 0:rows, :] = jnp.where(blk_f == first, -jnp.inf, sc)
            return carry

        lax.fori_loop(0, N_SEL - 3, pick, 0)
        for st in range(N_STREAMS):
            sel = (pick_st[st, 0:rows, :] == -jnp.inf) & block_validity(st, rows)[2]
            bias_st[st, 0:rows, :] = jnp.where(sel, 0.0, MASK_BIAS).astype(BF16)

    @pl.when(cc == 0)
    def _():
        bias_st[...] = jnp.full(bias_st.shape, MASK_BIAS, BF16)

    b_cuts = [nsb * k // SEL_PATHS for k in range(1, SEL_PATHS + 1)] if nsb % (16 * SEL_PATHS) == 0 else [nsb]
    need_b = 2 * (cc + 1) * N_STREAMS
    for k, rows in enumerate(b_cuts):
        lo = b_cuts[k - 1] if k else 0
        pl.when((need_b > lo) & (need_b <= rows))(functools.partial(select_blocks, rows))

    def stream(st):
        c = cc * N_STREAMS + st
        qs = slice(st * Q_BLOCK, (st + 1) * Q_BLOCK)
        qaug_ref, psum_ref, bias_ref, acc_ref = qaug_st.at[st], psum_st.at[st], bias_st.at[st], acc_st.at[st]
        m_ref, s_ref, oc_ref, sw_ref, ow_ref = m_st.at[st], s_st.at[st], oc_st.at[st], sw_st.at[st], ow_st.at[st]
        qT = qaug_ref[0:HEAD_DIM, :]
        t_q = query_pos(st)

        row0 = pl.multiple_of(jnp.maximum(c * Q_BLOCK - WIN, 0), Q_BLOCK)
        sw = _dot_3tiles(kwin_ref[0, pl.ds(row0, WIN_KEYS), :], qT)
        kp = row0 + lax.broadcasted_iota(jnp.int32, (WIN_KEYS, 1), 0)
        sw = jnp.where((kp <= t_q) & (kp > t_q - WIN), sw, -jnp.inf)
        sw_ref[...] = sw
        mw = jnp.max(jnp.max(sw.reshape(WIN_KEYS // 8, 8, width), axis=0), axis=0, keepdims=True)
        mw = jnp.where(mw == -jnp.inf, 0.0, mw)

        acc_ref[...] = jnp.zeros(acc_ref.shape, F32)
        m_ref[...] = jnp.full(m_ref.shape, M_INIT, F32)

        def set_bias(j):
            span = j // (ONEHOT_SPAN // KEY_TILE)
            b16 = bias_ref[pl.ds(pl.multiple_of(span * SEL_PER_SPAN, SEL_PER_SPAN), SEL_PER_SPAN), :]
            for hh in range(hg):
                qaug_ref[HEAD_DIM:HEAD_DIM + SEL_PER_SPAN, hh * Q_BLOCK:(hh + 1) * Q_BLOCK] = b16

        def scores(j, r):
            k0 = pl.multiple_of(j * KEY_TILE + r * KEY_CHUNK, KEY_CHUNK)
            return _dot(ksel_ref[0, pl.ds(k0, KEY_CHUNK), :], qaug_ref[...])

        def chunk_max(cmax, sc):
            return jnp.maximum(cmax, jnp.max(sc.reshape(KEY_CHUNK // 8, 8, width), axis=0))

        def new_max(cmax):
            m_old = m_ref[...]
            m_new = jnp.maximum(m_old, jnp.max(cmax, axis=0, keepdims=True))
            m_ref[...] = m_new
            return m_new, jnp.exp2(m_old - m_new)

        def values(j, r):
            k0 = pl.multiple_of(j * KEY_TILE + r * KEY_CHUNK, KEY_CHUNK)
            return vselT_ref[0, :, pl.ds(k0, KEY_CHUNK)]

        cmax0 = jnp.full((8, width), MASK_BIAS, F32)
        set_bias(0)
        cmax = cmax0
        ow = jnp.zeros((VAUG_ROWS, width), F32)
        w_cuts = [WIN_KEYS * r // n_chunks // 128 * 128 for r in range(n_chunks)] + [WIN_KEYS]
        for r in range(n_chunks):
            sc = _dot_3tiles(ksel_ref[0, r * KEY_CHUNK:(r + 1) * KEY_CHUNK, :], qaug_ref[...])
            ws = slice(w_cuts[r], w_cuts[r + 1])
            pw = jnp.exp2(sw_ref[ws, :] - mw).astype(BF16)
            s_ref[r * KEY_CHUNK:(r + 1) * KEY_CHUNK, :] = sc
            cmax = chunk_max(cmax, sc)
            ow = ow + _dot(vwinT_ref[0, :, pl.ds(pl.multiple_of(row0 + w_cuts[r], 128), w_cuts[r + 1] - w_cuts[r])], pw)
        ow_ref[...] = ow[0:HEAD_DIM, :] / jnp.maximum(ow[HEAD_DIM:HEAD_DIM + 1, :], 1e-30)

        def pipe_step(j, cmax):
            m_new, alpha = new_max(cmax)
            set_bias(j + 1)
            acc = acc_ref[...] * alpha
            cnext = cmax0
            for r in range(n_chunks):
                rs = slice(r * KEY_CHUNK, (r + 1) * KEY_CHUNK)
                sc = scores(j + 1, r)
                pj = jnp.exp2(s_ref[rs, :] - m_new).astype(BF16)
                s_ref[rs, :] = sc
                cnext = chunk_max(cnext, sc)
                acc = acc + _dot(values(j, r), pj)
            acc_ref[...] = acc
            return cnext

        def mask_diagonal():
            diag = pl.ds(pl.multiple_of((c % (KEY_TILE // Q_BLOCK)) * Q_BLOCK, Q_BLOCK), Q_BLOCK)
            ku = lax.broadcasted_iota(jnp.int32, (Q_BLOCK, 1), 0)
            s_ref[diag, :] = jnp.where(ku <= t_q - c * Q_BLOCK, s_ref[diag, :], MASK_BIAS)

        def finish_tile(rows):
            s_last = s_ref[0:rows, :]
            m_new, alpha = new_max(jnp.max(s_last.reshape(rows // 8, 8, width), axis=0))
            p_last = jnp.exp2(s_last - m_new).astype(BF16)
            acc_ref[...] = acc_ref[...] * alpha + _dot(
                vselT_ref[0, :, pl.ds(pl.multiple_of(last * KEY_TILE, KEY_TILE), rows)], p_last)

        def emit():
            o_cmp = oc_ref[...]
            o_win = ow_ref[...]
            o_slc = acc_ref[0:HEAD_DIM, :] / jnp.maximum(acc_ref[HEAD_DIM:HEAD_DIM + 1, :], 1e-30)

            heads = []
            for hh in range(hg):
                hs = slice(hh * Q_BLOCK, (hh + 1) * Q_BLOCK)
                heads.append(o_cmp[:, hs] * gT_ref[hh:hh + 1, qs] + o_slc[:, hs] * gT_ref[8 + hh:9 + hh, qs]
                             + o_win[:, hs] * gT_ref[16 + hh:17 + hh, qs])
            for pr in range(hg // 2):
                pair = jnp.concatenate([heads[2 * pr], heads[2 * pr + 1]], axis=0)
                out_ref[qs, pr * 128:(pr + 1) * 128] = pair.T.astype(BF16)

        return cmax, pipe_step, mask_diagonal, finish_tile, emit

    streams = [stream(st) for st in range(N_STREAMS)]

    def sweep_steps(j, cmaxes):
        return tuple(s[1](j, cm) for s, cm in zip(streams, cmaxes))

    lax.fori_loop(0, last, sweep_steps, tuple(s[0] for s in streams))
    for s in streams:
        s[2]()
    per = KEY_CHUNK // Q_BLOCK
    for first in range(0, N_STREAMS, per):
        diag_chunk = ((cc * N_STREAMS + first) % (KEY_TILE // Q_BLOCK)) // per

        def finish_group(rows, first=first):
            for s in streams[first:first + per]:
                s[3](rows)

        for k in range(n_chunks):
            pl.when(diag_chunk == k)(functools.partial(finish_group, (k + 1) * KEY_CHUNK))
    for s in streams:
        s[4]()


def _nsa(qT, gT, kc, vcT, ksel, vselT, kwin, vwinT):
    s = ksel.shape[1]
    nc = kc.shape[1]
    nsb = s // SEL_L
    hg = HEADS_PER_GROUP
    ns = N_STREAMS
    qb = ns * Q_BLOCK
    width = hg * Q_BLOCK
    grp = lambda g, c: (g, 0, 0)
    once = pl.Buffered(1)
    return pl.pallas_call(
        _nsa_kernel,
        grid=(GROUPS, s // qb),
        in_specs=[
            pl.BlockSpec((hg, HEAD_DIM, qb), lambda g, c: (g, 0, c)),
            pl.BlockSpec((GATE_ROWS, qb), lambda g, c: (g, c)),
            pl.BlockSpec((1, nc, HEAD_DIM), grp, pipeline_mode=once),
            pl.BlockSpec((1, VAUG_ROWS, nc), lambda g, c: (GROUPS + g, 0, 0), pipeline_mode=once),
            pl.BlockSpec((1, s, 128), grp, pipeline_mode=once),
            pl.BlockSpec((1, VAUG_ROWS, s), grp, pipeline_mode=once),
            pl.BlockSpec((1, s, HEAD_DIM), grp, pipeline_mode=once),
            pl.BlockSpec((1, VAUG_ROWS, s), grp, pipeline_mode=once),
        ],
        out_specs=pl.BlockSpec((qb, hg * HEAD_DIM), lambda g, c: (c, g)),
        out_shape=jax.ShapeDtypeStruct((s, NSA_W), BF16),
        scratch_shapes=[
            pltpu.VMEM((ns, 128, width), BF16),
            pltpu.VMEM((ns, nc + 16, Q_BLOCK), F32),
            pltpu.VMEM((ns, nsb, Q_BLOCK), BF16),
            pltpu.VMEM((ns, VAUG_ROWS, width), F32),
            pltpu.VMEM((ns, 1, width), F32),
            pltpu.VMEM((ns, KEY_TILE, width), F32),
            pltpu.VMEM((ns, HEAD_DIM, width), F32),
            pltpu.VMEM((ns, WIN_KEYS, width), F32),
            pltpu.VMEM((ns, HEAD_DIM, width), F32),
            pltpu.VMEM((ns, nsb, Q_BLOCK), F32),
        ],
        compiler_params=pltpu.CompilerParams(
            dimension_semantics=("arbitrary", "arbitrary"), vmem_limit_bytes=VMEM_LIMIT),
        name="nsa_attention",
    )(qT, gT, kc, vcT, ksel, vselT, kwin, vwinT)


def _block_diag_ones():
    idx = np.arange(MEM_W) // HEAD_DIM
    return jnp.asarray((idx[:, None] == idx[None, :]).astype(np.float32), BF16)


def _head_masks():
    idx = np.arange(MEM_W) // HEAD_DIM
    return jnp.asarray((idx[None, :] == np.arange(MEM_HEADS)[:, None]).astype(np.float32))


def _onehot_pattern():
    rows = np.arange(ONEHOT_SPAN)[:, None] // SEL_L
    lanes = np.arange(128)[None, :] - HEAD_DIM
    return jnp.asarray((rows == lanes).astype(np.float32))


def _gate_layout():
    src = np.full((GATE_PAD,), -1, np.int64)
    for g in range(GROUPS):
        for b in range(3):
            for hh in range(HEADS_PER_GROUP):
                src[g * GATE_ROWS + b * 8 + hh] = (g * HEADS_PER_GROUP + hh) * 3 + b
    return src


def kernel(x, mem, norm_mix_g, norm_mlp_g, mem_norm_g, w_mem_kv, mem_q_norm_g, mem_k_norm_g, w_out, w_mlp_in, w_mlp_out, a_w_in, a_b_glu, a_dw, a_dw_b, a_ln_g, a_ln_b, b_w_in, b_gate_b, b_q_norm_g, kv_norm_g, w_kv, k_norm_g, cmp_pe_k, cmp_pe_v, cmp_w1_k, cmp_w2_k, cmp_w1_v, cmp_w2_v):
    batch, s, _ = x.shape
    assert batch == 1 and s % ONEHOT_SPAN == 0 and s >= WIN_KEYS and (KEY_TILE // Q_BLOCK) % N_STREAMS == 0 and N_STREAMS % (KEY_CHUNK // Q_BLOCK) == 0
    assert w_out.shape[0] == 2 and a_w_in.shape[0] == 1 and b_w_in.shape[0] == 1
    nc = s // CMP_STRIDE
    row = lambda v: v.reshape(1, -1)
    bd = _block_diag_ones()
    hm = _head_masks()

    mkT, mv = _memkv(mem[0], row(mem_norm_g), w_mem_kv.astype(BF16),
                     jnp.tile(mem_k_norm_g, (1, MEM_HEADS))[:, None, :], bd)

    conv0, memo0 = _layer0(
        x[0], row(norm_mix_g[0]), a_w_in[0].astype(BF16), row(a_b_glu[0]), a_dw[0], row(a_dw_b[0]),
        row(a_ln_g[0]), row(a_ln_b[0]), row(jnp.tile(mem_q_norm_g[0], MEM_HEADS)), mkT, mv, bd, hm)
    w_out_bf = w_out.astype(BF16)
    w_in_bf = w_mlp_in.astype(BF16)
    w_o_bf = w_mlp_out.astype(BF16)
    x1 = _post(x[0], conv0, memo0, w_out_bf, row(norm_mlp_g[0]), w_in_bf, w_o_bf, 0)

    src = _gate_layout()
    used = src >= 0
    w_gate = jnp.where(used[None, :], b_w_in[0][:, NSA_W + MEM_W + np.maximum(src, 0)], 0.0)
    b_gate = jnp.where(used, b_gate_b[0][np.maximum(src, 0)], 0.0)
    w_in1 = jnp.concatenate([b_w_in[0][:, :NSA_W + MEM_W], w_gate], axis=1).astype(BF16)
    q_gain = jnp.tile(b_q_norm_g[0], NSA_HEADS) * (QK_SCALE * LOG2E)
    kng = jnp.stack([jnp.tile(k_norm_g[1], GROUPS), jnp.tile(k_norm_g[2], GROUPS)])
    qT, gT, ksel, kwin, vselT, vwinT, kvc, memo1 = _layer1_proj(
        x1, row(kv_norm_g), row(norm_mix_g[1]), w_kv.astype(BF16), w_in1, row(b_gate), row(q_gain),
        row(jnp.tile(mem_q_norm_g[1], MEM_HEADS)), kng, mkT, mv, bd, hm, _onehot_pattern())

    t = kvc.reshape(2 * GROUPS, nc, CMP_STRIDE * HEAD_DIM)
    w1 = jnp.stack([cmp_w1_k, cmp_w1_v]).astype(BF16)
    w2 = jnp.stack([cmp_w2_k, cmp_w2_v]).astype(BF16)
    pe = jnp.stack([cmp_pe_k.reshape(1, -1), cmp_pe_v.reshape(1, -1)])
    cmp_rows, cmp_cols = _compress(t, w1, w2, w2.transpose(0, 2, 1), pe, row(k_norm_g[0]),
                                   k_norm_g[0].reshape(-1, 1))

    nsa = _nsa(qT, gT, cmp_rows, cmp_cols, ksel, vselT, kwin, vwinT)
    x2 = _post(x1, nsa, memo1, w_out_bf, row(norm_mlp_g[1]), w_in_bf, w_o_bf, 1)
    return x2[None]
```

```python
import functools

import numpy as np
import jax
import jax.numpy as jnp
from jax import lax
from jax.experimental import pallas as pl
from jax.experimental.pallas import tpu as pltpu

F32 = jnp.float32
BF16 = jnp.bfloat16

D_MODEL = 1024
HEAD_DIM = 64
MEM_LEN = 256
MEM_HEADS = 4
MEM_W = MEM_HEADS * HEAD_DIM
CONV_W = D_MODEL - MEM_W
CONV_K = 31
NSA_HEADS = CONV_W // HEAD_DIM
NSA_W = NSA_HEADS * HEAD_DIM
GROUPS = 2
HEADS_PER_GROUP = NSA_HEADS // GROUPS
CMP_L = 32
CMP_STRIDE = 16
CMP_HID = 256
SEL_L = 64
N_SEL = 16
WIN = 512
Q_BLOCK = 128
D_FF = 4 * D_MODEL
GATE_PAD = 128
GATE_ROWS = GATE_PAD // GROUPS

ROW_TILE = 512
FF_CHUNK = 1024
KEY_TILE = 1024
ONEHOT_SPAN = 1024
CMP_PATHS = 8
SEL_PATHS = 8
N_STREAMS = 4
KEY_CHUNK = 256
SEL_PER_SPAN = ONEHOT_SPAN // SEL_L
WIN_KEYS = WIN + Q_BLOCK
CONV_HALO = 32
CONV_ROWS = 64
VAUG_ROWS = 80
VMEM_LIMIT = 56 * 1024 * 1024
MXU_WIDTH = 256

EPS = 1e-6
LN_EPS = 1e-5
LOG2E = 1.4426950408889634
QK_SCALE = HEAD_DIM ** -0.5
MASK_BIAS = -2.0 ** 126
M_INIT = -2.0 ** 100


def _dot(a, b):
    return jnp.dot(a, b, preferred_element_type=F32)


def _dot_3tiles(a, b):
    two = 2 * MXU_WIDTH
    half = a.shape[0] // 2
    return jnp.concatenate([
        _dot(a, b[:, 0:two]),
        jnp.concatenate([_dot(a[0:half], b[:, two:]), _dot(a[half:], b[:, two:])], axis=0),
    ], axis=1)


def _sigmoid(x):
    return 0.5 * jnp.tanh(0.5 * x) + 0.5


def _rms_scale(x):
    return lax.rsqrt(jnp.mean(x * x, axis=-1, keepdims=True) + EPS)


def _head_meansq(x, bd):
    x2 = x * x
    hi = x2.astype(BF16)
    lo = (x2 - hi.astype(F32)).astype(BF16)
    return (_dot(hi, bd) + _dot(lo, bd)) * (1.0 / HEAD_DIM)


def _mem_attention(qn, mkT, mv, hm_ref):
    out = jnp.zeros(qn.shape, F32)
    for h in range(MEM_HEADS):
        hm = hm_ref[h:h + 1, :]
        s = _dot((qn * hm).astype(BF16), mkT)
        e = jnp.exp2(s - jnp.max(s, axis=-1, keepdims=True))
        l = jnp.sum(e, axis=-1, keepdims=True)
        out = out + _dot(e.astype(BF16), mv) * (hm / l)
    return out


def _memkv_kernel(mem_ref, g_ref, w_ref, kg_ref, bd_ref, mkT_ref, mv_ref):
    m = mem_ref[...]
    mn = (m * _rms_scale(m) * g_ref[...]).astype(BF16)
    kv = _dot(mn, w_ref[0])
    k = kv[:, :MEM_W]
    kn = k * lax.rsqrt(_head_meansq(k, bd_ref[...]) + EPS) * kg_ref[0]
    mkT_ref[0] = (kn * (QK_SCALE * LOG2E)).T.astype(BF16)
    mv_ref[0] = kv[:, MEM_W:].astype(BF16)


def _memkv(mem, g, w, kg, bd):
    depth = w.shape[0]
    return pl.pallas_call(
        _memkv_kernel,
        grid=(depth,),
        in_specs=[
            pl.BlockSpec((MEM_LEN, D_MODEL), lambda l: (0, 0)),
            pl.BlockSpec((1, D_MODEL), lambda l: (0, 0)),
            pl.BlockSpec((1, D_MODEL, 2 * MEM_W), lambda l: (l, 0, 0)),
            pl.BlockSpec((1, 1, MEM_W), lambda l: (l, 0, 0)),
            pl.BlockSpec((MEM_W, MEM_W), lambda l: (0, 0)),
        ],
        out_specs=[
            pl.BlockSpec((1, MEM_W, MEM_LEN), lambda l: (l, 0, 0)),
            pl.BlockSpec((1, MEM_LEN, MEM_W), lambda l: (l, 0, 0)),
        ],
        out_shape=[
            jax.ShapeDtypeStruct((depth, MEM_W, MEM_LEN), BF16),
            jax.ShapeDtypeStruct((depth, MEM_LEN, MEM_W), BF16),
        ],
        name="mem_kv",
    )(mem, g, w, kg, bd)


def _layer0_kernel(x_ref, g_ref, win_ref, bglu_ref, dw_ref, dwb_ref, lng_ref, lnb_ref, qg_ref,
                   mkT_ref, mv_ref, bd_ref, hm_ref, conv_ref, memo_ref, buf_ref, cv_ref, sh_ref):
    i = pl.program_id(0)
    tm = x_ref.shape[0]
    x = x_ref[...]
    h = (x * _rms_scale(x) * g_ref[...]).astype(BF16)
    u = _dot(h, win_ref[...])
    a = u[:, :CONV_W] + bglu_ref[:, :CONV_W]
    gate = u[:, CONV_W:2 * CONV_W] + bglu_ref[:, CONV_W:]
    v = a * _sigmoid(gate)

    @pl.when(i == 0)
    def _():
        buf_ref[0:CONV_HALO, :] = jnp.zeros((CONV_HALO, CONV_W), F32)

    buf_ref[CONV_HALO:CONV_HALO + tm, :] = v

    base = CONV_HALO - (CONV_K - 1)
    rows = tm + CONV_HALO
    for cb in range(CONV_W // 128):
        cs = slice(cb * 128, (cb + 1) * 128)
        xb = buf_ref[:, cs]
        sh_ref[0] = xb
        for b in range(1, 8):
            sh_ref[b] = pltpu.roll(xb, rows - b, axis=0)

        def conv_rows(r, carry, cs=cs):
            r0 = pl.multiple_of(r * CONV_ROWS, CONV_ROWS)
            acc = jnp.zeros((CONV_ROWS, 128), F32) + dwb_ref[:, cs]
            for k in range(CONV_K):
                a, b = divmod(base + k, 8)
                acc = acc + dw_ref[k:k + 1, cs] * sh_ref[b, pl.ds(r0 + 8 * a, CONV_ROWS), :]
            cv_ref[pl.ds(r0, CONV_ROWS), cs] = acc
            return carry

        lax.fori_loop(0, tm // CONV_ROWS, conv_rows, 0)
    buf_ref[0:CONV_HALO, :] = buf_ref[tm:tm + CONV_HALO, :]

    cv = cv_ref[...]
    mu = jnp.mean(cv, axis=-1, keepdims=True)
    d = cv - mu
    var = jnp.mean(d * d, axis=-1, keepdims=True)
    y = d * lax.rsqrt(var + LN_EPS) * lng_ref[...] + lnb_ref[...]
    conv_ref[...] = (y * _sigmoid(y)).astype(BF16)

    qm = u[:, 2 * CONV_W:]
    qn = qm * lax.rsqrt(_head_meansq(qm, bd_ref[...]) + EPS) * qg_ref[...]
    memo_ref[...] = _mem_attention(qn, mkT_ref[0], mv_ref[0], hm_ref).astype(BF16)


def _layer0(x, g, w_in, bglu, dw, dwb, lng, lnb, qg, mkT, mv, bd, hm):
    s = x.shape[0]
    tm = ROW_TILE
    const = lambda i: (0, 0)
    return pl.pallas_call(
        _layer0_kernel,
        grid=(s // tm,),
        in_specs=[
            pl.BlockSpec((tm, D_MODEL), lambda i: (i, 0)),
            pl.BlockSpec((1, D_MODEL), const),
            pl.BlockSpec(w_in.shape, const),
            pl.BlockSpec((1, 2 * CONV_W), const),
            pl.BlockSpec((CONV_K, CONV_W), const),
            pl.BlockSpec((1, CONV_W), const),
            pl.BlockSpec((1, CONV_W), const),
            pl.BlockSpec((1, CONV_W), const),
            pl.BlockSpec((1, MEM_W), const),
            pl.BlockSpec((1, MEM_W, MEM_LEN), lambda i: (0, 0, 0)),
            pl.BlockSpec((1, MEM_LEN, MEM_W), lambda i: (0, 0, 0)),
            pl.BlockSpec((MEM_W, MEM_W), const),
            pl.BlockSpec((MEM_HEADS, MEM_W), const),
        ],
        out_specs=[
            pl.BlockSpec((tm, CONV_W), lambda i: (i, 0)),
            pl.BlockSpec((tm, MEM_W), lambda i: (i, 0)),
        ],
        out_shape=[
            jax.ShapeDtypeStruct((s, CONV_W), BF16),
            jax.ShapeDtypeStruct((s, MEM_W), BF16),
        ],
        scratch_shapes=[
            pltpu.VMEM((tm + CONV_HALO, CONV_W), F32),
            pltpu.VMEM((tm, CONV_W), F32),
            pltpu.VMEM((8, tm + CONV_HALO, 128), F32),
        ],
        compiler_params=pltpu.CompilerParams(
            dimension_semantics=("arbitrary",), vmem_limit_bytes=VMEM_LIMIT),
        name="layer0_mixer",
    )(x, g, w_in, bglu, dw, dwb, lng, lnb, qg, mkT, mv, bd, hm)


def _post_kernel(x_ref, ma_ref, mb_ref, wa_ref, wb_ref, g_ref, win_ref, wout_ref, o_ref):
    o_ref[...] = x_ref[...] + _dot(ma_ref[...], wa_ref[0]) + _dot(mb_ref[...], wb_ref[0])
    x1 = o_ref[...]
    h = (x1 * _rms_scale(x1) * g_ref[...]).astype(BF16)
    for c in range(D_FF // FF_CHUNK):
        cs = slice(c * FF_CHUNK, (c + 1) * FF_CHUNK)
        t = jnp.maximum(_dot(h, win_ref[0, :, cs]), 0.0)
        o_ref[...] += _dot((t * t).astype(BF16), wout_ref[0, cs, :])


def _post(x, ma, mb, w_o, g, w_in, w_out, layer):
    s = x.shape[0]
    tm = ROW_TILE
    const = lambda i: (0, 0)
    this = lambda i: (layer, 0, 0)
    single = pl.Buffered(1)
    return pl.pallas_call(
        _post_kernel,
        grid=(s // tm,),
        in_specs=[
            pl.BlockSpec((tm, D_MODEL), lambda i: (i, 0)),
            pl.BlockSpec((tm, CONV_W), lambda i: (i, 0)),
            pl.BlockSpec((tm, MEM_W), lambda i: (i, 0)),
            pl.BlockSpec((1, CONV_W, D_MODEL), this, pipeline_mode=single),
            pl.BlockSpec((1, MEM_W, D_MODEL), lambda i: (layer, CONV_W // MEM_W, 0), pipeline_mode=single),
            pl.BlockSpec((1, D_MODEL), const),
            pl.BlockSpec((1, D_MODEL, D_FF), this, pipeline_mode=single),
            pl.BlockSpec((1, D_FF, D_MODEL), this, pipeline_mode=single),
        ],
        out_specs=pl.BlockSpec((tm, D_MODEL), lambda i: (i, 0)),
        out_shape=jax.ShapeDtypeStruct((s, D_MODEL), F32),
        compiler_params=pltpu.CompilerParams(
            dimension_semantics=("arbitrary",), vmem_limit_bytes=VMEM_LIMIT),
        name="outproj_mlp",
    )(x, ma, mb, w_o, w_o, g, w_in, w_out)


def _layer1_proj_kernel(x_ref, gkv_ref, gmix_ref, wkv_ref, win_ref, gb_ref, qg_ref, mqg_ref, kng_ref,
                        mkT_ref, mv_ref, bd_ref, hm_ref, pat_ref,
                        qT_ref, gT_ref, ksel_ref, kwin_ref, vselT_ref, vwinT_ref, kvc_ref, memo_ref):
    tm = x_ref.shape[0]
    x = x_ref[...]
    xn = x * _rms_scale(x)
    bd = bd_ref[...]

    kv = _dot((xn * gkv_ref[...]).astype(BF16), wkv_ref[...])
    for i in range(2 * GROUPS):
        kvc_ref[i] = kv[:, i * HEAD_DIM:(i + 1) * HEAD_DIM].astype(BF16)
    k2 = kv[:, 256:384]
    v2 = kv[:, 384:512]
    kw = kv[:, 512:640]
    vw = kv[:, 640:768]
    bd2 = bd[0:128, 0:128]
    k2n = k2 * lax.rsqrt(_head_meansq(k2, bd2) + EPS) * kng_ref[0:1, :]
    kwn = kw * lax.rsqrt(_head_meansq(kw, bd2) + EPS) * kng_ref[1:2, :]

    lane = lax.broadcasted_iota(jnp.int32, (tm, 128), 1)
    pat = pat_ref[...]
    ksel_ref[0] = jnp.where(lane < HEAD_DIM, k2n, pat).astype(BF16)
    ksel_ref[1] = jnp.where(lane < HEAD_DIM, pltpu.roll(k2n, HEAD_DIM, axis=1), pat).astype(BF16)
    kwin_ref[0] = kwn[:, :HEAD_DIM].astype(BF16)
    kwin_ref[1] = kwn[:, HEAD_DIM:].astype(BF16)

    ones_rows = (lax.broadcasted_iota(jnp.int32, (VAUG_ROWS - HEAD_DIM, tm), 0) == 0).astype(BF16)
    v2T = v2.T.astype(BF16)
    vwT = vw.T.astype(BF16)
    for g in range(GROUPS):
        vselT_ref[g, 0:HEAD_DIM, :] = v2T[g * HEAD_DIM:(g + 1) * HEAD_DIM, :]
        vselT_ref[g, HEAD_DIM:VAUG_ROWS, :] = ones_rows
        vwinT_ref[g, 0:HEAD_DIM, :] = vwT[g * HEAD_DIM:(g + 1) * HEAD_DIM, :]
        vwinT_ref[g, HEAD_DIM:VAUG_ROWS, :] = ones_rows

    u = _dot((xn * gmix_ref[...]).astype(BF16), win_ref[...])
    for cb in range(NSA_W // 256):
        cs = slice(cb * 256, (cb + 1) * 256)
        qc = u[:, cs]
        qn = qc * lax.rsqrt(_head_meansq(qc, bd) + EPS) * qg_ref[:, cs]
        qT_ref[4 * cb:4 * cb + 4] = qn.T.astype(BF16).reshape(4, HEAD_DIM, tm)

    gates = _sigmoid(u[:, NSA_W + MEM_W:] + gb_ref[...])
    gT_ref[...] = gates.T

    qm = u[:, NSA_W:NSA_W + MEM_W]
    qmn = qm * lax.rsqrt(_head_meansq(qm, bd) + EPS) * mqg_ref[...]
    memo_ref[...] = _mem_attention(qmn, mkT_ref[0], mv_ref[0], hm_ref).astype(BF16)


def _layer1_proj(x, gkv, gmix, wkv, w_in, gb, qg, mqg, kng, mkT, mv, bd, hm, pat):
    s = x.shape[0]
    tm = ROW_TILE
    const = lambda i: (0, 0)
    pat_blocks = pat.shape[0] // tm
    return pl.pallas_call(
        _layer1_proj_kernel,
        grid=(s // tm,),
        in_specs=[
            pl.BlockSpec((tm, D_MODEL), lambda i: (i, 0)),
            pl.BlockSpec((1, D_MODEL), const),
            pl.BlockSpec((1, D_MODEL), const),
            pl.BlockSpec(wkv.shape, const),
            pl.BlockSpec(w_in.shape, const),
            pl.BlockSpec((1, GATE_PAD), const),
            pl.BlockSpec((1, NSA_W), const),
            pl.BlockSpec((1, MEM_W), const),
            pl.BlockSpec((2, 128), const),
            pl.BlockSpec((1, MEM_W, MEM_LEN), lambda i: (1, 0, 0)),
            pl.BlockSpec((1, MEM_LEN, MEM_W), lambda i: (1, 0, 0)),
            pl.BlockSpec((MEM_W, MEM_W), const),
            pl.BlockSpec((MEM_HEADS, MEM_W), const),
            pl.BlockSpec((tm, 128), lambda i: (i % pat_blocks, 0)),
        ],
        out_specs=[
            pl.BlockSpec((NSA_HEADS, HEAD_DIM, tm), lambda i: (0, 0, i)),
            pl.BlockSpec((GATE_PAD, tm), lambda i: (0, i)),
            pl.BlockSpec((GROUPS, tm, 128), lambda i: (0, i, 0)),
            pl.BlockSpec((GROUPS, tm, HEAD_DIM), lambda i: (0, i, 0)),
            pl.BlockSpec((GROUPS, VAUG_ROWS, tm), lambda i: (0, 0, i)),
            pl.BlockSpec((GROUPS, VAUG_ROWS, tm), lambda i: (0, 0, i)),
            pl.BlockSpec((2 * GROUPS, tm, HEAD_DIM), lambda i: (0, i, 0)),
            pl.BlockSpec((tm, MEM_W), lambda i: (i, 0)),
        ],
        out_shape=[
            jax.ShapeDtypeStruct((NSA_HEADS, HEAD_DIM, s), BF16),
            jax.ShapeDtypeStruct((GATE_PAD, s), F32),
            jax.ShapeDtypeStruct((GROUPS, s, 128), BF16),
            jax.ShapeDtypeStruct((GROUPS, s, HEAD_DIM), BF16),
            jax.ShapeDtypeStruct((GROUPS, VAUG_ROWS, s), BF16),
            jax.ShapeDtypeStruct((GROUPS, VAUG_ROWS, s), BF16),
            jax.ShapeDtypeStruct((2 * GROUPS, s, HEAD_DIM), BF16),
            jax.ShapeDtypeStruct((s, MEM_W), BF16),
        ],
        compiler_params=pltpu.CompilerParams(
            dimension_semantics=("arbitrary",), vmem_limit_bytes=VMEM_LIMIT),
        name="layer1_proj",
    )(x, gkv, gmix, wkv, w_in, gb, qg, mqg, kng, mkT, mv, bd, hm, pat)


def _compress_kernel(t_ref, w1_ref, w2_ref, w2T_ref, pe_ref, kg_ref, kgT_ref, o_ref, oT_ref, hb_ref):
    nc = t_ref.shape[1]
    half = CMP_STRIDE * HEAD_DIM
    t = t_ref[0]
    ha = _dot(t, w1_ref[0, 0:half, :])
    hb_ref[0:nc, :] = _dot(t, w1_ref[0, half:2 * half, :])
    hb_ref[nc:nc + 8, :] = jnp.zeros((8, CMP_HID), F32)
    pe = jnp.broadcast_to(pe_ref[0], (8, 2 * half)).astype(BF16)
    pe_term = _dot(pe, w1_ref[0])[0:1, :]
    h = ha + hb_ref[1:nc + 1, :] + pe_term
    hg = 0.5 * h * (1.0 + jnp.tanh(0.7978845608028654 * (h + 0.044715 * (h * h * h))))
    hg = hg.astype(BF16)
    o = _dot(hg, w2_ref[0])
    oT = lax.dot_general(w2T_ref[0], hg, (((1,), (1,)), ((), ())), preferred_element_type=F32)
    is_k = pl.program_id(0) < GROUPS
    on = o * lax.rsqrt(jnp.mean(o * o, axis=1, keepdims=True) + EPS) * kg_ref[...]
    oTn = oT * lax.rsqrt(jnp.mean(oT * oT, axis=0, keepdims=True) + EPS) * kgT_ref[...]
    o_ref[0] = jnp.where(is_k, on, o).astype(BF16)
    oT_ref[0, 0:HEAD_DIM, :] = jnp.where(is_k, oTn, oT).astype(BF16)
    oT_ref[0, HEAD_DIM:VAUG_ROWS, :] = (
        lax.broadcasted_iota(jnp.int32, (VAUG_ROWS - HEAD_DIM, nc), 0) == 0).astype(BF16)


def _compress(t, w1, w2, w2T, pe, kg, kgT):
    n, nc, width = t.shape
    return pl.pallas_call(
        _compress_kernel,
        grid=(n,),
        in_specs=[
            pl.BlockSpec((1, nc, width), lambda i: (i, 0, 0)),
            pl.BlockSpec((1, 2 * width, CMP_HID), lambda i: (i // GROUPS, 0, 0)),
            pl.BlockSpec((1, CMP_HID, HEAD_DIM), lambda i: (i // GROUPS, 0, 0)),
            pl.BlockSpec((1, HEAD_DIM, CMP_HID), lambda i: (i // GROUPS, 0, 0)),
            pl.BlockSpec((1, 1, 2 * width), lambda i: (i // GROUPS, 0, 0)),
            pl.BlockSpec((1, HEAD_DIM), lambda i: (0, 0)),
            pl.BlockSpec((HEAD_DIM, 1), lambda i: (0, 0)),
        ],
        out_specs=[
            pl.BlockSpec((1, nc, HEAD_DIM), lambda i: (i, 0, 0)),
            pl.BlockSpec((1, VAUG_ROWS, nc), lambda i: (i, 0, 0)),
        ],
        out_shape=[
            jax.ShapeDtypeStruct((n, nc, HEAD_DIM), BF16),
            jax.ShapeDtypeStruct((n, VAUG_ROWS, nc), BF16),
        ],
        scratch_shapes=[pltpu.VMEM((nc + 8, CMP_HID), F32)],
        compiler_params=pltpu.CompilerParams(
            dimension_semantics=("arbitrary",), vmem_limit_bytes=VMEM_LIMIT),
        name="compress_kv",
    )(t, w1, w2, w2T, pe, kg, kgT)


def _nsa_kernel(qT_ref, gT_ref, kc_ref, vcT_ref, ksel_ref, vselT_ref, kwin_ref, vwinT_ref, out_ref,
                qaug_st, psum_st, bias_st, acc_st, m_st, s_st, oc_st, sw_st, ow_st, pick_st):
    cc = pl.program_id(1)
    nc = kc_ref.shape[1]
    nsb = bias_st.shape[1]
    hg = HEADS_PER_GROUP
    width = hg * Q_BLOCK
    n_chunks = KEY_TILE // KEY_CHUNK
    last = (cc * N_STREAMS) // (KEY_TILE // Q_BLOCK)

    def load_queries(st):
        qaug_ref = qaug_st.at[st]
        for hh in range(hg):
            qaug_ref[0:HEAD_DIM, hh * Q_BLOCK:(hh + 1) * Q_BLOCK] = qT_ref[hh, :, st * Q_BLOCK:(st + 1) * Q_BLOCK]
        qaug_ref[HEAD_DIM:128, :] = jnp.zeros((128 - HEAD_DIM, width), BF16)

    def query_pos(st):
        return (cc * N_STREAMS + st) * Q_BLOCK + (lax.broadcasted_iota(jnp.int32, (1, width), 1) & (Q_BLOCK - 1))

    for st in range(N_STREAMS):
        load_queries(st)

    def cmp_branch(rows):
        for st in range(N_STREAMS):
            qT = qaug_st[st, 0:HEAD_DIM, :]
            t_q = query_pos(st)
            oc_ref, psum_ref = oc_st.at[st], psum_st.at[st]
            s = _dot_3tiles(kc_ref[0, 0:rows, :], qT)
            seen = max(rows - 2 * cuts[0], 0)
            ci = seen + lax.broadcasted_iota(jnp.int32, (rows - seen, 1), 0)
            tail = jnp.where(ci * CMP_STRIDE + (CMP_L - 1) <= t_q, s[seen:, :], -jnp.inf)
            s = jnp.concatenate([s[0:seen, :], tail], axis=0) if seen else tail
            m = jnp.max(s, axis=0, keepdims=True)
            m = jnp.where(m == -jnp.inf, 0.0, m)
            e = jnp.exp2(s - m)
            oa = _dot(vcT_ref[0, :, 0:rows], e.astype(BF16))
            rl = 1.0 / jnp.maximum(oa[HEAD_DIM:HEAD_DIM + 1, :], 1e-30)
            oc_ref[...] = oa[0:HEAD_DIM, :] * rl
            p = e * rl
            psum = p[:, 0:Q_BLOCK]
            for hh in range(1, hg):
                psum = psum + p[:, hh * Q_BLOCK:(hh + 1) * Q_BLOCK]
            psum_ref[8:8 + rows, :] = psum

    @pl.when(cc == 0)
    def _():
        psum_st[...] = jnp.zeros(psum_st.shape, F32)

    cuts = [nc * k // CMP_PATHS for k in range(1, CMP_PATHS + 1)] if nc % (128 * CMP_PATHS) == 0 else [nc]
    need = (cc + 1) * N_STREAMS * (Q_BLOCK // CMP_STRIDE)
    for k, rows in enumerate(cuts):
        lo = cuts[k - 1] if k else 0
        pl.when((need > lo) & (need <= rows))(functools.partial(cmp_branch, rows))

    def block_validity(st, rows):
        blk = lax.broadcasted_iota(jnp.int32, (rows, Q_BLOCK), 0)
        tb = ((cc * N_STREAMS + st) * Q_BLOCK + lax.broadcasted_iota(jnp.int32, (1, Q_BLOCK), 1)) >> 6
        return blk, tb, blk <= tb

    def select_blocks(rows):
        for st in range(N_STREAMS):
            psum_ref = psum_st.at[st]
            imp = (psum_ref[pl.ds(7, rows, stride=4), :] + psum_ref[pl.ds(11, rows, stride=4), :]
                   + 2.0 * (psum_ref[pl.ds(8, rows, stride=4), :] + psum_ref[pl.ds(9, rows, stride=4), :]
                            + psum_ref[pl.ds(10, rows, stride=4), :]))
            blk, tb, valid = block_validity(st, rows)
            forced = (blk == 0) | (blk == tb) | (blk == tb - 1)
            pick_st[st, 0:rows, :] = jnp.where(forced | ~valid, -jnp.inf, imp)
        blk_f = lax.broadcasted_iota(jnp.int32, (rows, Q_BLOCK), 0).astype(F32)

        def pick(_, carry):
            for st in range(N_STREAMS):
                sc = pick_st[st, 0:rows, :]
                mx = jnp.max(sc, axis=0, keepdims=True)
                first = jnp.min(jnp.where(sc == mx, blk_f, float(rows)), axis=0, keepdims=True)
                pick_st[st, 0:rows, :] = jnp.where(blk_f == first, -jnp.inf, sc)
            return carry

        lax.fori_loop(0, N_SEL - 3, pick, 0)
        for st in range(N_STREAMS):
            sel = (pick_st[st, 0:rows, :] == -jnp.inf) & block_validity(st, rows)[2]
            bias_st[st, 0:rows, :] = jnp.where(sel, 0.0, MASK_BIAS).astype(BF16)

    @pl.when(cc == 0)
    def _():
        bias_st[...] = jnp.full(bias_st.shape, MASK_BIAS, BF16)

    b_cuts = [nsb * k // SEL_PATHS for k in range(1, SEL_PATHS + 1)] if nsb % (16 * SEL_PATHS) == 0 else [nsb]
    need_b = 2 * (cc + 1) * N_STREAMS
    for k, rows in enumerate(b_cuts):
        lo = b_cuts[k - 1] if k else 0
        pl.when((need_b > lo) & (need_b <= rows))(functools.partial(select_blocks, rows))

    def stream(st):
        c = cc * N_STREAMS + st
        qs = slice(st * Q_BLOCK, (st + 1) * Q_BLOCK)
        qaug_ref, psum_ref, bias_ref, acc_ref = qaug_st.at[st], psum_st.at[st], bias_st.at[st], acc_st.at[st]
        m_ref, s_ref, oc_ref, sw_ref, ow_ref = m_st.at[st], s_st.at[st], oc_st.at[st], sw_st.at[st], ow_st.at[st]
        qT = qaug_ref[0:HEAD_DIM, :]
        t_q = query_pos(st)

        row0 = pl.multiple_of(jnp.maximum(c * Q_BLOCK - WIN, 0), Q_BLOCK)
        sw = _dot_3tiles(kwin_ref[0, pl.ds(row0, WIN_KEYS), :], qT)
        kp = row0 + lax.broadcasted_iota(jnp.int32, (WIN_KEYS, 1), 0)
        sw = jnp.where((kp <= t_q) & (kp > t_q - WIN), sw, -jnp.inf)
        sw_ref[...] = sw
        mw = jnp.max(jnp.max(sw.reshape(WIN_KEYS // 8, 8, width), axis=0), axis=0, keepdims=True)
        mw = jnp.where(mw == -jnp.inf, 0.0, mw)

        acc_ref[...] = jnp.zeros(acc_ref.shape, F32)
        m_ref[...] = jnp.full(m_ref.shape, M_INIT, F32)

        def set_bias(j):
            span = j // (ONEHOT_SPAN // KEY_TILE)
            b16 = bias_ref[pl.ds(pl.multiple_of(span * SEL_PER_SPAN, SEL_PER_SPAN), SEL_PER_SPAN), :]
            for hh in range(hg):
                qaug_ref[HEAD_DIM:HEAD_DIM + SEL_PER_SPAN, hh * Q_BLOCK:(hh + 1) * Q_BLOCK] = b16

        def scores(j, r):
            k0 = pl.multiple_of(j * KEY_TILE + r * KEY_CHUNK, KEY_CHUNK)
            return _dot(ksel_ref[0, pl.ds(k0, KEY_CHUNK), :], qaug_ref[...])

        def chunk_max(cmax, sc):
            return jnp.maximum(cmax, jnp.max(sc.reshape(KEY_CHUNK // 8, 8, width), axis=0))

        def new_max(cmax):
            m_old = m_ref[...]
            m_new = jnp.maximum(m_old, jnp.max(cmax, axis=0, keepdims=True))
            m_ref[...] = m_new
            return m_new, jnp.exp2(m_old - m_new)

        def values(j, r):
            k0 = pl.multiple_of(j * KEY_TILE + r * KEY_CHUNK, KEY_CHUNK)
            return vselT_ref[0, :, pl.ds(k0, KEY_CHUNK)]

        cmax0 = jnp.full((8, width), MASK_BIAS, F32)
        set_bias(0)
        cmax = cmax0
        ow = jnp.zeros((VAUG_ROWS, width), F32)
        w_cuts = [WIN_KEYS * r // n_chunks // 128 * 128 for r in range(n_chunks)] + [WIN_KEYS]
        for r in range(n_chunks):
            sc = _dot_3tiles(ksel_ref[0, r * KEY_CHUNK:(r + 1) * KEY_CHUNK, :], qaug_ref[...])
            ws = slice(w_cuts[r], w_cuts[r + 1])
            pw = jnp.exp2(sw_ref[ws, :] - mw).astype(BF16)
            s_ref[r * KEY_CHUNK:(r + 1) * KEY_CHUNK, :] = sc
            cmax = chunk_max(cmax, sc)
            ow = ow + _dot(vwinT_ref[0, :, pl.ds(pl.multiple_of(row0 + w_cuts[r], 128), w_cuts[r + 1] - w_cuts[r])], pw)
        ow_ref[...] = ow[0:HEAD_DIM, :] / jnp.maximum(ow[HEAD_DIM:HEAD_DIM + 1, :], 1e-30)

        def pipe_step(j, cmax):
            m_new, alpha = new_max(cmax)
            set_bias(j + 1)
            acc = acc_ref[...] * alpha
            cnext = cmax0
            for r in range(n_chunks):
                rs = slice(r * KEY_CHUNK, (r + 1) * KEY_CHUNK)
                sc = scores(j + 1, r)
                pj = jnp.exp2(s_ref[rs, :] - m_new).astype(BF16)
                s_ref[rs, :] = sc
                cnext = chunk_max(cnext, sc)
                acc = acc + _dot(values(j, r), pj)
            acc_ref[...] = acc
            return cnext

        def mask_diagonal():
            diag = pl.ds(pl.multiple_of((c % (KEY_TILE // Q_BLOCK)) * Q_BLOCK, Q_BLOCK), Q_BLOCK)
            ku = lax.broadcasted_iota(jnp.int32, (Q_BLOCK, 1), 0)
            s_ref[diag, :] = jnp.where(ku <= t_q - c * Q_BLOCK, s_ref[diag, :], MASK_BIAS)

        def finish_tile(rows):
            s_last = s_ref[0:rows, :]
            m_new, alpha = new_max(jnp.max(s_last.reshape(rows // 8, 8, width), axis=0))
            p_last = jnp.exp2(s_last - m_new).astype(BF16)
            acc_ref[...] = acc_ref[...] * alpha + _dot(
                vselT_ref[0, :, pl.ds(pl.multiple_of(last * KEY_TILE, KEY_TILE), rows)], p_last)

        def emit():
            o_cmp = oc_ref[...]
            o_win = ow_ref[...]
            o_slc = acc_ref[0:HEAD_DIM, :] / jnp.maximum(acc_ref[HEAD_DIM:HEAD_DIM + 1, :], 1e-30)

            heads = []
            for hh in range(hg):
                hs = slice(hh * Q_BLOCK, (hh + 1) * Q_BLOCK)
                heads.append(o_cmp[:, hs] * gT_ref[hh:hh + 1, qs] + o_slc[:, hs] * gT_ref[8 + hh:9 + hh, qs]
                             + o_win[:, hs] * gT_ref[16 + hh:17 + hh, qs])
            for pr in range(hg // 2):
                pair = jnp.concatenate([heads[2 * pr], heads[2 * pr + 1]], axis=0)
                out_ref[qs, pr * 128:(pr + 1) * 128] = pair.T.astype(BF16)

        return cmax, pipe_step, mask_diagonal, finish_tile, emit

    streams = [stream(st) for st in range(N_STREAMS)]

    def sweep_steps(j, cmaxes):
        return tuple(s[1](j, cm) for s, cm in zip(streams, cmaxes))

    lax.fori_loop(0, last, sweep_steps, tuple(s[0] for s in streams))
    for s in streams:
        s[2]()
    per = KEY_CHUNK // Q_BLOCK
    for first in range(0, N_STREAMS, per):
        diag_chunk = ((cc * N_STREAMS + first) % (KEY_TILE // Q_BLOCK)) // per

        def finish_group(rows, first=first):
            for s in streams[first:first + per]:
                s[3](rows)

        for k in range(n_chunks):
            pl.when(diag_chunk == k)(functools.partial(finish_group, (k + 1) * KEY_CHUNK))
    for s in streams:
        s[4]()


def _nsa(qT, gT, kc, vcT, ksel, vselT, kwin, vwinT):
    s = ksel.shape[1]
    nc = kc.shape[1]
    nsb = s // SEL_L
    hg = HEADS_PER_GROUP
    ns = N_STREAMS
    qb = ns * Q_BLOCK
    width = hg * Q_BLOCK
    grp = lambda g, c: (g, 0, 0)
    once = pl.Buffered(1)
    return pl.pallas_call(
        _nsa_kernel,
        grid=(GROUPS, s // qb),
        in_specs=[
            pl.BlockSpec((hg, HEAD_DIM, qb), lambda g, c: (g, 0, c)),
            pl.BlockSpec((GATE_ROWS, qb), lambda g, c: (g, c)),
            pl.BlockSpec((1, nc, HEAD_DIM), grp, pipeline_mode=once),
            pl.BlockSpec((1, VAUG_ROWS, nc), lambda g, c: (GROUPS + g, 0, 0), pipeline_mode=once),
            pl.BlockSpec((1, s, 128), grp, pipeline_mode=once),
            pl.BlockSpec((1, VAUG_ROWS, s), grp, pipeline_mode=once),
            pl.BlockSpec((1, s, HEAD_DIM), grp, pipeline_mode=once),
            pl.BlockSpec((1, VAUG_ROWS, s), grp, pipeline_mode=once),
        ],
        out_specs=pl.BlockSpec((qb, hg * HEAD_DIM), lambda g, c: (c, g)),
        out_shape=jax.ShapeDtypeStruct((s, NSA_W), BF16),
        scratch_shapes=[
            pltpu.VMEM((ns, 128, width), BF16),
            pltpu.VMEM((ns, nc + 16, Q_BLOCK), F32),
            pltpu.VMEM((ns, nsb, Q_BLOCK), BF16),
            pltpu.VMEM((ns, VAUG_ROWS, width), F32),
            pltpu.VMEM((ns, 1, width), F32),
            pltpu.VMEM((ns, KEY_TILE, width), F32),
            pltpu.VMEM((ns, HEAD_DIM, width), F32),
            pltpu.VMEM((ns, WIN_KEYS, width), F32),
            pltpu.VMEM((ns, HEAD_DIM, width), F32),
            pltpu.VMEM((ns, nsb, Q_BLOCK), F32),
        ],
        compiler_params=pltpu.CompilerParams(
            dimension_semantics=("arbitrary", "arbitrary"), vmem_limit_bytes=VMEM_LIMIT),
        name="nsa_attention",
    )(qT, gT, kc, vcT, ksel, vselT, kwin, vwinT)


def _block_diag_ones():
    idx = np.arange(MEM_W) // HEAD_DIM
    return jnp.asarray((idx[:, None] == idx[None, :]).astype(np.float32), BF16)


def _head_masks():
    idx = np.arange(MEM_W) // HEAD_DIM
    return jnp.asarray((idx[None, :] == np.arange(MEM_HEADS)[:, None]).astype(np.float32))


def _onehot_pattern():
    rows = np.arange(ONEHOT_SPAN)[:, None] // SEL_L
    lanes = np.arange(128)[None, :] - HEAD_DIM
    return jnp.asarray((rows == lanes).astype(np.float32))


def _gate_layout():
    src = np.full((GATE_PAD,), -1, np.int64)
    for g in range(GROUPS):
        for b in range(3):
            for hh in range(HEADS_PER_GROUP):
                src[g * GATE_ROWS + b * 8 + hh] = (g * HEADS_PER_GROUP + hh) * 3 + b
    return src


def kernel(x, mem, norm_mix_g, norm_mlp_g, mem_norm_g, w_mem_kv, mem_q_norm_g, mem_k_norm_g, w_out, w_mlp_in, w_mlp_out, a_w_in, a_b_glu, a_dw, a_dw_b, a_ln_g, a_ln_b, b_w_in, b_gate_b, b_q_norm_g, kv_norm_g, w_kv, k_norm_g, cmp_pe_k, cmp_pe_v, cmp_w1_k, cmp_w2_k, cmp_w1_v, cmp_w2_v):
    batch, s, _ = x.shape
    assert batch == 1 and s % ONEHOT_SPAN == 0 and s >= WIN_KEYS and (KEY_TILE // Q_BLOCK) % N_STREAMS == 0 and N_STREAMS % (KEY_CHUNK // Q_BLOCK) == 0
    assert w_out.shape[0] == 2 and a_w_in.shape[0] == 1 and b_w_in.shape[0] == 1
    nc = s // CMP_STRIDE
    row = lambda v: v.reshape(1, -1)
    bd = _block_diag_ones()
    hm = _head_masks()

    mkT, mv = _memkv(mem[0], row(mem_norm_g), w_mem_kv.astype(BF16),
                     jnp.tile(mem_k_norm_g, (1, MEM_HEADS))[:, None, :], bd)

    conv0, memo0 = _layer0(
        x[0], row(norm_mix_g[0]), a_w_in[0].astype(BF16), row(a_b_glu[0]), a_dw[0], row(a_dw_b[0]),
        row(a_ln_g[0]), row(a_ln_b[0]), row(jnp.tile(mem_q_norm_g[0], MEM_HEADS)), mkT, mv, bd, hm)
    w_out_bf = w_out.astype(BF16)
    w_in_bf = w_mlp_in.astype(BF16)
    w_o_bf = w_mlp_out.astype(BF16)
    x1 = _post(x[0], conv0, memo0, w_out_bf, row(norm_mlp_g[0]), w_in_bf, w_o_bf, 0)

    src = _gate_layout()
    used = src >= 0
    w_gate = jnp.where(used[None, :], b_w_in[0][:, NSA_W + MEM_W + np.maximum(src, 0)], 0.0)
    b_gate = jnp.where(used, b_gate_b[0][np.maximum(src, 0)], 0.0)
    w_in1 = jnp.concatenate([b_w_in[0][:, :NSA_W + MEM_W], w_gate], axis=1).astype(BF16)
    q_gain = jnp.tile(b_q_norm_g[0], NSA_HEADS) * (QK_SCALE * LOG2E)
    kng = jnp.stack([jnp.tile(k_norm_g[1], GROUPS), jnp.tile(k_norm_g[2], GROUPS)])
    qT, gT, ksel, kwin, vselT, vwinT, kvc, memo1 = _layer1_proj(
        x1, row(kv_norm_g), row(norm_mix_g[1]), w_kv.astype(BF16), w_in1, row(b_gate), row(q_gain),
        row(jnp.tile(mem_q_norm_g[1], MEM_HEADS)), kng, mkT, mv, bd, hm, _onehot_pattern())

    t = kvc.reshape(2 * GROUPS, nc, CMP_STRIDE * HEAD_DIM)
    w1 = jnp.stack([cmp_w1_k, cmp_w1_v]).astype(BF16)
    w2 = jnp.stack([cmp_w2_k, cmp_w2_v]).astype(BF16)
    pe = jnp.stack([cmp_pe_k.reshape(1, -1), cmp_pe_v.reshape(1, -1)])
    cmp_rows, cmp_cols = _compress(t, w1, w2, w2.transpose(0, 2, 1), pe, row(k_norm_g[0]),
                                   k_norm_g[0].reshape(-1, 1))

    nsa = _nsa(qT, gT, cmp_rows, cmp_cols, ksel, vselT, kwin, vwinT)
    x2 = _post(x1, nsa, memo1, w_out_bf, row(norm_mlp_g[1]), w_in_bf, w_o_bf, 1)
    return x2[None]
```

```python
import functools

import numpy as np
import jax
import jax.numpy as jnp
from jax import lax
from jax.experimental import pallas as pl
from jax.experimental.pallas import tpu as pltpu

F32 = jnp.float32
BF16 = jnp.bfloat16

D_MODEL = 1024
HEAD_DIM = 64
MEM_LEN = 256
MEM_HEADS = 4
MEM_W = MEM_HEADS * HEAD_DIM
CONV_W = D_MODEL - MEM_W
CONV_K = 31
NSA_HEADS = CONV_W // HEAD_DIM
NSA_W = NSA_HEADS * HEAD_DIM
GROUPS = 2
HEADS_PER_GROUP = NSA_HEADS // GROUPS
CMP_L = 32
CMP_STRIDE = 16
CMP_HID = 256
SEL_L = 64
N_SEL = 16
WIN = 512
Q_BLOCK = 128
D_FF = 4 * D_MODEL
GATE_PAD = 128
GATE_ROWS = GATE_PAD // GROUPS

ROW_TILE = 512
FF_CHUNK = 1024
KEY_TILE = 1024
ONEHOT_SPAN = 1024
CMP_PATHS = 4
SEL_PATHS = 4
N_STREAMS = 4
KEY_CHUNK = 256
SEL_PER_SPAN = ONEHOT_SPAN // SEL_L
WIN_KEYS = WIN + Q_BLOCK
CONV_HALO = 32
CONV_ROWS = 64
VAUG_ROWS = 80
VMEM_LIMIT = 56 * 1024 * 1024
MXU_WIDTH = 256

EPS = 1e-6
LN_EPS = 1e-5
LOG2E = 1.4426950408889634
QK_SCALE = HEAD_DIM ** -0.5
MASK_BIAS = -2.0 ** 126
M_INIT = -2.0 ** 100


def _dot(a, b):
    return jnp.dot(a, b, preferred_element_type=F32)


def _dot_3tiles(a, b):
    two = 2 * MXU_WIDTH
    half = a.shape[0] // 2
    return jnp.concatenate([
        _dot(a, b[:, 0:two]),
        jnp.concatenate([_dot(a[0:half], b[:, two:]), _dot(a[half:], b[:, two:])], axis=0),
    ], axis=1)


def _sigmoid(x):
    return 0.5 * jnp.tanh(0.5 * x) + 0.5


def _rms_scale(x):
    return lax.rsqrt(jnp.mean(x * x, axis=-1, keepdims=True) + EPS)


def _head_meansq(x, bd):
    x2 = x * x
    hi = x2.astype(BF16)
    lo = (x2 - hi.astype(F32)).astype(BF16)
    return (_dot(hi, bd) + _dot(lo, bd)) * (1.0 / HEAD_DIM)


def _mem_attention(qn, mkT, mv, hm_ref):
    out = jnp.zeros(qn.shape, F32)
    for h in range(MEM_HEADS):
        hm = hm_ref[h:h + 1, :]
        s = _dot((qn * hm).astype(BF16), mkT)
        e = jnp.exp2(s - jnp.max(s, axis=-1, keepdims=True))
        l = jnp.sum(e, axis=-1, keepdims=True)
        out = out + _dot(e.astype(BF16), mv) * (hm / l)
    return out


def _memkv_kernel(mem_ref, g_ref, w_ref, kg_ref, bd_ref, mkT_ref, mv_ref):
    m = mem_ref[...]
    mn = (m * _rms_scale(m) * g_ref[...]).astype(BF16)
    kv = _dot(mn, w_ref[0])
    k = kv[:, :MEM_W]
    kn = k * lax.rsqrt(_head_meansq(k, bd_ref[...]) + EPS) * kg_ref[0]
    mkT_ref[0] = (kn * (QK_SCALE * LOG2E)).T.astype(BF16)
    mv_ref[0] = kv[:, MEM_W:].astype(BF16)


def _memkv(mem, g, w, kg, bd):
    depth = w.shape[0]
    return pl.pallas_call(
        _memkv_kernel,
        grid=(depth,),
        in_specs=[
            pl.BlockSpec((MEM_LEN, D_MODEL), lambda l: (0, 0)),
            pl.BlockSpec((1, D_MODEL), lambda l: (0, 0)),
            pl.BlockSpec((1, D_MODEL, 2 * MEM_W), lambda l: (l, 0, 0)),
            pl.BlockSpec((1, 1, MEM_W), lambda l: (l, 0, 0)),
            pl.BlockSpec((MEM_W, MEM_W), lambda l: (0, 0)),
        ],
        out_specs=[
            pl.BlockSpec((1, MEM_W, MEM_LEN), lambda l: (l, 0, 0)),
            pl.BlockSpec((1, MEM_LEN, MEM_W), lambda l: (l, 0, 0)),
        ],
        out_shape=[
            jax.ShapeDtypeStruct((depth, MEM_W, MEM_LEN), BF16),
            jax.ShapeDtypeStruct((depth, MEM_LEN, MEM_W), BF16),
        ],
        name="mem_kv",
    )(mem, g, w, kg, bd)


def _layer0_kernel(x_ref, g_ref, win_ref, bglu_ref, dw_ref, dwb_ref, lng_ref, lnb_ref, qg_ref,
                   mkT_ref, mv_ref, bd_ref, hm_ref, conv_ref, memo_ref, buf_ref, cv_ref, sh_ref):
    i = pl.program_id(0)
    tm = x_ref.shape[0]
    x = x_ref[...]
    h = (x * _rms_scale(x) * g_ref[...]).astype(BF16)
    u = _dot(h, win_ref[...])
    a = u[:, :CONV_W] + bglu_ref[:, :CONV_W]
    gate = u[:, CONV_W:2 * CONV_W] + bglu_ref[:, CONV_W:]
    v = a * _sigmoid(gate)

    @pl.when(i == 0)
    def _():
        buf_ref[0:CONV_HALO, :] = jnp.zeros((CONV_HALO, CONV_W), F32)

    buf_ref[CONV_HALO:CONV_HALO + tm, :] = v

    base = CONV_HALO - (CONV_K - 1)
    rows = tm + CONV_HALO
    for cb in range(CONV_W // 128):
        cs = slice(cb * 128, (cb + 1) * 128)
        xb = buf_ref[:, cs]
        sh_ref[0] = xb
        for b in range(1, 8):
            sh_ref[b] = pltpu.roll(xb, rows - b, axis=0)

        def conv_rows(r, carry, cs=cs):
            r0 = pl.multiple_of(r * CONV_ROWS, CONV_ROWS)
            acc = jnp.zeros((CONV_ROWS, 128), F32) + dwb_ref[:, cs]
            for k in range(CONV_K):
                a, b = divmod(base + k, 8)
                acc = acc + dw_ref[k:k + 1, cs] * sh_ref[b, pl.ds(r0 + 8 * a, CONV_ROWS), :]
            cv_ref[pl.ds(r0, CONV_ROWS), cs] = acc
            return carry

        lax.fori_loop(0, tm // CONV_ROWS, conv_rows, 0)
    buf_ref[0:CONV_HALO, :] = buf_ref[tm:tm + CONV_HALO, :]

    cv = cv_ref[...]
    mu = jnp.mean(cv, axis=-1, keepdims=True)
    d = cv - mu
    var = jnp.mean(d * d, axis=-1, keepdims=True)
    y = d * lax.rsqrt(var + LN_EPS) * lng_ref[...] + lnb_ref[...]
    conv_ref[...] = (y * _sigmoid(y)).astype(BF16)

    qm = u[:, 2 * CONV_W:]
    qn = qm * lax.rsqrt(_head_meansq(qm, bd_ref[...]) + EPS) * qg_ref[...]
    memo_ref[...] = _mem_attention(qn, mkT_ref[0], mv_ref[0], hm_ref).astype(BF16)


def _layer0(x, g, w_in, bglu, dw, dwb, lng, lnb, qg, mkT, mv, bd, hm):
    s = x.shape[0]
    tm = ROW_TILE
    const = lambda i: (0, 0)
    return pl.pallas_call(
        _layer0_kernel,
        grid=(s // tm,),
        in_specs=[
            pl.BlockSpec((tm, D_MODEL), lambda i: (i, 0)),
            pl.BlockSpec((1, D_MODEL), const),
            pl.BlockSpec(w_in.shape, const),
            pl.BlockSpec((1, 2 * CONV_W), const),
            pl.BlockSpec((CONV_K, CONV_W), const),
            pl.BlockSpec((1, CONV_W), const),
            pl.BlockSpec((1, CONV_W), const),
            pl.BlockSpec((1, CONV_W), const),
            pl.BlockSpec((1, MEM_W), const),
            pl.BlockSpec((1, MEM_W, MEM_LEN), lambda i: (0, 0, 0)),
            pl.BlockSpec((1, MEM_LEN, MEM_W), lambda i: (0, 0, 0)),
            pl.BlockSpec((MEM_W, MEM_W), const),
            pl.BlockSpec((MEM_HEADS, MEM_W), const),
        ],
        out_specs=[
            pl.BlockSpec((tm, CONV_W), lambda i: (i, 0)),
            pl.BlockSpec((tm, MEM_W), lambda i: (i, 0)),
        ],
        out_shape=[
            jax.ShapeDtypeStruct((s, CONV_W), BF16),
            jax.ShapeDtypeStruct((s, MEM_W), BF16),
        ],
        scratch_shapes=[
            pltpu.VMEM((tm + CONV_HALO, CONV_W), F32),
            pltpu.VMEM((tm, CONV_W), F32),
            pltpu.VMEM((8, tm + CONV_HALO, 128), F32),
        ],
        compiler_params=pltpu.CompilerParams(
            dimension_semantics=("arbitrary",), vmem_limit_bytes=VMEM_LIMIT),
        name="layer0_mixer",
    )(x, g, w_in, bglu, dw, dwb, lng, lnb, qg, mkT, mv, bd, hm)


def _post_kernel(x_ref, ma_ref, mb_ref, wa_ref, wb_ref, g_ref, win_ref, wout_ref, o_ref):
    o_ref[...] = x_ref[...] + _dot(ma_ref[...], wa_ref[0]) + _dot(mb_ref[...], wb_ref[0])
    x1 = o_ref[...]
    h = (x1 * _rms_scale(x1) * g_ref[...]).astype(BF16)
    for c in range(D_FF // FF_CHUNK):
        cs = slice(c * FF_CHUNK, (c + 1) * FF_CHUNK)
        t = jnp.maximum(_dot(h, win_ref[0, :, cs]), 0.0)
        o_ref[...] += _dot((t * t).astype(BF16), wout_ref[0, cs, :])


def _post(x, ma, mb, w_o, g, w_in, w_out, layer):
    s = x.shape[0]
    tm = ROW_TILE
    const = lambda i: (0, 0)
    this = lambda i: (layer, 0, 0)
    single = pl.Buffered(1)
    return pl.pallas_call(
        _post_kernel,
        grid=(s // tm,),
        in_specs=[
            pl.BlockSpec((tm, D_MODEL), lambda i: (i, 0)),
            pl.BlockSpec((tm, CONV_W), lambda i: (i, 0)),
            pl.BlockSpec((tm, MEM_W), lambda i: (i, 0)),
            pl.BlockSpec((1, CONV_W, D_MODEL), this, pipeline_mode=single),
            pl.BlockSpec((1, MEM_W, D_MODEL), lambda i: (layer, CONV_W // MEM_W, 0), pipeline_mode=single),
            pl.BlockSpec((1, D_MODEL), const),
            pl.BlockSpec((1, D_MODEL, D_FF), this, pipeline_mode=single),
            pl.BlockSpec((1, D_FF, D_MODEL), this, pipeline_mode=single),
        ],
        out_specs=pl.BlockSpec((tm, D_MODEL), lambda i: (i, 0)),
        out_shape=jax.ShapeDtypeStruct((s, D_MODEL), F32),
        compiler_params=pltpu.CompilerParams(
            dimension_semantics=("arbitrary",), vmem_limit_bytes=VMEM_LIMIT),
        name="outproj_mlp",
    )(x, ma, mb, w_o, w_o, g, w_in, w_out)


def _layer1_proj_kernel(x_ref, gkv_ref, gmix_ref, wkv_ref, win_ref, gb_ref, qg_ref, mqg_ref, kng_ref,
                        mkT_ref, mv_ref, bd_ref, hm_ref, pat_ref,
                        qT_ref, gT_ref, ksel_ref, kwin_ref, vselT_ref, vwinT_ref, kvc_ref, memo_ref):
    tm = x_ref.shape[0]
    x = x_ref[...]
    xn = x * _rms_scale(x)
    bd = bd_ref[...]

    kv = _dot((xn * gkv_ref[...]).astype(BF16), wkv_ref[...])
    for i in range(2 * GROUPS):
        kvc_ref[i] = kv[:, i * HEAD_DIM:(i + 1) * HEAD_DIM].astype(BF16)
    k2 = kv[:, 256:384]
    v2 = kv[:, 384:512]
    kw = kv[:, 512:640]
    vw = kv[:, 640:768]
    bd2 = bd[0:128, 0:128]
    k2n = k2 * lax.rsqrt(_head_meansq(k2, bd2) + EPS) * kng_ref[0:1, :]
    kwn = kw * lax.rsqrt(_head_meansq(kw, bd2) + EPS) * kng_ref[1:2, :]

    lane = lax.broadcasted_iota(jnp.int32, (tm, 128), 1)
    pat = pat_ref[...]
    ksel_ref[0] = jnp.where(lane < HEAD_DIM, k2n, pat).astype(BF16)
    ksel_ref[1] = jnp.where(lane < HEAD_DIM, pltpu.roll(k2n, HEAD_DIM, axis=1), pat).astype(BF16)
    kwin_ref[0] = kwn[:, :HEAD_DIM].astype(BF16)
    kwin_ref[1] = kwn[:, HEAD_DIM:].astype(BF16)

    ones_rows = (lax.broadcasted_iota(jnp.int32, (VAUG_ROWS - HEAD_DIM, tm), 0) == 0).astype(BF16)
    v2T = v2.T.astype(BF16)
    vwT = vw.T.astype(BF16)
    for g in range(GROUPS):
        vselT_ref[g, 0:HEAD_DIM, :] = v2T[g * HEAD_DIM:(g + 1) * HEAD_DIM, :]
        vselT_ref[g, HEAD_DIM:VAUG_ROWS, :] = ones_rows
        vwinT_ref[g, 0:HEAD_DIM, :] = vwT[g * HEAD_DIM:(g + 1) * HEAD_DIM, :]
        vwinT_ref[g, HEAD_DIM:VAUG_ROWS, :] = ones_rows

    u = _dot((xn * gmix_ref[...]).astype(BF16), win_ref[...])
    for cb in range(NSA_W // 256):
        cs = slice(cb * 256, (cb + 1) * 256)
        qc = u[:, cs]
        qn = qc * lax.rsqrt(_head_meansq(qc, bd) + EPS) * qg_ref[:, cs]
        qT_ref[4 * cb:4 * cb + 4] = qn.T.astype(BF16).reshape(4, HEAD_DIM, tm)

    gates = _sigmoid(u[:, NSA_W + MEM_W:] + gb_ref[...])
    gT_ref[...] = gates.T

    qm = u[:, NSA_W:NSA_W + MEM_W]
    qmn = qm * lax.rsqrt(_head_meansq(qm, bd) + EPS) * mqg_ref[...]
    memo_ref[...] = _mem_attention(qmn, mkT_ref[0], mv_ref[0], hm_ref).astype(BF16)


def _layer1_proj(x, gkv, gmix, wkv, w_in, gb, qg, mqg, kng, mkT, mv, bd, hm, pat):
    s = x.shape[0]
    tm = ROW_TILE
    const = lambda i: (0, 0)
    pat_blocks = pat.shape[0] // tm
    return pl.pallas_call(
        _layer1_proj_kernel,
        grid=(s // tm,),
        in_specs=[
            pl.BlockSpec((tm, D_MODEL), lambda i: (i, 0)),
            pl.BlockSpec((1, D_MODEL), const),
            pl.BlockSpec((1, D_MODEL), const),
            pl.BlockSpec(wkv.shape, const),
            pl.BlockSpec(w_in.shape, const),
            pl.BlockSpec((1, GATE_PAD), const),
            pl.BlockSpec((1, NSA_W), const),
            pl.BlockSpec((1, MEM_W), const),
            pl.BlockSpec((2, 128), const),
            pl.BlockSpec((1, MEM_W, MEM_LEN), lambda i: (1, 0, 0)),
            pl.BlockSpec((1, MEM_LEN, MEM_W), lambda i: (1, 0, 0)),
            pl.BlockSpec((MEM_W, MEM_W), const),
            pl.BlockSpec((MEM_HEADS, MEM_W), const),
            pl.BlockSpec((tm, 128), lambda i: (i % pat_blocks, 0)),
        ],
        out_specs=[
            pl.BlockSpec((NSA_HEADS, HEAD_DIM, tm), lambda i: (0, 0, i)),
            pl.BlockSpec((GATE_PAD, tm), lambda i: (0, i)),
            pl.BlockSpec((GROUPS, tm, 128), lambda i: (0, i, 0)),
            pl.BlockSpec((GROUPS, tm, HEAD_DIM), lambda i: (0, i, 0)),
            pl.BlockSpec((GROUPS, VAUG_ROWS, tm), lambda i: (0, 0, i)),
            pl.BlockSpec((GROUPS, VAUG_ROWS, tm), lambda i: (0, 0, i)),
            pl.BlockSpec((2 * GROUPS, tm, HEAD_DIM), lambda i: (0, i, 0)),
            pl.BlockSpec((tm, MEM_W), lambda i: (i, 0)),
        ],
        out_shape=[
            jax.ShapeDtypeStruct((NSA_HEADS, HEAD_DIM, s), BF16),
            jax.ShapeDtypeStruct((GATE_PAD, s), F32),
            jax.ShapeDtypeStruct((GROUPS, s, 128), BF16),
            jax.ShapeDtypeStruct((GROUPS, s, HEAD_DIM), BF16),
            jax.ShapeDtypeStruct((GROUPS, VAUG_ROWS, s), BF16),
            jax.ShapeDtypeStruct((GROUPS, VAUG_ROWS, s), BF16),
            jax.ShapeDtypeStruct((2 * GROUPS, s, HEAD_DIM), BF16),
            jax.ShapeDtypeStruct((s, MEM_W), BF16),
        ],
        compiler_params=pltpu.CompilerParams(
            dimension_semantics=("arbitrary",), vmem_limit_bytes=VMEM_LIMIT),
        name="layer1_proj",
    )(x, gkv, gmix, wkv, w_in, gb, qg, mqg, kng, mkT, mv, bd, hm, pat)


def _compress_kernel(t_ref, w1_ref, w2_ref, w2T_ref, pe_ref, kg_ref, kgT_ref, o_ref, oT_ref, hb_ref):
    nc = t_ref.shape[1]
    half = CMP_STRIDE * HEAD_DIM
    t = t_ref[0]
    ha = _dot(t, w1_ref[0, 0:half, :])
    hb_ref[0:nc, :] = _dot(t, w1_ref[0, half:2 * half, :])
    hb_ref[nc:nc + 8, :] = jnp.zeros((8, CMP_HID), F32)
    pe = jnp.broadcast_to(pe_ref[0], (8, 2 * half)).astype(BF16)
    pe_term = _dot(pe, w1_ref[0])[0:1, :]
    h = ha + hb_ref[1:nc + 1, :] + pe_term
    hg = 0.5 * h * (1.0 + jnp.tanh(0.7978845608028654 * (h + 0.044715 * (h * h * h))))
    hg = hg.astype(BF16)
    o = _dot(hg, w2_ref[0])
    oT = lax.dot_general(w2T_ref[0], hg, (((1,), (1,)), ((), ())), preferred_element_type=F32)
    is_k = pl.program_id(0) < GROUPS
    on = o * lax.rsqrt(jnp.mean(o * o, axis=1, keepdims=True) + EPS) * kg_ref[...]
    oTn = oT * lax.rsqrt(jnp.mean(oT * oT, axis=0, keepdims=True) + EPS) * kgT_ref[...]
    o_ref[0] = jnp.where(is_k, on, o).astype(BF16)
    oT_ref[0, 0:HEAD_DIM, :] = jnp.where(is_k, oTn, oT).astype(BF16)
    oT_ref[0, HEAD_DIM:VAUG_ROWS, :] = (
        lax.broadcasted_iota(jnp.int32, (VAUG_ROWS - HEAD_DIM, nc), 0) == 0).astype(BF16)


def _compress(t, w1, w2, w2T, pe, kg, kgT):
    n, nc, width = t.shape
    return pl.pallas_call(
        _compress_kernel,
        grid=(n,),
        in_specs=[
            pl.BlockSpec((1, nc, width), lambda i: (i, 0, 0)),
            pl.BlockSpec((1, 2 * width, CMP_HID), lambda i: (i // GROUPS, 0, 0)),
            pl.BlockSpec((1, CMP_HID, HEAD_DIM), lambda i: (i // GROUPS, 0, 0)),
            pl.BlockSpec((1, HEAD_DIM, CMP_HID), lambda i: (i // GROUPS, 0, 0)),
            pl.BlockSpec((1, 1, 2 * width), lambda i: (i // GROUPS, 0, 0)),
            pl.BlockSpec((1, HEAD_DIM), lambda i: (0, 0)),
            pl.BlockSpec((HEAD_DIM, 1), lambda i: (0, 0)),
        ],
        out_specs=[
            pl.BlockSpec((1, nc, HEAD_DIM), lambda i: (i, 0, 0)),
            pl.BlockSpec((1, VAUG_ROWS, nc), lambda i: (i, 0, 0)),
        ],
        out_shape=[
            jax.ShapeDtypeStruct((n, nc, HEAD_DIM), BF16),
            jax.ShapeDtypeStruct((n, VAUG_ROWS, nc), BF16),
        ],
        scratch_shapes=[pltpu.VMEM((nc + 8, CMP_HID), F32)],
        compiler_params=pltpu.CompilerParams(
            dimension_semantics=("arbitrary",), vmem_limit_bytes=VMEM_LIMIT),
        name="compress_kv",
    )(t, w1, w2, w2T, pe, kg, kgT)


def _nsa_kernel(qT_ref, gT_ref, kc_ref, vcT_ref, ksel_ref, vselT_ref, kwin_ref, vwinT_ref, wmask_ref, out_ref,
                qaug_st, psum_st, bias_st, acc_st, m_st, s_st, oc_st, sw_st, ow_st, pick_st):
    cc = pl.program_id(1)
    nc = kc_ref.shape[1]
    nsb = bias_st.shape[1]
    hg = HEADS_PER_GROUP
    width = hg * Q_BLOCK
    n_chunks = KEY_TILE // KEY_CHUNK
    last = (cc * N_STREAMS) // (KEY_TILE // Q_BLOCK)

    def load_queries(st):
        qaug_ref = qaug_st.at[st]
        for hh in range(hg):
            qaug_ref[0:HEAD_DIM, hh * Q_BLOCK:(hh + 1) * Q_BLOCK] = qT_ref[hh, :, st * Q_BLOCK:(st + 1) * Q_BLOCK]
        qaug_ref[HEAD_DIM:128, :] = jnp.zeros((128 - HEAD_DIM, width), BF16)

    def query_pos(st):
        return (cc * N_STREAMS + st) * Q_BLOCK + (lax.broadcasted_iota(jnp.int32, (1, width), 1) & (Q_BLOCK - 1))

    for st in range(N_STREAMS):
        load_queries(st)

    def cmp_branch(rows):
        for st in range(N_STREAMS):
            qT = qaug_st[st, 0:HEAD_DIM, :]
            t_q = query_pos(st)
            oc_ref, psum_ref = oc_st.at[st], psum_st.at[st]
            s = _dot_3tiles(kc_ref[0, 0:rows, :], qT)
            seen = max(rows - 2 * cuts[0], 0)
            ci = seen + lax.broadcasted_iota(jnp.int32, (rows - seen, 1), 0)
            tail = jnp.where(ci * CMP_STRIDE + (CMP_L - 1) <= t_q, s[seen:, :], -jnp.inf)
            s = jnp.concatenate([s[0:seen, :], tail], axis=0) if seen else tail
            m = jnp.max(s, axis=0, keepdims=True)
            m = jnp.where(m == -jnp.inf, 0.0, m)
            e = jnp.exp2(s - m)
            oa = _dot(vcT_ref[0, :, 0:rows], e.astype(BF16))
            rl = 1.0 / jnp.maximum(oa[HEAD_DIM:HEAD_DIM + 1, :], 1e-30)
            oc_ref[...] = oa[0:HEAD_DIM, :] * rl
            p = e * rl
            psum = p[:, 0:Q_BLOCK]
            for hh in range(1, hg):
                psum = psum + p[:, hh * Q_BLOCK:(hh + 1) * Q_BLOCK]
            psum_ref[8:8 + rows, :] = psum

    @pl.when(cc == 0)
    def _():
        psum_st[...] = jnp.zeros(psum_st.shape, F32)

    cuts = [nc * k // CMP_PATHS for k in range(1, CMP_PATHS + 1)] if nc % (128 * CMP_PATHS) == 0 else [nc]
    need = (cc + 1) * N_STREAMS * (Q_BLOCK // CMP_STRIDE)
    for k, rows in enumerate(cuts):
        lo = cuts[k - 1] if k else 0
        pl.when((need > lo) & (need <= rows))(functools.partial(cmp_branch, rows))

    def block_validity(st, rows):
        blk = lax.broadcasted_iota(jnp.int32, (rows, Q_BLOCK), 0)
        tb = ((cc * N_STREAMS + st) * Q_BLOCK + lax.broadcasted_iota(jnp.int32, (1, Q_BLOCK), 1)) >> 6
        return blk, tb, blk <= tb

    def select_blocks(rows):
        for st in range(N_STREAMS):
            psum_ref = psum_st.at[st]
            imp = (psum_ref[pl.ds(7, rows, stride=4), :] + psum_ref[pl.ds(11, rows, stride=4), :]
                   + 2.0 * (psum_ref[pl.ds(8, rows, stride=4), :] + psum_ref[pl.ds(9, rows, stride=4), :]
                            + psum_ref[pl.ds(10, rows, stride=4), :]))
            blk, tb, valid = block_validity(st, rows)
            forced = (blk == 0) | (blk == tb) | (blk == tb - 1)
            pick_st[st, 0:rows, :] = jnp.where(forced | ~valid, -jnp.inf, imp)
        blk_f = lax.broadcasted_iota(jnp.int32, (rows, Q_BLOCK), 0).astype(F32)

        def pick(_, carry):
            for st in range(N_STREAMS):
                sc = pick_st[st, 0:rows, :]
                mx = jnp.max(sc, axis=0, keepdims=True)
                first = jnp.min(jnp.where(sc == mx, blk_f, float(rows)), axis=0, keepdims=True)
                pick_st[st, 0:rows, :] = jnp.where(blk_f == first, -jnp.inf, sc)
            return carry

        lax.fori_loop(0, N_SEL - 3, pick, 0)
        for st in range(N_STREAMS):
            sel = (pick_st[st, 0:rows, :] == -jnp.inf) & block_validity(st, rows)[2]
            bias_st[st, 0:rows, :] = jnp.where(sel, 0.0, MASK_BIAS).astype(BF16)

    @pl.when(cc == 0)
    def _():
        bias_st[...] = jnp.full(bias_st.shape, MASK_BIAS, BF16)

    b_cuts = [nsb * k // SEL_PATHS for k in range(1, SEL_PATHS + 1)] if nsb % (16 * SEL_PATHS) == 0 else [nsb]
    need_b = 2 * (cc + 1) * N_STREAMS
    for k, rows in enumerate(b_cuts):
        lo = b_cuts[k - 1] if k else 0
        pl.when((need_b > lo) & (need_b <= rows))(functools.partial(select_blocks, rows))

    def stream(st):
        c = cc * N_STREAMS + st
        qs = slice(st * Q_BLOCK, (st + 1) * Q_BLOCK)
        qaug_ref, psum_ref, bias_ref, acc_ref = qaug_st.at[st], psum_st.at[st], bias_st.at[st], acc_st.at[st]
        m_ref, s_ref, oc_ref, sw_ref, ow_ref = m_st.at[st], s_st.at[st], oc_st.at[st], sw_st.at[st], ow_st.at[st]
        qT = qaug_ref[0:HEAD_DIM, :]
        t_q = query_pos(st)

        row0 = pl.multiple_of(jnp.maximum(c * Q_BLOCK - WIN, 0), Q_BLOCK)
        sw = _dot_3tiles(kwin_ref[0, pl.ds(row0, WIN_KEYS), :], qT)
        sw = sw + wmask_ref[pl.ds(pl.multiple_of(row0 - c * Q_BLOCK + WIN, Q_BLOCK), WIN_KEYS), :]
        sw_ref[...] = sw
        mw = jnp.max(jnp.max(sw.reshape(WIN_KEYS // 8, 8, width), axis=0), axis=0, keepdims=True)
        mw = jnp.where(mw == -jnp.inf, 0.0, mw)

        acc_ref[...] = jnp.zeros(acc_ref.shape, F32)
        m_ref[...] = jnp.full(m_ref.shape, M_INIT, F32)

        def set_bias(j):
            span = j // (ONEHOT_SPAN // KEY_TILE)
            b16 = bias_ref[pl.ds(pl.multiple_of(span * SEL_PER_SPAN, SEL_PER_SPAN), SEL_PER_SPAN), :]
            for hh in range(hg):
                qaug_ref[HEAD_DIM:HEAD_DIM + SEL_PER_SPAN, hh * Q_BLOCK:(hh + 1) * Q_BLOCK] = b16

        def scores(j, r):
            k0 = pl.multiple_of(j * KEY_TILE + r * KEY_CHUNK, KEY_CHUNK)
            return _dot(ksel_ref[0, pl.ds(k0, KEY_CHUNK), :], qaug_ref[...])

        def chunk_max(cmax, sc):
            return jnp.maximum(cmax, jnp.max(sc.reshape(KEY_CHUNK // 8, 8, width), axis=0))

        def new_max(cmax):
            m_old = m_ref[...]
            m_new = jnp.maximum(m_old, jnp.max(cmax, axis=0, keepdims=True))
            m_ref[...] = m_new
            return m_new, jnp.exp2(m_old - m_new)

        def values(j, r):
            k0 = pl.multiple_of(j * KEY_TILE + r * KEY_CHUNK, KEY_CHUNK)
            return vselT_ref[0, :, pl.ds(k0, KEY_CHUNK)]

        cmax0 = jnp.full((8, width), MASK_BIAS, F32)
        set_bias(0)
        cmax = cmax0
        ow = jnp.zeros((VAUG_ROWS, width), F32)
        w_cuts = [WIN_KEYS * r // n_chunks // 128 * 128 for r in range(n_chunks)] + [WIN_KEYS]
        for r in range(n_chunks):
            sc = _dot_3tiles(ksel_ref[0, r * KEY_CHUNK:(r + 1) * KEY_CHUNK, :], qaug_ref[...])
            ws = slice(w_cuts[r], w_cuts[r + 1])
            pw = jnp.exp2(sw_ref[ws, :] - mw).astype(BF16)
            s_ref[r * KEY_CHUNK:(r + 1) * KEY_CHUNK, :] = sc
            cmax = chunk_max(cmax, sc)
            ow = ow + _dot(vwinT_ref[0, :, pl.ds(pl.multiple_of(row0 + w_cuts[r], 128), w_cuts[r + 1] - w_cuts[r])], pw)
        ow_ref[...] = ow[0:HEAD_DIM, :] / jnp.maximum(ow[HEAD_DIM:HEAD_DIM + 1, :], 1e-30)

        def pipe_step(j, cmax):
            m_new, alpha = new_max(cmax)
            set_bias(j + 1)
            acc = acc_ref[...] * alpha
            cnext = cmax0
            for r in range(n_chunks):
                rs = slice(r * KEY_CHUNK, (r + 1) * KEY_CHUNK)
                sc = scores(j + 1, r)
                pj = jnp.exp2(s_ref[rs, :] - m_new).astype(BF16)
                s_ref[rs, :] = sc
                cnext = chunk_max(cnext, sc)
                acc = acc + _dot(values(j, r), pj)
            acc_ref[...] = acc
            return cnext

        def mask_diagonal():
            diag = pl.ds(pl.multiple_of((c % (KEY_TILE // Q_BLOCK)) * Q_BLOCK, Q_BLOCK), Q_BLOCK)
            ku = lax.broadcasted_iota(jnp.int32, (Q_BLOCK, 1), 0)
            s_ref[diag, :] = jnp.where(ku <= t_q - c * Q_BLOCK, s_ref[diag, :], MASK_BIAS)

        def finish_tile(rows):
            s_last = s_ref[0:rows, :]
            m_new, alpha = new_max(jnp.max(s_last.reshape(rows // 8, 8, width), axis=0))
            p_last = jnp.exp2(s_last - m_new).astype(BF16)
            acc_ref[...] = acc_ref[...] * alpha + _dot(
                vselT_ref[0, :, pl.ds(pl.multiple_of(last * KEY_TILE, KEY_TILE), rows)], p_last)

        def emit():
            o_cmp = oc_ref[...]
            o_win = ow_ref[...]
            o_slc = acc_ref[0:HEAD_DIM, :] / jnp.maximum(acc_ref[HEAD_DIM:HEAD_DIM + 1, :], 1e-30)

            heads = []
            for hh in range(hg):
                hs = slice(hh * Q_BLOCK, (hh + 1) * Q_BLOCK)
                heads.append(o_cmp[:, hs] * gT_ref[hh:hh + 1, qs] + o_slc[:, hs] * gT_ref[8 + hh:9 + hh, qs]
                             + o_win[:, hs] * gT_ref[16 + hh:17 + hh, qs])
            for pr in range(hg // 2):
                pair = jnp.concatenate([heads[2 * pr], heads[2 * pr + 1]], axis=0)
                out_ref[qs, pr * 128:(pr + 1) * 128] = pair.T.astype(BF16)

        return cmax, pipe_step, mask_diagonal, finish_tile, emit

    streams = [stream(st) for st in range(N_STREAMS)]

    def sweep_steps(j, cmaxes):
        return tuple(s[1](j, cm) for s, cm in zip(streams, cmaxes))

    lax.fori_loop(0, last, sweep_steps, tuple(s[0] for s in streams))
    for s in streams:
        s[2]()
    per = KEY_CHUNK // Q_BLOCK
    for first in range(0, N_STREAMS, per):
        diag_chunk = ((cc * N_STREAMS + first) % (KEY_TILE // Q_BLOCK)) // per

        def finish_group(rows, first=first):
            for s in streams[first:first + per]:
                s[3](rows)

        for k in range(n_chunks):
            pl.when(diag_chunk == k)(functools.partial(finish_group, (k + 1) * KEY_CHUNK))
    for s in streams:
        s[4]()


def _nsa(qT, gT, kc, vcT, ksel, vselT, kwin, vwinT, wmask):
    s = ksel.shape[1]
    nc = kc.shape[1]
    nsb = s // SEL_L
    hg = HEADS_PER_GROUP
    ns = N_STREAMS
    qb = ns * Q_BLOCK
    width = hg * Q_BLOCK
    grp = lambda g, c: (g, 0, 0)
    once = pl.Buffered(1)
    return pl.pallas_call(
        _nsa_kernel,
        grid=(GROUPS, s // qb),
        in_specs=[
            pl.BlockSpec((hg, HEAD_DIM, qb), lambda g, c: (g, 0, c)),
            pl.BlockSpec((GATE_ROWS, qb), lambda g, c: (g, c)),
            pl.BlockSpec((1, nc, HEAD_DIM), grp, pipeline_mode=once),
            pl.BlockSpec((1, VAUG_ROWS, nc), lambda g, c: (GROUPS + g, 0, 0), pipeline_mode=once),
            pl.BlockSpec((1, s, 128), grp, pipeline_mode=once),
            pl.BlockSpec((1, VAUG_ROWS, s), grp, pipeline_mode=once),
            pl.BlockSpec((1, s, HEAD_DIM), grp, pipeline_mode=once),
            pl.BlockSpec((1, VAUG_ROWS, s), grp, pipeline_mode=once),
            pl.BlockSpec(wmask.shape, lambda g, c: (0, 0), pipeline_mode=once),
        ],
        out_specs=pl.BlockSpec((qb, hg * HEAD_DIM), lambda g, c: (c, g)),
        out_shape=jax.ShapeDtypeStruct((s, NSA_W), BF16),
        scratch_shapes=[
            pltpu.VMEM((ns, 128, width), BF16),
            pltpu.VMEM((ns, nc + 16, Q_BLOCK), F32),
            pltpu.VMEM((ns, nsb, Q_BLOCK), BF16),
            pltpu.VMEM((ns, VAUG_ROWS, width), F32),
            pltpu.VMEM((ns, 1, width), F32),
            pltpu.VMEM((ns, KEY_TILE, width), F32),
            pltpu.VMEM((ns, HEAD_DIM, width), F32),
            pltpu.VMEM((ns, WIN_KEYS, width), F32),
            pltpu.VMEM((ns, HEAD_DIM, width), F32),
            pltpu.VMEM((ns, nsb, Q_BLOCK), F32),
        ],
        compiler_params=pltpu.CompilerParams(
            dimension_semantics=("arbitrary", "arbitrary"), vmem_limit_bytes=VMEM_LIMIT),
        name="nsa_attention",
    )(qT, gT, kc, vcT, ksel, vselT, kwin, vwinT, wmask)


def _block_diag_ones():
    idx = np.arange(MEM_W) // HEAD_DIM
    return jnp.asarray((idx[:, None] == idx[None, :]).astype(np.float32), BF16)


def _head_masks():
    idx = np.arange(MEM_W) // HEAD_DIM
    return jnp.asarray((idx[None, :] == np.arange(MEM_HEADS)[:, None]).astype(np.float32))


def _onehot_pattern():
    rows = np.arange(ONEHOT_SPAN)[:, None] // SEL_L
    lanes = np.arange(128)[None, :] - HEAD_DIM
    return jnp.asarray((rows == lanes).astype(np.float32))


def _window_bias():
    r = np.arange(WIN_KEYS + WIN)[:, None] - WIN
    q = np.arange(HEADS_PER_GROUP * Q_BLOCK)[None, :] % Q_BLOCK
    return jnp.asarray(np.where((r <= q) & (r > q - WIN), 0.0, -np.inf).astype(np.float32))


def _gate_layout():
    src = np.full((GATE_PAD,), -1, np.int64)
    for g in range(GROUPS):
        for b in range(3):
            for hh in range(HEADS_PER_GROUP):
                src[g * GATE_ROWS + b * 8 + hh] = (g * HEADS_PER_GROUP + hh) * 3 + b
    return src


def kernel(x, mem, norm_mix_g, norm_mlp_g, mem_norm_g, w_mem_kv, mem_q_norm_g, mem_k_norm_g, w_out, w_mlp_in, w_mlp_out, a_w_in, a_b_glu, a_dw, a_dw_b, a_ln_g, a_ln_b, b_w_in, b_gate_b, b_q_norm_g, kv_norm_g, w_kv, k_norm_g, cmp_pe_k, cmp_pe_v, cmp_w1_k, cmp_w2_k, cmp_w1_v, cmp_w2_v):
    batch, s, _ = x.shape
    assert batch == 1 and s % ONEHOT_SPAN == 0 and s >= WIN_KEYS and (KEY_TILE // Q_BLOCK) % N_STREAMS == 0 and N_STREAMS % (KEY_CHUNK // Q_BLOCK) == 0
    assert w_out.shape[0] == 2 and a_w_in.shape[0] == 1 and b_w_in.shape[0] == 1
    nc = s // CMP_STRIDE
    row = lambda v: v.reshape(1, -1)
    bd = _block_diag_ones()
    hm = _head_masks()

    mkT, mv = _memkv(mem[0], row(mem_norm_g), w_mem_kv.astype(BF16),
                     jnp.tile(mem_k_norm_g, (1, MEM_HEADS))[:, None, :], bd)

    conv0, memo0 = _layer0(
        x[0], row(norm_mix_g[0]), a_w_in[0].astype(BF16), row(a_b_glu[0]), a_dw[0], row(a_dw_b[0]),
        row(a_ln_g[0]), row(a_ln_b[0]), row(jnp.tile(mem_q_norm_g[0], MEM_HEADS)), mkT, mv, bd, hm)
    w_out_bf = w_out.astype(BF16)
    w_in_bf = w_mlp_in.astype(BF16)
    w_o_bf = w_mlp_out.astype(BF16)
    x1 = _post(x[0], conv0, memo0, w_out_bf, row(norm_mlp_g[0]), w_in_bf, w_o_bf, 0)

    src = _gate_layout()
    used = src >= 0
    w_gate = jnp.where(used[None, :], b_w_in[0][:, NSA_W + MEM_W + np.maximum(src, 0)], 0.0)
    b_gate = jnp.where(used, b_gate_b[0][np.maximum(src, 0)], 0.0)
    w_in1 = jnp.concatenate([b_w_in[0][:, :NSA_W + MEM_W], w_gate], axis=1).astype(BF16)
    q_gain = jnp.tile(b_q_norm_g[0], NSA_HEADS) * (QK_SCALE * LOG2E)
    kng = jnp.stack([jnp.tile(k_norm_g[1], GROUPS), jnp.tile(k_norm_g[2], GROUPS)])
    qT, gT, ksel, kwin, vselT, vwinT, kvc, memo1 = _layer1_proj(
        x1, row(kv_norm_g), row(norm_mix_g[1]), w_kv.astype(BF16), w_in1, row(b_gate), row(q_gain),
        row(jnp.tile(mem_q_norm_g[1], MEM_HEADS)), kng, mkT, mv, bd, hm, _onehot_pattern())

    t = kvc.reshape(2 * GROUPS, nc, CMP_STRIDE * HEAD_DIM)
    w1 = jnp.stack([cmp_w1_k, cmp_w1_v]).astype(BF16)
    w2 = jnp.stack([cmp_w2_k, cmp_w2_v]).astype(BF16)
    pe = jnp.stack([cmp_pe_k.reshape(1, -1), cmp_pe_v.reshape(1, -1)])
    cmp_rows, cmp_cols = _compress(t, w1, w2, w2.transpose(0, 2, 1), pe, row(k_norm_g[0]),
                                   k_norm_g[0].reshape(-1, 1))

    nsa = _nsa(qT, gT, cmp_rows, cmp_cols, ksel, vselT, kwin, vwinT, _window_bias())
    x2 = _post(x1, nsa, memo1, w_out_bf, row(norm_mlp_g[1]), w_in_bf, w_o_bf, 1)
    return x2[None]
```
